```python
import math
import jax, jax.numpy as jnp
from jax import lax
import numpy as np

D_MODEL = 1024
BATCH = 2
SEQ = 8192
DEPTH = 4
DEC_BATCH = 128
DEC_SEQ = 8
PAST_LEN = 2048
PAGE_SIZE = 128

N_MIXERS = 3
N_GDN = (DEPTH + 2) // N_MIXERS
N_NSA = (DEPTH + 1) // N_MIXERS
N_DIFF = DEPTH // N_MIXERS
HEAD_DIM = 128
GDN_HEADS = D_MODEL // HEAD_DIM
GDN_DK = HEAD_DIM
GDN_DV = HEAD_DIM
CONV_W = 4
GDN_CHUNK = 64
NSA_HEADS = D_MODEL // HEAD_DIM
NSA_KV_GROUPS = 2
CMP_BLOCK = 32
SEL_BLOCK = 64
TOP_N = 16
WINDOW = 512
Q_BLOCK = 128
DIFF_HEADS = D_MODEL // HEAD_DIM
DIFF_DH = D_MODEL // DIFF_HEADS // 2
PEER_HEADS = 8
N_KEYS = 128
N_EXPERTS = N_KEYS * N_KEYS
PEER_TOPK = 16
PEER_QDIM = 128
PEER_TOKEN_BLOCK = 256
ROPE_THETA = 10000.0
EPS = 1e-6

kernel_name = 'hybrid_gdn_nsa_diff_peer_step'


def rmsnorm(x, g):
    xf = x.astype(jnp.float32)
    y = xf * lax.rsqrt(jnp.mean(xf * xf, -1, keepdims=True) + EPS)
    return (y * g.astype(jnp.float32)).astype(x.dtype)


def l2norm(x):
    xf = x.astype(jnp.float32)
    return (xf * lax.rsqrt(jnp.sum(xf * xf, -1, keepdims=True) + EPS)).astype(x.dtype)


def rope(x, pos):
    half = x.shape[-1] // 2
    inv = ROPE_THETA ** (-jnp.arange(half, dtype=jnp.float32) / half)
    ang = pos.astype(jnp.float32)[:, None] * inv[None, :]
    cos = jnp.cos(ang)[:, None, :]
    sin = jnp.sin(ang)[:, None, :]
    xf = x.astype(jnp.float32)
    x1, x2 = xf[..., :half], xf[..., half:]
    return jnp.concatenate([x1 * cos - x2 * sin, x2 * cos + x1 * sin], -1).astype(x.dtype)


def masked_softmax(s, mask):
    s = jnp.where(mask, s.astype(jnp.float32), -jnp.inf)
    m = jnp.max(s, -1, keepdims=True)
    m = jnp.where(jnp.isfinite(m), m, 0.0)
    p = jnp.exp(s - m)
    return p / jnp.maximum(jnp.sum(p, -1, keepdims=True), 1e-30)


def gather_pages(pool, page_table):
    g = pool[page_table]
    return g.reshape(g.shape[0], g.shape[1] * g.shape[2], *pool.shape[2:])


def causal_conv(x, buf, w):
    T = x.shape[1]
    xx = jnp.concatenate([buf, x], 1)
    y = xx[:, 0:T] * w[0]
    for j in range(1, CONV_W):
        y = y + xx[:, j:j + T] * w[j]
    return jax.nn.silu(y), xx[:, -(CONV_W - 1):]


def gated_delta_rule(q, k, v, g, beta, S0):
    f32 = jnp.float32
    B, T, H, dk = q.shape
    dv = v.shape[-1]
    C = GDN_CHUNK
    Tp = -(-T // C) * C

    def prep(a):
        a = a.astype(f32)
        a = jnp.pad(a, [(0, 0), (0, Tp - T)] + [(0, 0)] * (a.ndim - 2))
        a = a.reshape(B, Tp // C, C, *a.shape[2:])
        return jnp.moveaxis(a, (1, 3), (0, 2))

    qc = prep(q) * dk ** -0.5
    kc, vc, gc, bc = prep(k), prep(v), prep(g), prep(beta)
    gcum = jnp.cumsum(gc, -1)
    idx = jnp.arange(C)
    incl = idx[:, None] >= idx[None, :]
    strict = idx[:, None] > idx[None, :]
    decay = jnp.exp(jnp.where(incl, gcum[..., :, None] - gcum[..., None, :], -jnp.inf))
    kb = kc * bc[..., None]
    L = jnp.where(strict, jnp.einsum('nbhid,nbhjd->nbhij', kb, kc) * decay, 0.0)
    eye = jnp.eye(C, dtype=f32)
    A = eye + L
    Tinv = lax.linalg.triangular_solve(A, jnp.broadcast_to(eye, A.shape), left_side=True, lower=True, unit_diagonal=True)
    u = Tinv @ (vc * bc[..., None])
    w = Tinv @ (kb * jnp.exp(gcum)[..., None])
    a_intra = jnp.einsum('nbhid,nbhjd->nbhij', qc, kc) * decay

    def step(S, xs):
        q_i, k_i, u_i, w_i, a_i, g_i = xs
        v_new = u_i - w_i @ S
        o = (q_i * jnp.exp(g_i)[..., None]) @ S + a_i @ v_new
        g_last = g_i[..., -1:]
        S = S * jnp.exp(g_last)[..., None] + jnp.einsum('bhck,bhcv->bhkv', k_i * jnp.exp(g_last - g_i)[..., None], v_new)
        return S, o

    S, o = lax.scan(step, S0.astype(f32), (qc, kc, u, w, a_intra, gcum))
    o = jnp.moveaxis(o, (0, 2), (1, 3)).reshape(B, Tp, H, dv)[:, :T]
    return o, S


def gdn_mixer(h, conv_buf, S0, w_in, conv_w, a_log, dt_bias, norm_g, w_out):
    B, T, _ = h.shape
    H, dk, dv = GDN_HEADS, GDN_DK, GDN_DV
    n_qkv = H * (2 * dk + dv)
    f32 = jnp.float32
    proj = h @ w_in
    qkv, conv_new = causal_conv(proj[..., :n_qkv], conv_buf, conv_w)
    z = proj[..., n_qkv:n_qkv + H * dv].reshape(B, T, H, dv)
    b_raw = proj[..., n_qkv + H * dv:n_qkv + H * dv + H]
    a_raw = proj[..., n_qkv + H * dv + H:]
    q = l2norm(qkv[..., :H * dk].reshape(B, T, H, dk))
    k = l2norm(qkv[..., H * dk:2 * H * dk].reshape(B, T, H, dk))
    v = qkv[..., 2 * H * dk:].reshape(B, T, H, dv)
    beta = jax.nn.sigmoid(b_raw.astype(f32))
    g = -jnp.exp(a_log.astype(f32)) * jax.nn.softplus(a_raw.astype(f32) + dt_bias.astype(f32))
    o, S = gated_delta_rule(q, k, v, g, beta, S0)
    o = rmsnorm(o.astype(h.dtype), norm_g) * jax.nn.silu(z)
    return o.reshape(B, T, D_MODEL) @ w_out, conv_new, S


def nsa_seq(q_s, g_s, full_s, win_s, pos0, wpos0, cmp_pos, cmp_w):
    dt = q_s.dtype
    T = q_s.shape[0]
    G, R, dh = NSA_KV_GROUPS, NSA_HEADS // NSA_KV_GROUPS, HEAD_DIM
    Tk = full_s.shape[0]
    Tkp = -(-Tk // SEL_BLOCK) * SEL_BLOCK
    full_p = jnp.pad(full_s, ((0, Tkp - Tk), (0, 0), (0, 0), (0, 0)))
    n_cmp = Tkp // CMP_BLOCK
    n_sel = Tkp // SEL_BLOCK
    blocks = full_p[:, :2].reshape(n_cmp, CMP_BLOCK, 2, G, dh) + jnp.transpose(cmp_pos, (1, 0, 2))[:, :, None, :]
    kv_cmp = jnp.einsum('nlsgd,slde->nsge', blocks, cmp_w)
    k_cmp, v_cmp = kv_cmp[:, 0], kv_cmp[:, 1]
    cmp_end = (jnp.arange(n_cmp) + 1) * CMP_BLOCK - 1
    sel = jnp.transpose(full_p[:, 2:].reshape(n_sel, SEL_BLOCK, 2, G, dh), (2, 3, 0, 1, 4))
    k_selb, v_selb = sel[0], sel[1]
    n_top = min(TOP_N, n_sel)
    win_p = jnp.pad(win_s, ((WINDOW, 0), (0, 0), (0, 0), (0, 0)))
    qb = Q_BLOCK if T % Q_BLOCK == 0 else T
    nqb = T // qb
    lw = WINDOW + qb - 1
    scale = HEAD_DIM ** -0.5
    g_idx = jnp.arange(G)[None, :, None]
    blk_ids = jnp.arange(n_sel)

    def block(args):
        qi, gi, bi = args
        qstart = pos0 + bi * qb
        qpos = qstart + jnp.arange(qb)
        qg = qi.reshape(qb, G, R, dh) * scale
        s_c = jnp.einsum('qgrd,ngd->qgrn', qg, k_cmp)
        p_c = masked_softmax(s_c, (cmp_end[None, :] <= qpos[:, None])[:, None, None, :])
        o_c = jnp.einsum('qgrn,ngd->qgrd', p_c.astype(dt), v_cmp)
        imp = p_c.sum(2).reshape(qb, G, n_sel, SEL_BLOCK // CMP_BLOCK).sum(-1)
        cur = (qpos // SEL_BLOCK)[:, None] == blk_ids[None, :]
        causal_blk = (blk_ids * SEL_BLOCK)[None, :] <= qpos[:, None]
        imp = jnp.where(cur[:, None], jnp.inf, jnp.where(causal_blk[:, None], imp, -jnp.inf))
        top_v, top_i = lax.top_k(imp, n_top)
        k_sel = k_selb[g_idx, top_i]
        v_sel = v_selb[g_idx, top_i]
        kpos = top_i[..., None] * SEL_BLOCK + jnp.arange(SEL_BLOCK)
        m_s = (kpos <= qpos[:, None, None, None]) & (top_v > -jnp.inf)[..., None]
        s_s = jnp.einsum('qgrd,qgksd->qgrks', qg, k_sel).reshape(qb, G, R, n_top * SEL_BLOCK)
        p_s = masked_softmax(s_s, m_s.reshape(qb, G, 1, n_top * SEL_BLOCK))
        o_s = jnp.einsum('qgrm,qgmd->qgrd', p_s.astype(dt), v_sel.reshape(qb, G, n_top * SEL_BLOCK, dh))
        wblk = lax.dynamic_slice_in_dim(win_p, qstart - wpos0 + 1, lw, axis=0)
        wpos = qstart - WINDOW + 1 + jnp.arange(lw)
        m_w = (wpos[None] <= qpos[:, None]) & (wpos[None] > qpos[:, None] - WINDOW) & (wpos[None] >= wpos0)
        s_w = jnp.einsum('qgrd,kgd->qgrk', qg, wblk[:, 0])
        p_w = masked_softmax(s_w, m_w[:, None, None, :])
        o_w = jnp.einsum('qgrk,kgd->qgrd', p_w.astype(dt), wblk[:, 1])
        gg = gi.reshape(qb, G, R, 3)
        o = gg[..., 0:1] * o_c + gg[..., 1:2] * o_s + gg[..., 2:3] * o_w
        return o.reshape(qb, NSA_HEADS, dh)

    out = lax.map(block, (q_s.reshape(nqb, qb, NSA_HEADS, dh), g_s.reshape(nqb, qb, NSA_HEADS, 3), jnp.arange(nqb)))
    return out.reshape(T, NSA_HEADS, dh)


def nsa_mixer(h, pos0, past, wbuf, w_in, cmp_pos, cmp_w, w_out):
    B, T, _ = h.shape
    G, dh = NSA_KV_GROUPS, HEAD_DIM
    qd = NSA_HEADS * dh
    kvd = 6 * G * dh
    pos = pos0 + jnp.arange(T, dtype=jnp.int32)
    proj = h @ w_in
    q = rope(proj[..., :qd].reshape(B, T, NSA_HEADS, dh), pos)
    kv = proj[..., qd:qd + kvd].reshape(B, T, 6, G, dh)
    gates = jax.nn.sigmoid(proj[..., qd + kvd:].reshape(B, T, NSA_HEADS, 3))
    keys = rope(kv[:, :, 0::2].reshape(B, T, 3 * G, dh), pos).reshape(B, T, 3, G, dh)
    kv = jnp.stack([keys[:, :, 0], kv[:, :, 1], keys[:, :, 1], kv[:, :, 3], keys[:, :, 2], kv[:, :, 5]], axis=2)
    new_rows = kv[:, :, :4]
    if past is None:
        full = new_rows
        win = kv[:, :, 4:]
        wpos0 = 0
    else:
        full = jnp.concatenate([past, new_rows], 1)
        win = jnp.concatenate([wbuf, kv[:, :, 4:]], 1)
        wpos0 = pos0 - wbuf.shape[1]
    o = lax.map(lambda a: nsa_seq(a[0], a[1], a[2], a[3], pos0, wpos0, cmp_pos, cmp_w), (q, gates, full, win))
    new_win = win[:, -min(WINDOW, win.shape[1]):]
    return o.reshape(B, T, D_MODEL) @ w_out, new_rows, new_win


def diff_mixer(h, pos0, past, layer_idx, w_in, lq1, lk1, lq2, lk2, subln_g, w_out):
    B, T, _ = h.shape
    H, dd = DIFF_HEADS, DIFF_DH
    f32 = jnp.float32
    dt = h.dtype
    pos = pos0 + jnp.arange(T, dtype=jnp.int32)
    proj = h @ w_in
    q = rope(proj[..., :D_MODEL].reshape(B, T, 2 * H, dd), pos)
    k = rope(proj[..., D_MODEL:2 * D_MODEL].reshape(B, T, 2 * H, dd), pos)
    v = proj[..., 2 * D_MODEL:].reshape(B, T, 2 * H, dd)
    new_rows = jnp.stack([k, v], 2)
    kv = new_rows if past is None else jnp.concatenate([past, new_rows], 1)
    Tk = kv.shape[1]
    kk = kv[:, :, 0].reshape(B, Tk, H, 2, dd)
    vv = kv[:, :, 1].reshape(B, Tk, H, 2 * dd)
    lam_init = 0.8 - 0.6 * math.exp(-0.3 * layer_idx)
    lam = (jnp.exp(jnp.sum(lq1.astype(f32) * lk1.astype(f32))) - jnp.exp(jnp.sum(lq2.astype(f32) * lk2.astype(f32))) + lam_init)
    qg = q.reshape(B, T, H, 2, dd) * dd ** -0.5
    qb = Q_BLOCK if T % Q_BLOCK == 0 else T
    nqb = T // qb
    kpos = jnp.arange(Tk)

    def block(args):
        qi, bi = args
        qpos = pos0 + bi * qb + jnp.arange(qb)
        s = jnp.einsum('bqhcd,bkhcd->bhcqk', qi, kk)
        p = masked_softmax(s, kpos[None, :] <= qpos[:, None])
        a = p[:, :, 0] - lam * p[:, :, 1]
        return jnp.einsum('bhqk,bkhe->bqhe', a.astype(dt), vv)

    o = lax.map(block, (jnp.moveaxis(qg.reshape(B, nqb, qb, H, 2, dd), 1, 0), jnp.arange(nqb)))
    o = jnp.moveaxis(o, 0, 1).reshape(B, T, H, 2 * dd)
    o = rmsnorm(o, subln_g) * (1.0 - lam_init)
    return o.reshape(B, T, D_MODEL) @ w_out, new_rows


def peer(h, w_q, k1, k2, u_tab, v_tab):
    B, T, D = h.shape
    dt = h.dtype
    n = B * T
    tb = PEER_TOKEN_BLOCK
    n_pad = -(-n // tb) * tb
    x = jnp.pad(h.reshape(n, D), ((0, n_pad - n), (0, 0)))
    half = PEER_QDIM // 2

    def block(xb):
        q = (xb @ w_q).reshape(tb, PEER_HEADS, 2, half)
        s1 = jnp.einsum('thd,hnd->thn', q[:, :, 0], k1).astype(jnp.float32)
        s2 = jnp.einsum('thd,hnd->thn', q[:, :, 1], k2).astype(jnp.float32)
        v1, i1 = lax.top_k(s1, PEER_TOPK)
        v2, i2 = lax.top_k(s2, PEER_TOPK)
        cand = (v1[..., :, None] + v2[..., None, :]).reshape(tb, PEER_HEADS, PEER_TOPK * PEER_TOPK)
        vals, ci = lax.top_k(cand, PEER_TOPK)
        e = jnp.take_along_axis(i1, ci // PEER_TOPK, -1) * N_KEYS + jnp.take_along_axis(i2, ci % PEER_TOPK, -1)
        gate = jax.nn.softmax(vals, -1)
        act = jax.nn.gelu(jnp.einsum('td,thkd->thk', xb, u_tab[e]).astype(jnp.float32), approximate=False)
        return jnp.einsum('thk,thkd->td', (gate * act).astype(dt), v_tab[e])

    out = lax.map(block, x.reshape(n_pad // tb, tb, D))
    return out.reshape(n_pad, D)[:n].reshape(B, T, D)


def setup_inputs(seed: int = 0) -> dict:
    key = jax.random.key(seed)
    keys = list(jax.random.split(key, 48))

    def nrm(shape, scale):
        return jax.random.normal(keys.pop(), shape, jnp.float32) * scale

    D = D_MODEL
    G, dh = NSA_KV_GROUPS, HEAD_DIM
    n_pages = PAST_LEN // PAGE_SIZE
    n_pool = (DEC_BATCH * n_pages * 5) // 4
    w_buf = min(WINDOW, PAST_LEN)
    c_qkv = GDN_HEADS * (2 * GDN_DK + GDN_DV)
    inp = {}
    inp['x_prompt'] = nrm((BATCH, SEQ, D), 1.0)
    inp['x_sample'] = nrm((DEC_BATCH, DEC_SEQ, D), 1.0)
    inp['cache_nsa_kv'] = nrm((N_NSA, n_pool, PAGE_SIZE, 4, G, dh), 1.0)
    inp['cache_diff_kv'] = nrm((N_DIFF, n_pool, PAGE_SIZE, 2, 2 * DIFF_HEADS, DIFF_DH), 1.0)
    inp['state_nsa_window'] = nrm((N_NSA, DEC_BATCH, w_buf, 2, G, dh), 1.0)
    inp['state_gdn_S'] = nrm((N_GDN, DEC_BATCH, GDN_HEADS, GDN_DK, GDN_DV), 0.05)
    inp['state_gdn_conv'] = nrm((N_GDN, DEC_BATCH, CONV_W - 1, c_qkv), 1.0)
    perm = jax.random.permutation(keys.pop(), n_pool)
    inp['page_table'] = perm[:DEC_BATCH * n_pages].reshape(DEC_BATCH, n_pages).astype(jnp.int32)
    inp['c_prompt'] = nrm((BATCH, D), 1.0)
    inp['c_sample'] = nrm((DEC_BATCH, D), 1.0)
    inp['ada_w'] = nrm((DEPTH, D, 6 * D), 0.5 * D ** -0.5)
    inp['ada_b'] = nrm((DEPTH, 6 * D), 0.02)
    inp['norm_mix_g'] = 1.0 + nrm((DEPTH, D), 0.05)
    inp['norm_ffn_g'] = 1.0 + nrm((DEPTH, D), 0.05)
    inp['final_norm_g'] = 1.0 + nrm((D,), 0.05)
    inp['gdn_w_in'] = nrm((N_GDN, D, c_qkv + GDN_HEADS * GDN_DV + 2 * GDN_HEADS), D ** -0.5)
    inp['gdn_conv_w'] = nrm((N_GDN, CONV_W, c_qkv), CONV_W ** -0.5)
    inp['gdn_a_log'] = jnp.log(jax.random.uniform(keys.pop(), (N_GDN, GDN_HEADS), jnp.float32, 1.0, 16.0))
    dt0 = jnp.exp(jax.random.uniform(keys.pop(), (N_GDN, GDN_HEADS), jnp.float32, math.log(1e-3), math.log(1e-1)))
    inp['gdn_dt_bias'] = dt0 + jnp.log(-jnp.expm1(-dt0))
    inp['gdn_norm_g'] = 1.0 + nrm((N_GDN, GDN_DV), 0.05)
    inp['gdn_w_out'] = nrm((N_GDN, D, D), D ** -0.5)
    inp['nsa_w_in'] = nrm((N_NSA, D, NSA_HEADS * dh + 6 * G * dh + 3 * NSA_HEADS), D ** -0.5)
    inp['nsa_cmp_pos'] = nrm((N_NSA, 2, CMP_BLOCK, dh), 0.1)
    inp['nsa_cmp_w'] = nrm((N_NSA, 2, CMP_BLOCK, dh, dh), (CMP_BLOCK * dh) ** -0.5)
    inp['nsa_w_out'] = nrm((N_NSA, D, D), D ** -0.5)
    inp['diff_w_in'] = nrm((N_DIFF, D, 3 * D), D ** -0.5)
    inp['diff_lq1'] = nrm((N_DIFF, DIFF_DH), 0.1)
    inp['diff_lk1'] = nrm((N_DIFF, DIFF_DH), 0.1)
    inp['diff_lq2'] = nrm((N_DIFF, DIFF_DH), 0.1)
    inp['diff_lk2'] = nrm((N_DIFF, DIFF_DH), 0.1)
    inp['diff_subln_g'] = 1.0 + nrm((N_DIFF, 2 * DIFF_DH), 0.05)
    inp['diff_w_out'] = nrm((N_DIFF, D, D), D ** -0.5)
    inp['peer_w_q'] = nrm((DEPTH, D, PEER_HEADS * PEER_QDIM), D ** -0.5)
    inp['peer_k1'] = nrm((DEPTH, PEER_HEADS, N_KEYS, PEER_QDIM // 2), (PEER_QDIM // 2) ** -0.5)
    inp['peer_k2'] = nrm((DEPTH, PEER_HEADS, N_KEYS, PEER_QDIM // 2), (PEER_QDIM // 2) ** -0.5)
    inp['peer_u'] = nrm((DEPTH, N_EXPERTS, D), D ** -0.5)
    inp['peer_v'] = nrm((DEPTH, N_EXPERTS, D), PEER_HEADS ** -0.5)
    return inp


def reference(x_prompt, x_sample, cache_nsa_kv, cache_diff_kv, state_nsa_window, state_gdn_S, state_gdn_conv,
              page_table, c_prompt, c_sample, ada_w, ada_b, norm_mix_g, norm_ffn_g, final_norm_g,
              gdn_w_in, gdn_conv_w, gdn_a_log, gdn_dt_bias, gdn_norm_g, gdn_w_out,
              nsa_w_in, nsa_cmp_pos, nsa_cmp_w, nsa_w_out,
              diff_w_in, diff_lq1, diff_lk1, diff_lq2, diff_lk2, diff_subln_g, diff_w_out,
              peer_w_q, peer_k1, peer_k2, peer_u, peer_v):
    past_len = page_table.shape[1] * cache_nsa_kv.shape[2]

    def trunk(x, c, sample):
        B, T, _ = x.shape
        pos0 = past_len if sample else 0
        cs = jax.nn.silu(c)
        new_S, new_conv, new_nsa_kv, new_nsa_win, new_diff_kv = [], [], [], [], []
        for i in range(DEPTH):
            mod = (cs @ ada_w[i] + ada_b[i]).reshape(B, 6, 1, D_MODEL)
            h = rmsnorm(x, norm_mix_g[i]) * (1.0 + mod[:, 1]) + mod[:, 0]
            j = i // N_MIXERS
            kind = i % N_MIXERS
            if kind == 0:
                if sample:
                    S0 = state_gdn_S[j].astype(jnp.float32)
                    buf = state_gdn_conv[j]
                else:
                    S0 = jnp.zeros((B, GDN_HEADS, GDN_DK, GDN_DV), jnp.float32)
                    buf = jnp.zeros((B, CONV_W - 1, GDN_HEADS * (2 * GDN_DK + GDN_DV)), x.dtype)
                m, buf_n, S_n = gdn_mixer(h, buf, S0, gdn_w_in[j], gdn_conv_w[j], gdn_a_log[j], gdn_dt_bias[j], gdn_norm_g[j], gdn_w_out[j])
                new_S.append(S_n.astype(x.dtype))
                new_conv.append(buf_n)
            elif kind == 1:
                past = gather_pages(cache_nsa_kv[j], page_table) if sample else None
                wbuf = state_nsa_window[j] if sample else None
                m, kv_n, win_n = nsa_mixer(h, pos0, past, wbuf, nsa_w_in[j], nsa_cmp_pos[j], nsa_cmp_w[j], nsa_w_out[j])
                new_nsa_kv.append(kv_n)
                new_nsa_win.append(win_n)
            else:
                past = gather_pages(cache_diff_kv[j], page_table) if sample else None
                m, kv_n = diff_mixer(h, pos0, past, i, diff_w_in[j], diff_lq1[j], diff_lk1[j], diff_lq2[j], diff_lk2[j], diff_subln_g[j], diff_w_out[j])
                new_diff_kv.append(kv_n)
            x = x + mod[:, 2] * m
            h = rmsnorm(x, norm_ffn_g[i]) * (1.0 + mod[:, 4]) + mod[:, 3]
            x = x + mod[:, 5] * peer(h, peer_w_q[i], peer_k1[i], peer_k2[i], peer_u[i], peer_v[i])
        y = rmsnorm(x, final_norm_g)
        return y, jnp.stack(new_S), jnp.stack(new_conv), jnp.stack(new_nsa_kv), jnp.stack(new_nsa_win), jnp.stack(new_diff_kv)

    y_prompt, p_gdn_S, p_gdn_conv, p_nsa_kv, p_nsa_win, p_diff_kv = trunk(x_prompt, c_prompt, False)
    y_sample, s_gdn_S, s_gdn_conv, s_nsa_kv, s_nsa_win, s_diff_kv = trunk(x_sample, c_sample, True)
    return (y_prompt, y_sample, p_gdn_S, p_gdn_conv, p_nsa_kv, p_nsa_win, p_diff_kv, s_gdn_S, s_gdn_conv, s_nsa_kv, s_nsa_win, s_diff_kv)
```

```python
import functools
import math

import jax
import jax.numpy as jnp
from jax import lax
from jax.experimental import pallas as pl
from jax.experimental.pallas import tpu as pltpu

D_MODEL = 1024
DEPTH = 4
N_MIXERS = 3
HEAD_DIM = 128
GDN_HEADS = D_MODEL // HEAD_DIM
GDN_DK = HEAD_DIM
GDN_DV = HEAD_DIM
CONV_W = 4
GDN_CHUNK = 64
NSA_HEADS = D_MODEL // HEAD_DIM
NSA_KV_GROUPS = 2
CMP_BLOCK = 32
SEL_BLOCK = 64
TOP_N = 16
WINDOW = 512
Q_BLOCK = 128
DIFF_HEADS = D_MODEL // HEAD_DIM
DIFF_DH = D_MODEL // DIFF_HEADS // 2
PEER_HEADS = 8
N_KEYS = 128
N_EXPERTS = N_KEYS * N_KEYS
PEER_TOPK = 16
PEER_QDIM = 128
PEER_TOKEN_BLOCK = 256
ROPE_THETA = 10000.0
EPS = 1e-6

LANE = 128
VMEM_LIMIT = 48 * 1024 * 1024


def _mm_kernel(x_ref, w_ref, o_ref):
    o_ref[...] = jnp.dot(x_ref[...].astype(jnp.bfloat16), w_ref[...], preferred_element_type=jnp.float32)


def _mm(x, w, tm=256):
    lead = x.shape[:-1]
    K = x.shape[-1]
    N = w.shape[1]
    x2 = x.reshape(-1, K)
    M = x2.shape[0]
    n_pad = -(-N // 256) * 256
    wb = jnp.pad(w, ((0, 0), (0, n_pad - N))).astype(jnp.bfloat16)
    tm = min(tm, M)
    assert M % tm == 0
    out = pl.pallas_call(
        _mm_kernel,
        grid=(M // tm,),
        in_specs=[pl.BlockSpec((tm, K), lambda i: (i, 0)), pl.BlockSpec((K, n_pad), lambda i: (0, 0))],
        out_specs=pl.BlockSpec((tm, n_pad), lambda i: (i, 0)),
        out_shape=jax.ShapeDtypeStruct((M, n_pad), jnp.float32),
        compiler_params=pltpu.CompilerParams(dimension_semantics=("parallel",), vmem_limit_bytes=VMEM_LIMIT),
        name="proj_matmul",
    )(x2, wb)
    return out[:, :N].reshape(*lead, N)


def rmsnorm(x, g):
    xf = x.astype(jnp.float32)
    y = xf * lax.rsqrt(jnp.mean(xf * xf, -1, keepdims=True) + EPS)
    return (y * g.astype(jnp.float32)).astype(x.dtype)


def l2norm(x):
    xf = x.astype(jnp.float32)
    return (xf * lax.rsqrt(jnp.sum(xf * xf, -1, keepdims=True) + EPS)).astype(x.dtype)


def rope(x, pos):
    half = x.shape[-1] // 2
    inv = ROPE_THETA ** (-jnp.arange(half, dtype=jnp.float32) / half)
    ang = pos.astype(jnp.float32)[:, None] * inv[None, :]
    cos = jnp.cos(ang)[:, None, :]
    sin = jnp.sin(ang)[:, None, :]
    xf = x.astype(jnp.float32)
    x1, x2 = xf[..., :half], xf[..., half:]
    return jnp.concatenate([x1 * cos - x2 * sin, x2 * cos + x1 * sin], -1).astype(x.dtype)


def masked_softmax(s, mask):
    s = jnp.where(mask, s.astype(jnp.float32), -jnp.inf)
    m = jnp.max(s, -1, keepdims=True)
    m = jnp.where(jnp.isfinite(m), m, 0.0)
    p = jnp.exp(s - m)
    return p / jnp.maximum(jnp.sum(p, -1, keepdims=True), 1e-30)


def gather_pages(pool, page_table):
    g = pool[page_table]
    return g.reshape(g.shape[0], g.shape[1] * g.shape[2], *pool.shape[2:])


def causal_conv(x, buf, w):
    T = x.shape[1]
    xx = jnp.concatenate([buf, x], 1)
    y = xx[:, 0:T] * w[0]
    for j in range(1, CONV_W):
        y = y + xx[:, j:j + T] * w[j]
    return jax.nn.silu(y), xx[:, -(CONV_W - 1):]


def gated_delta_rule(q, k, v, g, beta, S0):
    f32 = jnp.float32
    B, T, H, dk = q.shape
    dv = v.shape[-1]
    C = GDN_CHUNK
    Tp = -(-T // C) * C

    def prep(a):
        a = a.astype(f32)
        a = jnp.pad(a, [(0, 0), (0, Tp - T)] + [(0, 0)] * (a.ndim - 2))
        a = a.reshape(B, Tp // C, C, *a.shape[2:])
        return jnp.moveaxis(a, (1, 3), (0, 2))

    qc = prep(q) * dk ** -0.5
    kc, vc, gc, bc = prep(k), prep(v), prep(g), prep(beta)
    gcum = jnp.cumsum(gc, -1)
    idx = jnp.arange(C)
    incl = idx[:, None] >= idx[None, :]
    strict = idx[:, None] > idx[None, :]
    decay = jnp.exp(jnp.where(incl, gcum[..., :, None] - gcum[..., None, :], -jnp.inf))
    kb = kc * bc[..., None]
    L = jnp.where(strict, jnp.einsum('nbhid,nbhjd->nbhij', kb, kc) * decay, 0.0)
    eye = jnp.eye(C, dtype=f32)
    A = eye + L
    Tinv = lax.linalg.triangular_solve(A, jnp.broadcast_to(eye, A.shape), left_side=True, lower=True, unit_diagonal=True)
    u = Tinv @ (vc * bc[..., None])
    w = Tinv @ (kb * jnp.exp(gcum)[..., None])
    a_intra = jnp.einsum('nbhid,nbhjd->nbhij', qc, kc) * decay

    def step(S, xs):
        q_i, k_i, u_i, w_i, a_i, g_i = xs
        v_new = u_i - w_i @ S
        o = (q_i * jnp.exp(g_i)[..., None]) @ S + a_i @ v_new
        g_last = g_i[..., -1:]
        S = S * jnp.exp(g_last)[..., None] + jnp.einsum('bhck,bhcv->bhkv', k_i * jnp.exp(g_last - g_i)[..., None], v_new)
        return S, o

    S, o = lax.scan(step, S0.astype(f32), (qc, kc, u, w, a_intra, gcum))
    o = jnp.moveaxis(o, (0, 2), (1, 3)).reshape(B, Tp, H, dv)[:, :T]
    return o, S


def gdn_mixer(h, conv_buf, S0, w_in, conv_w, a_log, dt_bias, norm_g, w_out):
    B, T, _ = h.shape
    H, dk, dv = GDN_HEADS, GDN_DK, GDN_DV
    n_qkv = H * (2 * dk + dv)
    f32 = jnp.float32
    proj = _mm(h, w_in)
    qkv, conv_new = causal_conv(proj[..., :n_qkv], conv_buf, conv_w)
    z = proj[..., n_qkv:n_qkv + H * dv].reshape(B, T, H, dv)
    b_raw = proj[..., n_qkv + H * dv:n_qkv + H * dv + H]
    a_raw = proj[..., n_qkv + H * dv + H:]
    q = l2norm(qkv[..., :H * dk].reshape(B, T, H, dk))
    k = l2norm(qkv[..., H * dk:2 * H * dk].reshape(B, T, H, dk))
    v = qkv[..., 2 * H * dk:].reshape(B, T, H, dv)
    beta = jax.nn.sigmoid(b_raw.astype(f32))
    g = -jnp.exp(a_log.astype(f32)) * jax.nn.softplus(a_raw.astype(f32) + dt_bias.astype(f32))
    o, S = gated_delta_rule(q, k, v, g, beta, S0)
    o = rmsnorm(o.astype(h.dtype), norm_g) * jax.nn.silu(z)
    return _mm(o.reshape(B, T, D_MODEL), w_out), conv_new, S


def nsa_seq(q_s, g_s, full_s, win_s, pos0, wpos0, cmp_pos, cmp_w):
    dt = q_s.dtype
    T = q_s.shape[0]
    G, R, dh = NSA_KV_GROUPS, NSA_HEADS // NSA_KV_GROUPS, HEAD_DIM
    Tk = full_s.shape[0]
    Tkp = -(-Tk // SEL_BLOCK) * SEL_BLOCK
    full_p = jnp.pad(full_s, ((0, Tkp - Tk), (0, 0), (0, 0), (0, 0)))
    n_cmp = Tkp // CMP_BLOCK
    n_sel = Tkp // SEL_BLOCK
    blocks = full_p[:, :2].reshape(n_cmp, CMP_BLOCK, 2, G, dh) + jnp.transpose(cmp_pos, (1, 0, 2))[:, :, None, :]
    kv_cmp = jnp.einsum('nlsgd,slde->nsge', blocks, cmp_w)
    k_cmp, v_cmp = kv_cmp[:, 0], kv_cmp[:, 1]
    cmp_end = (jnp.arange(n_cmp) + 1) * CMP_BLOCK - 1
    sel = jnp.transpose(full_p[:, 2:].reshape(n_sel, SEL_BLOCK, 2, G, dh), (2, 3, 0, 1, 4))
    k_selb, v_selb = sel[0], sel[1]
    n_top = min(TOP_N, n_sel)
    win_p = jnp.pad(win_s, ((WINDOW, 0), (0, 0), (0, 0), (0, 0)))
    qb = Q_BLOCK if T % Q_BLOCK == 0 else T
    nqb = T // qb
    lw = WINDOW + qb - 1
    scale = HEAD_DIM ** -0.5
    g_idx = jnp.arange(G)[None, :, None]
    blk_ids = jnp.arange(n_sel)

    def block(args):
        qi, gi, bi = args
        qstart = pos0 + bi * qb
        qpos = qstart + jnp.arange(qb)
        qg = qi.reshape(qb, G, R, dh) * scale
        s_c = jnp.einsum('qgrd,ngd->qgrn', qg, k_cmp)
        p_c = masked_softmax(s_c, (cmp_end[None, :] <= qpos[:, None])[:, None, None, :])
        o_c = jnp.einsum('qgrn,ngd->qgrd', p_c.astype(dt), v_cmp)
        imp = p_c.sum(2).reshape(qb, G, n_sel, SEL_BLOCK // CMP_BLOCK).sum(-1)
        cur = (qpos // SEL_BLOCK)[:, None] == blk_ids[None, :]
        causal_blk = (blk_ids * SEL_BLOCK)[None, :] <= qpos[:, None]
        imp = jnp.where(cur[:, None], jnp.inf, jnp.where(causal_blk[:, None], imp, -jnp.inf))
        top_v, top_i = lax.top_k(imp, n_top)
        k_sel = k_selb[g_idx, top_i]
        v_sel = v_selb[g_idx, top_i]
        kpos = top_i[..., None] * SEL_BLOCK + jnp.arange(SEL_BLOCK)
        m_s = (kpos <= qpos[:, None, None, None]) & (top_v > -jnp.inf)[..., None]
        s_s = jnp.einsum('qgrd,qgksd->qgrks', qg, k_sel).reshape(qb, G, R, n_top * SEL_BLOCK)
        p_s = masked_softmax(s_s, m_s.reshape(qb, G, 1, n_top * SEL_BLOCK))
        o_s = jnp.einsum('qgrm,qgmd->qgrd', p_s.astype(dt), v_sel.reshape(qb, G, n_top * SEL_BLOCK, dh))
        wblk = lax.dynamic_slice_in_dim(win_p, qstart - wpos0 + 1, lw, axis=0)
        wpos = qstart - WINDOW + 1 + jnp.arange(lw)
        m_w = (wpos[None] <= qpos[:, None]) & (wpos[None] > qpos[:, None] - WINDOW) & (wpos[None] >= wpos0)
        s_w = jnp.einsum('qgrd,kgd->qgrk', qg, wblk[:, 0])
        p_w = masked_softmax(s_w, m_w[:, None, None, :])
        o_w = jnp.einsum('qgrk,kgd->qgrd', p_w.astype(dt), wblk[:, 1])
        gg = gi.reshape(qb, G, R, 3)
        o = gg[..., 0:1] * o_c + gg[..., 1:2] * o_s + gg[..., 2:3] * o_w
        return o.reshape(qb, NSA_HEADS, dh)

    out = lax.map(block, (q_s.reshape(nqb, qb, NSA_HEADS, dh), g_s.reshape(nqb, qb, NSA_HEADS, 3), jnp.arange(nqb)))
    return out.reshape(T, NSA_HEADS, dh)


def nsa_mixer(h, pos0, past, wbuf, w_in, cmp_pos, cmp_w, w_out):
    B, T, _ = h.shape
    G, dh = NSA_KV_GROUPS, HEAD_DIM
    qd = NSA_HEADS * dh
    kvd = 6 * G * dh
    pos = pos0 + jnp.arange(T, dtype=jnp.int32)
    proj = _mm(h, w_in)
    q = rope(proj[..., :qd].reshape(B, T, NSA_HEADS, dh), pos)
    kv = proj[..., qd:qd + kvd].reshape(B, T, 6, G, dh)
    gates = jax.nn.sigmoid(proj[..., qd + kvd:].reshape(B, T, NSA_HEADS, 3))
    keys = rope(kv[:, :, 0::2].reshape(B, T, 3 * G, dh), pos).reshape(B, T, 3, G, dh)
    kv = jnp.stack([keys[:, :, 0], kv[:, :, 1], keys[:, :, 1], kv[:, :, 3], keys[:, :, 2], kv[:, :, 5]], axis=2)
    new_rows = kv[:, :, :4]
    if past is None:
        full = new_rows
        win = kv[:, :, 4:]
        wpos0 = 0
    else:
        full = jnp.concatenate([past, new_rows], 1)
        win = jnp.concatenate([wbuf, kv[:, :, 4:]], 1)
        wpos0 = pos0 - wbuf.shape[1]
    o = lax.map(lambda a: nsa_seq(a[0], a[1], a[2], a[3], pos0, wpos0, cmp_pos, cmp_w), (q, gates, full, win))
    new_win = win[:, -min(WINDOW, win.shape[1]):]
    return _mm(o.reshape(B, T, D_MODEL), w_out), new_rows, new_win


def diff_mixer(h, pos0, past, layer_idx, w_in, lq1, lk1, lq2, lk2, subln_g, w_out):
    B, T, _ = h.shape
    H, dd = DIFF_HEADS, DIFF_DH
    f32 = jnp.float32
    dt = h.dtype
    pos = pos0 + jnp.arange(T, dtype=jnp.int32)
    proj = _mm(h, w_in)
    q = rope(proj[..., :D_MODEL].reshape(B, T, 2 * H, dd), pos)
    k = rope(proj[..., D_MODEL:2 * D_MODEL].reshape(B, T, 2 * H, dd), pos)
    v = proj[..., 2 * D_MODEL:].reshape(B, T, 2 * H, dd)
    new_rows = jnp.stack([k, v], 2)
    kv = new_rows if past is None else jnp.concatenate([past, new_rows], 1)
    Tk = kv.shape[1]
    kk = kv[:, :, 0].reshape(B, Tk, H, 2, dd)
    vv = kv[:, :, 1].reshape(B, Tk, H, 2 * dd)
    lam_init = 0.8 - 0.6 * math.exp(-0.3 * layer_idx)
    lam = (jnp.exp(jnp.sum(lq1.astype(f32) * lk1.astype(f32))) - jnp.exp(jnp.sum(lq2.astype(f32) * lk2.astype(f32))) + lam_init)
    qg = q.reshape(B, T, H, 2, dd) * dd ** -0.5
    qb = Q_BLOCK if T % Q_BLOCK == 0 else T
    nqb = T // qb
    kpos = jnp.arange(Tk)

    def block(args):
        qi, bi = args
        qpos = pos0 + bi * qb + jnp.arange(qb)
        s = jnp.einsum('bqhcd,bkhcd->bhcqk', qi, kk)
        p = masked_softmax(s, kpos[None, :] <= qpos[:, None])
        a = p[:, :, 0] - lam * p[:, :, 1]
        return jnp.einsum('bhqk,bkhe->bqhe', a.astype(dt), vv)

    o = lax.map(block, (jnp.moveaxis(qg.reshape(B, nqb, qb, H, 2, dd), 1, 0), jnp.arange(nqb)))
    o = jnp.moveaxis(o, 0, 1).reshape(B, T, H, 2 * dd)
    o = rmsnorm(o, subln_g) * (1.0 - lam_init)
    return _mm(o.reshape(B, T, D_MODEL), w_out), new_rows


def peer(h, w_q, k1, k2, u_tab, v_tab):
    B, T, D = h.shape
    dt = h.dtype
    n = B * T
    tb = PEER_TOKEN_BLOCK
    n_pad = -(-n // tb) * tb
    x = jnp.pad(h.reshape(n, D), ((0, n_pad - n), (0, 0)))
    half = PEER_QDIM // 2
    qall = _mm(x, w_q)

    def block(args):
        xb, qb_ = args
        q = qb_.reshape(tb, PEER_HEADS, 2, half)
        s1 = jnp.einsum('thd,hnd->thn', q[:, :, 0], k1).astype(jnp.float32)
        s2 = jnp.einsum('thd,hnd->thn', q[:, :, 1], k2).astype(jnp.float32)
        v1, i1 = lax.top_k(s1, PEER_TOPK)
        v2, i2 = lax.top_k(s2, PEER_TOPK)
        cand = (v1[..., :, None] + v2[..., None, :]).reshape(tb, PEER_HEADS, PEER_TOPK * PEER_TOPK)
        vals, ci = lax.top_k(cand, PEER_TOPK)
        e = jnp.take_along_axis(i1, ci // PEER_TOPK, -1) * N_KEYS + jnp.take_along_axis(i2, ci % PEER_TOPK, -1)
        gate = jax.nn.softmax(vals, -1)
        act = jax.nn.gelu(jnp.einsum('td,thkd->thk', xb, u_tab[e]).astype(jnp.float32), approximate=False)
        return jnp.einsum('thk,thkd->td', (gate * act).astype(dt), v_tab[e])

    out = lax.map(block, (x.reshape(n_pad // tb, tb, D), qall.reshape(n_pad // tb, tb, -1)))
    return out.reshape(n_pad, D)[:n].reshape(B, T, D)


def kernel(x_prompt, x_sample, cache_nsa_kv, cache_diff_kv, state_nsa_window, state_gdn_S, state_gdn_conv,
           page_table, c_prompt, c_sample, ada_w, ada_b, norm_mix_g, norm_ffn_g, final_norm_g,
           gdn_w_in, gdn_conv_w, gdn_a_log, gdn_dt_bias, gdn_norm_g, gdn_w_out,
           nsa_w_in, nsa_cmp_pos, nsa_cmp_w, nsa_w_out,
           diff_w_in, diff_lq1, diff_lk1, diff_lq2, diff_lk2, diff_subln_g, diff_w_out,
           peer_w_q, peer_k1, peer_k2, peer_u, peer_v):
    past_len = page_table.shape[1] * cache_nsa_kv.shape[2]

    def trunk(x, c, sample):
        B, T, _ = x.shape
        pos0 = past_len if sample else 0
        cs = jax.nn.silu(c)
        new_S, new_conv, new_nsa_kv, new_nsa_win, new_diff_kv = [], [], [], [], []
        for i in range(DEPTH):
            mod = (cs @ ada_w[i] + ada_b[i]).reshape(B, 6, 1, D_MODEL)
            h = rmsnorm(x, norm_mix_g[i]) * (1.0 + mod[:, 1]) + mod[:, 0]
            j = i // N_MIXERS
            kind = i % N_MIXERS
            if kind == 0:
                if sample:
                    S0 = state_gdn_S[j].astype(jnp.float32)
                    buf = state_gdn_conv[j]
                else:
                    S0 = jnp.zeros((B, GDN_HEADS, GDN_DK, GDN_DV), jnp.float32)
                    buf = jnp.zeros((B, CONV_W - 1, GDN_HEADS * (2 * GDN_DK + GDN_DV)), x.dtype)
                m, buf_n, S_n = gdn_mixer(h, buf, S0, gdn_w_in[j], gdn_conv_w[j], gdn_a_log[j], gdn_dt_bias[j], gdn_norm_g[j], gdn_w_out[j])
                new_S.append(S_n.astype(x.dtype))
                new_conv.append(buf_n)
            elif kind == 1:
                past = gather_pages(cache_nsa_kv[j], page_table) if sample else None
                wbuf = state_nsa_window[j] if sample else None
                m, kv_n, win_n = nsa_mixer(h, pos0, past, wbuf, nsa_w_in[j], nsa_cmp_pos[j], nsa_cmp_w[j], nsa_w_out[j])
                new_nsa_kv.append(kv_n)
                new_nsa_win.append(win_n)
            else:
                past = gather_pages(cache_diff_kv[j], page_table) if sample else None
                m, kv_n = diff_mixer(h, pos0, past, i, diff_w_in[j], diff_lq1[j], diff_lk1[j], diff_lq2[j], diff_lk2[j], diff_subln_g[j], diff_w_out[j])
                new_diff_kv.append(kv_n)
            x = x + mod[:, 2] * m
            h = rmsnorm(x, norm_ffn_g[i]) * (1.0 + mod[:, 4]) + mod[:, 3]
            x = x + mod[:, 5] * peer(h, peer_w_q[i], peer_k1[i], peer_k2[i], peer_u[i], peer_v[i])
        y = rmsnorm(x, final_norm_g)
        return y, jnp.stack(new_S), jnp.stack(new_conv), jnp.stack(new_nsa_kv), jnp.stack(new_nsa_win), jnp.stack(new_diff_kv)

    y_prompt, p_gdn_S, p_gdn_conv, p_nsa_kv, p_nsa_win, p_diff_kv = trunk(x_prompt, c_prompt, False)
    y_sample, s_gdn_S, s_gdn_conv, s_nsa_kv, s_nsa_win, s_diff_kv = trunk(x_sample, c_sample, True)
    return (y_prompt, y_sample, p_gdn_S, p_gdn_conv, p_nsa_kv, p_nsa_win, p_diff_kv, s_gdn_S, s_gdn_conv, s_nsa_kv, s_nsa_win, s_diff_kv)
```

```python
import functools
import math

import jax
import jax.numpy as jnp
import numpy as np
from jax import lax
from jax.experimental import pallas as pl
from jax.experimental.pallas import tpu as pltpu

D_MODEL = 1024
DEPTH = 4
N_MIXERS = 3
HEAD_DIM = 128
GDN_HEADS = D_MODEL // HEAD_DIM
GDN_DK = HEAD_DIM
GDN_DV = HEAD_DIM
CONV_W = 4
GDN_CHUNK = 64
NSA_HEADS = D_MODEL // HEAD_DIM
NSA_KV_GROUPS = 2
CMP_BLOCK = 32
SEL_BLOCK = 64
TOP_N = 16
WINDOW = 512
Q_BLOCK = 128
DIFF_HEADS = D_MODEL // HEAD_DIM
DIFF_DH = D_MODEL // DIFF_HEADS // 2
PEER_HEADS = 8
N_KEYS = 128
N_EXPERTS = N_KEYS * N_KEYS
PEER_TOPK = 16
PEER_QDIM = 128
PEER_TOKEN_BLOCK = 256
ROPE_THETA = 10000.0
EPS = 1e-6

LANE = 128
VMEM_LIMIT = 48 * 1024 * 1024


def _mm_kernel(x_ref, w_ref, o_ref):
    o_ref[...] = jnp.dot(x_ref[...].astype(jnp.bfloat16), w_ref[...], preferred_element_type=jnp.float32)


def _mm(x, w, tm=256):
    lead = x.shape[:-1]
    K = x.shape[-1]
    N = w.shape[1]
    x2 = x.reshape(-1, K)
    M = x2.shape[0]
    n_pad = -(-N // 256) * 256
    wb = jnp.pad(w, ((0, 0), (0, n_pad - N))).astype(jnp.bfloat16)
    tm = min(tm, M)
    assert M % tm == 0
    out = pl.pallas_call(
        _mm_kernel,
        grid=(M // tm,),
        in_specs=[pl.BlockSpec((tm, K), lambda i: (i, 0)), pl.BlockSpec((K, n_pad), lambda i: (0, 0))],
        out_specs=pl.BlockSpec((tm, n_pad), lambda i: (i, 0)),
        out_shape=jax.ShapeDtypeStruct((M, n_pad), jnp.float32),
        compiler_params=pltpu.CompilerParams(dimension_semantics=("parallel",), vmem_limit_bytes=VMEM_LIMIT),
        name="proj_matmul",
    )(x2, wb)
    return out[:, :N].reshape(*lead, N)


def rmsnorm(x, g):
    xf = x.astype(jnp.float32)
    y = xf * lax.rsqrt(jnp.mean(xf * xf, -1, keepdims=True) + EPS)
    return (y * g.astype(jnp.float32)).astype(x.dtype)


def l2norm(x):
    xf = x.astype(jnp.float32)
    return (xf * lax.rsqrt(jnp.sum(xf * xf, -1, keepdims=True) + EPS)).astype(x.dtype)


def rope(x, pos):
    half = x.shape[-1] // 2
    inv = ROPE_THETA ** (-jnp.arange(half, dtype=jnp.float32) / half)
    ang = pos.astype(jnp.float32)[:, None] * inv[None, :]
    cos = jnp.cos(ang)[:, None, :]
    sin = jnp.sin(ang)[:, None, :]
    xf = x.astype(jnp.float32)
    x1, x2 = xf[..., :half], xf[..., half:]
    return jnp.concatenate([x1 * cos - x2 * sin, x2 * cos + x1 * sin], -1).astype(x.dtype)


def masked_softmax(s, mask):
    s = jnp.where(mask, s.astype(jnp.float32), -jnp.inf)
    m = jnp.max(s, -1, keepdims=True)
    m = jnp.where(jnp.isfinite(m), m, 0.0)
    p = jnp.exp(s - m)
    return p / jnp.maximum(jnp.sum(p, -1, keepdims=True), 1e-30)


def gather_pages(pool, page_table):
    g = pool[page_table]
    return g.reshape(g.shape[0], g.shape[1] * g.shape[2], *pool.shape[2:])


def causal_conv(x, buf, w):
    T = x.shape[1]
    xx = jnp.concatenate([buf, x], 1)
    y = xx[:, 0:T] * w[0]
    for j in range(1, CONV_W):
        y = y + xx[:, j:j + T] * w[j]
    return jax.nn.silu(y), xx[:, -(CONV_W - 1):]


def gated_delta_rule(q, k, v, g, beta, S0):
    f32 = jnp.float32
    B, T, H, dk = q.shape
    dv = v.shape[-1]
    C = GDN_CHUNK
    Tp = -(-T // C) * C

    def prep(a):
        a = a.astype(f32)
        a = jnp.pad(a, [(0, 0), (0, Tp - T)] + [(0, 0)] * (a.ndim - 2))
        a = a.reshape(B, Tp // C, C, *a.shape[2:])
        return jnp.moveaxis(a, (1, 3), (0, 2))

    qc = prep(q) * dk ** -0.5
    kc, vc, gc, bc = prep(k), prep(v), prep(g), prep(beta)
    gcum = jnp.cumsum(gc, -1)
    idx = jnp.arange(C)
    incl = idx[:, None] >= idx[None, :]
    strict = idx[:, None] > idx[None, :]
    decay = jnp.exp(jnp.where(incl, gcum[..., :, None] - gcum[..., None, :], -jnp.inf))
    kb = kc * bc[..., None]
    L = jnp.where(strict, jnp.einsum('nbhid,nbhjd->nbhij', kb, kc) * decay, 0.0)
    eye = jnp.eye(C, dtype=f32)
    A = eye + L
    Tinv = lax.linalg.triangular_solve(A, jnp.broadcast_to(eye, A.shape), left_side=True, lower=True, unit_diagonal=True)
    u = Tinv @ (vc * bc[..., None])
    w = Tinv @ (kb * jnp.exp(gcum)[..., None])
    a_intra = jnp.einsum('nbhid,nbhjd->nbhij', qc, kc) * decay

    def step(S, xs):
        q_i, k_i, u_i, w_i, a_i, g_i = xs
        v_new = u_i - w_i @ S
        o = (q_i * jnp.exp(g_i)[..., None]) @ S + a_i @ v_new
        g_last = g_i[..., -1:]
        S = S * jnp.exp(g_last)[..., None] + jnp.einsum('bhck,bhcv->bhkv', k_i * jnp.exp(g_last - g_i)[..., None], v_new)
        return S, o

    S, o = lax.scan(step, S0.astype(f32), (qc, kc, u, w, a_intra, gcum))
    o = jnp.moveaxis(o, (0, 2), (1, 3)).reshape(B, Tp, H, dv)[:, :T]
    return o, S


def gdn_mixer(h, conv_buf, S0, w_in, conv_w, a_log, dt_bias, norm_g, w_out):
    B, T, _ = h.shape
    H, dk, dv = GDN_HEADS, GDN_DK, GDN_DV
    n_qkv = H * (2 * dk + dv)
    f32 = jnp.float32
    proj = _mm(h, w_in)
    qkv, conv_new = causal_conv(proj[..., :n_qkv], conv_buf, conv_w)
    z = proj[..., n_qkv:n_qkv + H * dv].reshape(B, T, H, dv)
    b_raw = proj[..., n_qkv + H * dv:n_qkv + H * dv + H]
    a_raw = proj[..., n_qkv + H * dv + H:]
    q = l2norm(qkv[..., :H * dk].reshape(B, T, H, dk))
    k = l2norm(qkv[..., H * dk:2 * H * dk].reshape(B, T, H, dk))
    v = qkv[..., 2 * H * dk:].reshape(B, T, H, dv)
    beta = jax.nn.sigmoid(b_raw.astype(f32))
    g = -jnp.exp(a_log.astype(f32)) * jax.nn.softplus(a_raw.astype(f32) + dt_bias.astype(f32))
    o, S = gated_delta_rule(q, k, v, g, beta, S0)
    o = rmsnorm(o.astype(h.dtype), norm_g) * jax.nn.silu(z)
    return _mm(o.reshape(B, T, D_MODEL), w_out), conv_new, S


def nsa_seq(q_s, g_s, full_s, win_s, pos0, wpos0, cmp_pos, cmp_w):
    dt = q_s.dtype
    T = q_s.shape[0]
    G, R, dh = NSA_KV_GROUPS, NSA_HEADS // NSA_KV_GROUPS, HEAD_DIM
    Tk = full_s.shape[0]
    Tkp = -(-Tk // SEL_BLOCK) * SEL_BLOCK
    full_p = jnp.pad(full_s, ((0, Tkp - Tk), (0, 0), (0, 0), (0, 0)))
    n_cmp = Tkp // CMP_BLOCK
    n_sel = Tkp // SEL_BLOCK
    blocks = full_p[:, :2].reshape(n_cmp, CMP_BLOCK, 2, G, dh) + jnp.transpose(cmp_pos, (1, 0, 2))[:, :, None, :]
    kv_cmp = jnp.einsum('nlsgd,slde->nsge', blocks, cmp_w)
    k_cmp, v_cmp = kv_cmp[:, 0], kv_cmp[:, 1]
    cmp_end = (jnp.arange(n_cmp) + 1) * CMP_BLOCK - 1
    sel = jnp.transpose(full_p[:, 2:].reshape(n_sel, SEL_BLOCK, 2, G, dh), (2, 3, 0, 1, 4))
    k_selb, v_selb = sel[0], sel[1]
    n_top = min(TOP_N, n_sel)
    win_p = jnp.pad(win_s, ((WINDOW, 0), (0, 0), (0, 0), (0, 0)))
    qb = Q_BLOCK if T % Q_BLOCK == 0 else T
    nqb = T // qb
    lw = WINDOW + qb - 1
    scale = HEAD_DIM ** -0.5
    g_idx = jnp.arange(G)[None, :, None]
    blk_ids = jnp.arange(n_sel)

    def block(args):
        qi, gi, bi = args
        qstart = pos0 + bi * qb
        qpos = qstart + jnp.arange(qb)
        qg = qi.reshape(qb, G, R, dh) * scale
        s_c = jnp.einsum('qgrd,ngd->qgrn', qg, k_cmp)
        p_c = masked_softmax(s_c, (cmp_end[None, :] <= qpos[:, None])[:, None, None, :])
        o_c = jnp.einsum('qgrn,ngd->qgrd', p_c.astype(dt), v_cmp)
        imp = p_c.sum(2).reshape(qb, G, n_sel, SEL_BLOCK // CMP_BLOCK).sum(-1)
        cur = (qpos // SEL_BLOCK)[:, None] == blk_ids[None, :]
        causal_blk = (blk_ids * SEL_BLOCK)[None, :] <= qpos[:, None]
        imp = jnp.where(cur[:, None], jnp.inf, jnp.where(causal_blk[:, None], imp, -jnp.inf))
        top_v, top_i = lax.top_k(imp, n_top)
        k_sel = k_selb[g_idx, top_i]
        v_sel = v_selb[g_idx, top_i]
        kpos = top_i[..., None] * SEL_BLOCK + jnp.arange(SEL_BLOCK)
        m_s = (kpos <= qpos[:, None, None, None]) & (top_v > -jnp.inf)[..., None]
        s_s = jnp.einsum('qgrd,qgksd->qgrks', qg, k_sel).reshape(qb, G, R, n_top * SEL_BLOCK)
        p_s = masked_softmax(s_s, m_s.reshape(qb, G, 1, n_top * SEL_BLOCK))
        o_s = jnp.einsum('qgrm,qgmd->qgrd', p_s.astype(dt), v_sel.reshape(qb, G, n_top * SEL_BLOCK, dh))
        wblk = lax.dynamic_slice_in_dim(win_p, qstart - wpos0 + 1, lw, axis=0)
        wpos = qstart - WINDOW + 1 + jnp.arange(lw)
        m_w = (wpos[None] <= qpos[:, None]) & (wpos[None] > qpos[:, None] - WINDOW) & (wpos[None] >= wpos0)
        s_w = jnp.einsum('qgrd,kgd->qgrk', qg, wblk[:, 0])
        p_w = masked_softmax(s_w, m_w[:, None, None, :])
        o_w = jnp.einsum('qgrk,kgd->qgrd', p_w.astype(dt), wblk[:, 1])
        gg = gi.reshape(qb, G, R, 3)
        o = gg[..., 0:1] * o_c + gg[..., 1:2] * o_s + gg[..., 2:3] * o_w
        return o.reshape(qb, NSA_HEADS, dh)

    out = lax.map(block, (q_s.reshape(nqb, qb, NSA_HEADS, dh), g_s.reshape(nqb, qb, NSA_HEADS, 3), jnp.arange(nqb)))
    return out.reshape(T, NSA_HEADS, dh)


def nsa_mixer(h, pos0, past, wbuf, w_in, cmp_pos, cmp_w, w_out):
    B, T, _ = h.shape
    G, dh = NSA_KV_GROUPS, HEAD_DIM
    qd = NSA_HEADS * dh
    kvd = 6 * G * dh
    pos = pos0 + jnp.arange(T, dtype=jnp.int32)
    proj = _mm(h, w_in)
    q = rope(proj[..., :qd].reshape(B, T, NSA_HEADS, dh), pos)
    kv = proj[..., qd:qd + kvd].reshape(B, T, 6, G, dh)
    gates = jax.nn.sigmoid(proj[..., qd + kvd:].reshape(B, T, NSA_HEADS, 3))
    keys = rope(kv[:, :, 0::2].reshape(B, T, 3 * G, dh), pos).reshape(B, T, 3, G, dh)
    kv = jnp.stack([keys[:, :, 0], kv[:, :, 1], keys[:, :, 1], kv[:, :, 3], keys[:, :, 2], kv[:, :, 5]], axis=2)
    new_rows = kv[:, :, :4]
    if past is None:
        full = new_rows
        win = kv[:, :, 4:]
        wpos0 = 0
    else:
        full = jnp.concatenate([past, new_rows], 1)
        win = jnp.concatenate([wbuf, kv[:, :, 4:]], 1)
        wpos0 = pos0 - wbuf.shape[1]
    o = lax.map(lambda a: nsa_seq(a[0], a[1], a[2], a[3], pos0, wpos0, cmp_pos, cmp_w), (q, gates, full, win))
    new_win = win[:, -min(WINDOW, win.shape[1]):]
    return _mm(o.reshape(B, T, D_MODEL), w_out), new_rows, new_win


def diff_mixer(h, pos0, past, layer_idx, w_in, lq1, lk1, lq2, lk2, subln_g, w_out):
    B, T, _ = h.shape
    H, dd = DIFF_HEADS, DIFF_DH
    f32 = jnp.float32
    dt = h.dtype
    pos = pos0 + jnp.arange(T, dtype=jnp.int32)
    proj = _mm(h, w_in)
    q = rope(proj[..., :D_MODEL].reshape(B, T, 2 * H, dd), pos)
    k = rope(proj[..., D_MODEL:2 * D_MODEL].reshape(B, T, 2 * H, dd), pos)
    v = proj[..., 2 * D_MODEL:].reshape(B, T, 2 * H, dd)
    new_rows = jnp.stack([k, v], 2)
    kv = new_rows if past is None else jnp.concatenate([past, new_rows], 1)
    Tk = kv.shape[1]
    kk = kv[:, :, 0].reshape(B, Tk, H, 2, dd)
    vv = kv[:, :, 1].reshape(B, Tk, H, 2 * dd)
    lam_init = 0.8 - 0.6 * math.exp(-0.3 * layer_idx)
    lam = (jnp.exp(jnp.sum(lq1.astype(f32) * lk1.astype(f32))) - jnp.exp(jnp.sum(lq2.astype(f32) * lk2.astype(f32))) + lam_init)
    qg = q.reshape(B, T, H, 2, dd) * dd ** -0.5
    qb = Q_BLOCK if T % Q_BLOCK == 0 else T
    nqb = T // qb
    kpos = jnp.arange(Tk)

    def block(args):
        qi, bi = args
        qpos = pos0 + bi * qb + jnp.arange(qb)
        s = jnp.einsum('bqhcd,bkhcd->bhcqk', qi, kk)
        p = masked_softmax(s, kpos[None, :] <= qpos[:, None])
        a = p[:, :, 0] - lam * p[:, :, 1]
        return jnp.einsum('bhqk,bkhe->bqhe', a.astype(dt), vv)

    o = lax.map(block, (jnp.moveaxis(qg.reshape(B, nqb, qb, H, 2, dd), 1, 0), jnp.arange(nqb)))
    o = jnp.moveaxis(o, 0, 1).reshape(B, T, H, 2 * dd)
    o = rmsnorm(o, subln_g) * (1.0 - lam_init)
    return _mm(o.reshape(B, T, D_MODEL), w_out), new_rows


PEER_TB = 512
PEER_ET = 1024
_PEER_CAND = [(a, b) for a in range(PEER_TOPK) for b in range(PEER_TOPK) if (a + 1) * (b + 1) <= PEER_TOPK]


def _extract_top(s, n_iter, on_pick):
    rows = s.shape[0]
    iota = lax.broadcasted_iota(jnp.int32, s.shape, 0)
    for r in range(n_iter):
        m = jnp.max(s, axis=0, keepdims=True)
        idx = jnp.min(jnp.where(s == m, iota, rows), axis=0, keepdims=True)
        sel = iota == idx
        on_pick(r, m, sel)
        s = jnp.where(sel, -jnp.inf, s)


def _peer_route_kernel(x_ref, shift_ref, scale_ref, g_ref, wq_ref, k1_ref, k2_ref, aof_ref,
                       h_ref, c1_ref, cnt1_ref, rank2_ref, e2_ref, q_scr, v1_scr, v2_scr, cand_scr):
    f32 = jnp.float32
    x = x_ref[...]
    tb = x.shape[0] * x.shape[1]
    y = x * lax.rsqrt(jnp.mean(x * x, -1, keepdims=True) + EPS) * g_ref[...]
    h = (y * (1.0 + scale_ref[...]) + shift_ref[...]).reshape(tb, D_MODEL)
    hb = h.astype(jnp.bfloat16)
    h_ref[...] = hb
    q = jnp.dot(hb, wq_ref[...], preferred_element_type=f32).astype(jnp.bfloat16)
    for hh in range(PEER_HEADS):
        q_scr[hh] = q[:, hh * PEER_QDIM:(hh + 1) * PEER_QDIM]
    nt = (((1,), (1,)), ((), ()))
    n_chunk = tb // LANE
    cand_scr[...] = jnp.full(cand_scr.shape, -jnp.inf, f32)

    def body(it, carry):
        hh = it // n_chunk
        c0 = pl.multiple_of((it % n_chunk) * LANE, LANE)
        qc = q_scr[hh, pl.ds(c0, LANE), :]
        s1 = lax.dot_general(k1_ref[hh], qc, nt, preferred_element_type=f32)
        s2 = lax.dot_general(k2_ref[hh], qc, nt, preferred_element_type=f32)
        ranks = []
        for s, v_scr in ((s1, v1_scr), (s2, v2_scr)):
            rank = [jnp.full(s.shape, float(N_KEYS), f32)]

            def pick(r, m, sel, v_scr=v_scr, rank=rank):
                v_scr[r:r + 1, :] = m
                rank[0] = jnp.where(sel, float(r), rank[0])

            _extract_top(s, PEER_TOPK, pick)
            ranks.append(rank[0])
        for k, (a, b) in enumerate(_PEER_CAND):
            cand_scr[k:k + 1, :] = v1_scr[a:a + 1, :] + v2_scr[b:b + 1, :]
        top1 = v1_scr[0:1, :]
        top2 = v2_scr[0:1, :]
        top_val = top1 + top2
        a_of = aof_ref[...]
        iota16 = lax.broadcasted_iota(jnp.int32, (PEER_TOPK, LANE), 0)
        st = {"z": jnp.zeros((1, LANE), f32), "cnt": jnp.zeros((PEER_TOPK, LANE), f32)}

        def pick_c(r, m, sel, st=st):
            st["z"] = st["z"] + jnp.exp(m - top_val)
            a_sel = jnp.max(jnp.where(sel, a_of, 0), axis=0, keepdims=True)
            st["cnt"] = st["cnt"] + (iota16 == a_sel).astype(f32)

        _extract_top(cand_scr[...], PEER_TOPK, pick_c)
        inv_z = 1.0 / st["z"]
        cnt1 = jnp.zeros(s1.shape, f32)
        for a in range(PEER_TOPK):
            cnt1 = jnp.where(ranks[0] == float(a), st["cnt"][a:a + 1], cnt1)
        c1_ref[hh, :, pl.ds(c0, LANE)] = jnp.exp(s1 - top1) * inv_z
        cnt1_ref[hh, :, pl.ds(c0, LANE)] = cnt1
        rank2_ref[hh, :, pl.ds(c0, LANE)] = ranks[1]
        e2_ref[hh, :, pl.ds(c0, LANE)] = jnp.exp(s2 - top2)
        return carry

    lax.fori_loop(0, PEER_HEADS * n_chunk, body, 0)


def _peer_dense_kernel(hb_ref, u_ref, vt_ref, c1_ref, cnt1_ref, rank2_ref, e2_ref, x_ref, gm_ref,
                       o_ref, ht_ref, gh_ref, acc_ref):
    f32 = jnp.float32
    j = pl.program_id(1)
    tb = hb_ref.shape[0]

    @pl.when(j == 0)
    def _():
        acc_ref[...] = jnp.zeros_like(acc_ref)

    ht_ref[...] = lax.dot_general(u_ref[...], hb_ref[...], (((1,), (1,)), ((), ())), preferred_element_type=f32)

    def chunk_body(c, carry):
        lanes = pl.ds(pl.multiple_of(c * LANE, LANE), LANE)
        for r in range(PEER_ET // N_KEYS):
            g = jnp.zeros((N_KEYS, LANE), f32)
            for hh in range(PEER_HEADS):
                cnt_b = cnt1_ref[hh, r:r + 1, lanes]
                c_b = c1_ref[hh, r:r + 1, lanes]
                g = g + jnp.where(rank2_ref[hh, :, lanes] < cnt_b, e2_ref[hh, :, lanes] * c_b, 0.0)
            pre = ht_ref[r * N_KEYS:(r + 1) * N_KEYS, lanes]
            act = 0.5 * pre * (1.0 + lax.erf(pre * (2.0 ** -0.5)))
            gh_ref[r * N_KEYS:(r + 1) * N_KEYS, lanes] = (g * act).astype(jnp.bfloat16)
        return carry

    lax.fori_loop(0, tb // LANE, chunk_body, 0)
    acc_ref[...] += jnp.dot(vt_ref[...], gh_ref[...], preferred_element_type=f32)

    @pl.when(j == pl.num_programs(1) - 1)
    def _():
        upd = acc_ref[...].T.reshape(x_ref.shape)
        o_ref[...] = x_ref[...] + gm_ref[...] * upd


def _peer_sublayer(x, shift, scale, gate, norm_g, wq_b, k1p, k2p, u_b, vt_b):
    B, T, D = x.shape
    n = B * T
    tb = PEER_TB
    assert n % tb == 0
    if T % tb == 0:
        nbs, tper, per = 1, tb, T // tb
        xmap = lambda i, *_: (i // per, i % per, 0)
        mmap = lambda i, *_: (i // per, 0, 0)
    else:
        assert tb % T == 0 and T % 8 == 0
        nbs, tper = tb // T, T
        xmap = lambda i, *_: (i, 0, 0)
        mmap = lambda i, *_: (i, 0, 0)
    nblk = n // tb
    f32 = jnp.float32
    x_spec = pl.BlockSpec((nbs, tper, D), xmap)
    m_spec = pl.BlockSpec((nbs, 1, D), mmap)
    n_cand_pad = -(-len(_PEER_CAND) // 8) * 8
    a_of = jnp.asarray(np.broadcast_to(np.array([a for a, _ in _PEER_CAND] + [0] * (n_cand_pad - len(_PEER_CAND)),
                                                np.int32)[:, None], (n_cand_pad, LANE)))
    route_shape = jax.ShapeDtypeStruct((PEER_HEADS, N_KEYS, n), f32)
    route_spec = pl.BlockSpec((PEER_HEADS, N_KEYS, tb), lambda i: (0, 0, i))
    hb, c1, cnt1, rank2, e2 = pl.pallas_call(
        _peer_route_kernel,
        grid=(nblk,),
        in_specs=[x_spec, m_spec, m_spec,
                  pl.BlockSpec((1, D), lambda i: (0, 0)),
                  pl.BlockSpec((D, PEER_HEADS * PEER_QDIM), lambda i: (0, 0)),
                  pl.BlockSpec((PEER_HEADS, N_KEYS, PEER_QDIM), lambda i: (0, 0, 0)),
                  pl.BlockSpec((PEER_HEADS, N_KEYS, PEER_QDIM), lambda i: (0, 0, 0)),
                  pl.BlockSpec((n_cand_pad, LANE), lambda i: (0, 0))],
        out_specs=[pl.BlockSpec((tb, D), lambda i: (i, 0)), route_spec, route_spec, route_spec, route_spec],
        out_shape=[jax.ShapeDtypeStruct((n, D), jnp.bfloat16), route_shape, route_shape, route_shape, route_shape],
        scratch_shapes=[pltpu.VMEM((PEER_HEADS, tb, PEER_QDIM), jnp.bfloat16), pltpu.VMEM((PEER_TOPK, LANE), f32),
                        pltpu.VMEM((PEER_TOPK, LANE), f32), pltpu.VMEM((n_cand_pad, LANE), f32)],
        compiler_params=pltpu.CompilerParams(dimension_semantics=("parallel",), vmem_limit_bytes=VMEM_LIMIT),
        name="peer_route",
    )(x, shift, scale, norm_g.reshape(1, D), wq_b, k1p, k2p, a_of)

    rows = PEER_ET // N_KEYS
    sub_spec = pl.BlockSpec((PEER_HEADS, rows, tb), lambda i, j: (0, j, i))
    full_spec = pl.BlockSpec((PEER_HEADS, N_KEYS, tb), lambda i, j: (0, 0, i))
    return pl.pallas_call(
        _peer_dense_kernel,
        grid=(nblk, N_EXPERTS // PEER_ET),
        in_specs=[pl.BlockSpec((tb, D), lambda i, j: (i, 0)),
                  pl.BlockSpec((PEER_ET, D), lambda i, j: (j, 0)),
                  pl.BlockSpec((D, PEER_ET), lambda i, j: (0, j)),
                  sub_spec, sub_spec, full_spec, full_spec,
                  pl.BlockSpec((nbs, tper, D), lambda i, j: xmap(i)),
                  pl.BlockSpec((nbs, 1, D), lambda i, j: mmap(i))],
        out_specs=pl.BlockSpec((nbs, tper, D), lambda i, j: xmap(i)),
        out_shape=jax.ShapeDtypeStruct(x.shape, x.dtype),
        scratch_shapes=[pltpu.VMEM((PEER_ET, tb), f32), pltpu.VMEM((PEER_ET, tb), jnp.bfloat16),
                        pltpu.VMEM((D, tb), f32)],
        compiler_params=pltpu.CompilerParams(dimension_semantics=("parallel", "arbitrary"), vmem_limit_bytes=VMEM_LIMIT),
        name="peer_dense",
    )(hb, u_b, vt_b, c1, cnt1, rank2, e2, x, gate)


def _peer_weights(w_q, k1, k2, u_tab, v_tab):
    bf16 = jnp.bfloat16
    half = PEER_QDIM // 2
    k1p = jnp.pad(k1, ((0, 0), (0, 0), (0, half))).astype(bf16)
    k2p = jnp.pad(k2, ((0, 0), (0, 0), (half, 0))).astype(bf16)
    return w_q.astype(bf16), k1p, k2p, u_tab.astype(bf16), v_tab.astype(bf16).T


def kernel(x_prompt, x_sample, cache_nsa_kv, cache_diff_kv, state_nsa_window, state_gdn_S, state_gdn_conv,
           page_table, c_prompt, c_sample, ada_w, ada_b, norm_mix_g, norm_ffn_g, final_norm_g,
           gdn_w_in, gdn_conv_w, gdn_a_log, gdn_dt_bias, gdn_norm_g, gdn_w_out,
           nsa_w_in, nsa_cmp_pos, nsa_cmp_w, nsa_w_out,
           diff_w_in, diff_lq1, diff_lk1, diff_lq2, diff_lk2, diff_subln_g, diff_w_out,
           peer_w_q, peer_k1, peer_k2, peer_u, peer_v):
    past_len = page_table.shape[1] * cache_nsa_kv.shape[2]
    peer_w = [_peer_weights(peer_w_q[i], peer_k1[i], peer_k2[i], peer_u[i], peer_v[i]) for i in range(DEPTH)]

    def trunk(x, c, sample):
        B, T, _ = x.shape
        pos0 = past_len if sample else 0
        cs = jax.nn.silu(c)
        new_S, new_conv, new_nsa_kv, new_nsa_win, new_diff_kv = [], [], [], [], []
        for i in range(DEPTH):
            mod = (cs @ ada_w[i] + ada_b[i]).reshape(B, 6, 1, D_MODEL)
            h = rmsnorm(x, norm_mix_g[i]) * (1.0 + mod[:, 1]) + mod[:, 0]
            j = i // N_MIXERS
            kind = i % N_MIXERS
            if kind == 0:
                if sample:
                    S0 = state_gdn_S[j].astype(jnp.float32)
                    buf = state_gdn_conv[j]
                else:
                    S0 = jnp.zeros((B, GDN_HEADS, GDN_DK, GDN_DV), jnp.float32)
                    buf = jnp.zeros((B, CONV_W - 1, GDN_HEADS * (2 * GDN_DK + GDN_DV)), x.dtype)
                m, buf_n, S_n = gdn_mixer(h, buf, S0, gdn_w_in[j], gdn_conv_w[j], gdn_a_log[j], gdn_dt_bias[j], gdn_norm_g[j], gdn_w_out[j])
                new_S.append(S_n.astype(x.dtype))
                new_conv.append(buf_n)
            elif kind == 1:
                past = gather_pages(cache_nsa_kv[j], page_table) if sample else None
                wbuf = state_nsa_window[j] if sample else None
                m, kv_n, win_n = nsa_mixer(h, pos0, past, wbuf, nsa_w_in[j], nsa_cmp_pos[j], nsa_cmp_w[j], nsa_w_out[j])
                new_nsa_kv.append(kv_n)
                new_nsa_win.append(win_n)
            else:
                past = gather_pages(cache_diff_kv[j], page_table) if sample else None
                m, kv_n = diff_mixer(h, pos0, past, i, diff_w_in[j], diff_lq1[j], diff_lk1[j], diff_lq2[j], diff_lk2[j], diff_subln_g[j], diff_w_out[j])
                new_diff_kv.append(kv_n)
            x = x + mod[:, 2] * m
            x = _peer_sublayer(x, mod[:, 3], mod[:, 4], mod[:, 5], norm_ffn_g[i], *peer_w[i])
        y = rmsnorm(x, final_norm_g)
        return y, jnp.stack(new_S), jnp.stack(new_conv), jnp.stack(new_nsa_kv), jnp.stack(new_nsa_win), jnp.stack(new_diff_kv)

    y_prompt, p_gdn_S, p_gdn_conv, p_nsa_kv, p_nsa_win, p_diff_kv = trunk(x_prompt, c_prompt, False)
    y_sample, s_gdn_S, s_gdn_conv, s_nsa_kv, s_nsa_win, s_diff_kv = trunk(x_sample, c_sample, True)
    return (y_prompt, y_sample, p_gdn_S, p_gdn_conv, p_nsa_kv, p_nsa_win, p_diff_kv, s_gdn_S, s_gdn_conv, s_nsa_kv, s_nsa_win, s_diff_kv)
```

```python
import functools
import math

import jax
import jax.numpy as jnp
import numpy as np
from jax import lax
from jax.experimental import pallas as pl
from jax.experimental.pallas import tpu as pltpu

D_MODEL = 1024
DEPTH = 4
N_MIXERS = 3
HEAD_DIM = 128
GDN_HEADS = D_MODEL // HEAD_DIM
GDN_DK = HEAD_DIM
GDN_DV = HEAD_DIM
CONV_W = 4
GDN_CHUNK = 64
NSA_HEADS = D_MODEL // HEAD_DIM
NSA_KV_GROUPS = 2
CMP_BLOCK = 32
SEL_BLOCK = 64
TOP_N = 16
WINDOW = 512
Q_BLOCK = 128
DIFF_HEADS = D_MODEL // HEAD_DIM
DIFF_DH = D_MODEL // DIFF_HEADS // 2
PEER_HEADS = 8
N_KEYS = 128
N_EXPERTS = N_KEYS * N_KEYS
PEER_TOPK = 16
PEER_QDIM = 128
PEER_TOKEN_BLOCK = 256
ROPE_THETA = 10000.0
EPS = 1e-6

LANE = 128
VMEM_LIMIT = 48 * 1024 * 1024


def _mm_kernel(x_ref, w_ref, o_ref):
    o_ref[...] = jnp.dot(x_ref[...].astype(jnp.bfloat16), w_ref[...], preferred_element_type=jnp.float32)


def _mm(x, w, tm=256):
    lead = x.shape[:-1]
    K = x.shape[-1]
    N = w.shape[1]
    x2 = x.reshape(-1, K)
    M = x2.shape[0]
    n_pad = -(-N // 256) * 256
    wb = jnp.pad(w, ((0, 0), (0, n_pad - N))).astype(jnp.bfloat16)
    tm = min(tm, M)
    assert M % tm == 0
    out = pl.pallas_call(
        _mm_kernel,
        grid=(M // tm,),
        in_specs=[pl.BlockSpec((tm, K), lambda i: (i, 0)), pl.BlockSpec((K, n_pad), lambda i: (0, 0))],
        out_specs=pl.BlockSpec((tm, n_pad), lambda i: (i, 0)),
        out_shape=jax.ShapeDtypeStruct((M, n_pad), jnp.float32),
        compiler_params=pltpu.CompilerParams(dimension_semantics=("parallel",), vmem_limit_bytes=VMEM_LIMIT),
        name="proj_matmul",
    )(x2, wb)
    return out[:, :N].reshape(*lead, N)


def rmsnorm(x, g):
    xf = x.astype(jnp.float32)
    y = xf * lax.rsqrt(jnp.mean(xf * xf, -1, keepdims=True) + EPS)
    return (y * g.astype(jnp.float32)).astype(x.dtype)


def l2norm(x):
    xf = x.astype(jnp.float32)
    return (xf * lax.rsqrt(jnp.sum(xf * xf, -1, keepdims=True) + EPS)).astype(x.dtype)


def rope(x, pos):
    half = x.shape[-1] // 2
    inv = ROPE_THETA ** (-jnp.arange(half, dtype=jnp.float32) / half)
    ang = pos.astype(jnp.float32)[:, None] * inv[None, :]
    cos = jnp.cos(ang)[:, None, :]
    sin = jnp.sin(ang)[:, None, :]
    xf = x.astype(jnp.float32)
    x1, x2 = xf[..., :half], xf[..., half:]
    return jnp.concatenate([x1 * cos - x2 * sin, x2 * cos + x1 * sin], -1).astype(x.dtype)


def masked_softmax(s, mask):
    s = jnp.where(mask, s.astype(jnp.float32), -jnp.inf)
    m = jnp.max(s, -1, keepdims=True)
    m = jnp.where(jnp.isfinite(m), m, 0.0)
    p = jnp.exp(s - m)
    return p / jnp.maximum(jnp.sum(p, -1, keepdims=True), 1e-30)


def gather_pages(pool, page_table):
    g = pool[page_table]
    return g.reshape(g.shape[0], g.shape[1] * g.shape[2], *pool.shape[2:])


def causal_conv(x, buf, w):
    T = x.shape[1]
    xx = jnp.concatenate([buf, x], 1)
    y = xx[:, 0:T] * w[0]
    for j in range(1, CONV_W):
        y = y + xx[:, j:j + T] * w[j]
    return jax.nn.silu(y), xx[:, -(CONV_W - 1):]


def gated_delta_rule(q, k, v, g, beta, S0):
    f32 = jnp.float32
    B, T, H, dk = q.shape
    dv = v.shape[-1]
    C = GDN_CHUNK
    Tp = -(-T // C) * C

    def prep(a):
        a = a.astype(f32)
        a = jnp.pad(a, [(0, 0), (0, Tp - T)] + [(0, 0)] * (a.ndim - 2))
        a = a.reshape(B, Tp // C, C, *a.shape[2:])
        return jnp.moveaxis(a, (1, 3), (0, 2))

    qc = prep(q) * dk ** -0.5
    kc, vc, gc, bc = prep(k), prep(v), prep(g), prep(beta)
    gcum = jnp.cumsum(gc, -1)
    idx = jnp.arange(C)
    incl = idx[:, None] >= idx[None, :]
    strict = idx[:, None] > idx[None, :]
    decay = jnp.exp(jnp.where(incl, gcum[..., :, None] - gcum[..., None, :], -jnp.inf))
    kb = kc * bc[..., None]
    L = jnp.where(strict, jnp.einsum('nbhid,nbhjd->nbhij', kb, kc) * decay, 0.0)
    eye = jnp.eye(C, dtype=f32)
    A = eye + L
    Tinv = lax.linalg.triangular_solve(A, jnp.broadcast_to(eye, A.shape), left_side=True, lower=True, unit_diagonal=True)
    u = Tinv @ (vc * bc[..., None])
    w = Tinv @ (kb * jnp.exp(gcum)[..., None])
    a_intra = jnp.einsum('nbhid,nbhjd->nbhij', qc, kc) * decay

    def step(S, xs):
        q_i, k_i, u_i, w_i, a_i, g_i = xs
        v_new = u_i - w_i @ S
        o = (q_i * jnp.exp(g_i)[..., None]) @ S + a_i @ v_new
        g_last = g_i[..., -1:]
        S = S * jnp.exp(g_last)[..., None] + jnp.einsum('bhck,bhcv->bhkv', k_i * jnp.exp(g_last - g_i)[..., None], v_new)
        return S, o

    S, o = lax.scan(step, S0.astype(f32), (qc, kc, u, w, a_intra, gcum))
    o = jnp.moveaxis(o, (0, 2), (1, 3)).reshape(B, Tp, H, dv)[:, :T]
    return o, S


def gdn_mixer(h, conv_buf, S0, w_in, conv_w, a_log, dt_bias, norm_g, w_out):
    B, T, _ = h.shape
    H, dk, dv = GDN_HEADS, GDN_DK, GDN_DV
    n_qkv = H * (2 * dk + dv)
    f32 = jnp.float32
    proj = _mm(h, w_in)
    qkv, conv_new = causal_conv(proj[..., :n_qkv], conv_buf, conv_w)
    z = proj[..., n_qkv:n_qkv + H * dv].reshape(B, T, H, dv)
    b_raw = proj[..., n_qkv + H * dv:n_qkv + H * dv + H]
    a_raw = proj[..., n_qkv + H * dv + H:]
    q = l2norm(qkv[..., :H * dk].reshape(B, T, H, dk))
    k = l2norm(qkv[..., H * dk:2 * H * dk].reshape(B, T, H, dk))
    v = qkv[..., 2 * H * dk:].reshape(B, T, H, dv)
    beta = jax.nn.sigmoid(b_raw.astype(f32))
    g = -jnp.exp(a_log.astype(f32)) * jax.nn.softplus(a_raw.astype(f32) + dt_bias.astype(f32))
    o, S = _gdn_delta_rule(q.reshape(B, T, H * dk), k.reshape(B, T, H * dk), v.reshape(B, T, H * dv),
                           z.reshape(B, T, H * dv), g, beta, S0, norm_g)
    return _mm(o, w_out), conv_new, S


def _dot_split3(a, b):
    f32, bf16 = jnp.float32, jnp.bfloat16
    ah = a.astype(bf16)
    al = (a - ah.astype(f32)).astype(bf16)
    bh = b.astype(bf16)
    bl = (b - bh.astype(f32)).astype(bf16)
    dot = functools.partial(jnp.dot, preferred_element_type=f32)
    return dot(ah, bh) + dot(ah, bl) + dot(al, bh)


def _gdn_chunk_kernel(q_ref, k_ref, v_ref, z_ref, col_ref, row_ref, s0_ref, ng_ref, o_ref, sout_ref, s_scr):
    f32, bf16 = jnp.float32, jnp.bfloat16
    n = pl.program_id(1)
    C = q_ref.shape[1]
    H, dk, dv = GDN_HEADS, GDN_DK, GDN_DV
    dot = functools.partial(jnp.dot, preferred_element_type=f32)
    nt = (((1,), (1,)), ((), ()))

    @pl.when(n == 0)
    def _():
        s_scr[...] = s0_ref[0]

    ii = lax.broadcasted_iota(jnp.int32, (C, C), 0)
    jj = lax.broadcasted_iota(jnp.int32, (C, C), 1)
    incl = ii >= jj
    strict = ii > jj
    eye = (ii == jj).astype(f32)
    for h in range(H):
        ks = slice(h * dk, (h + 1) * dk)
        vs = slice(h * dv, (h + 1) * dv)
        qh = q_ref[0, :, ks] * dk ** -0.5
        kh = k_ref[0, :, ks]
        vh = v_ref[0, :, vs]
        gc = col_ref[0, :, h:h + 1]
        bc = col_ref[0, :, H + h:H + h + 1]
        gr = row_ref[0, 0, h:h + 1, :]
        decay = jnp.exp(jnp.where(incl, gc - gr, -jnp.inf))
        kb = kh * bc
        kcb = kh.astype(bf16)
        lower = jnp.where(strict, lax.dot_general(kb.astype(bf16), kcb, nt, preferred_element_type=f32) * decay, 0.0)
        mpow = -lower
        tinv = eye + mpow
        for _ in range(C.bit_length() - 2):
            mpow = _dot_split3(mpow, mpow)
            tinv = tinv + _dot_split3(tinv, mpow)
        eg = jnp.exp(gc)
        uw = dot(tinv.astype(bf16), jnp.concatenate([vh * bc, kb * eg], axis=1).astype(bf16))
        u, w = uw[:, :dv], uw[:, dv:]
        a_intra = lax.dot_general(qh.astype(bf16), kcb, nt, preferred_element_type=f32) * decay
        s_old = s_scr[h]
        s_b = s_old.astype(bf16)
        v_new = u - dot(w.astype(bf16), s_b)
        v_nb = v_new.astype(bf16)
        o = dot((qh * eg).astype(bf16), s_b) + dot(a_intra.astype(bf16), v_nb)
        g_last = gr[:, C - 1:C]
        kd = (kh * jnp.exp(g_last - gc)).T.astype(bf16)
        s_scr[h] = s_old * jnp.exp(g_last) + dot(kd, v_nb)
        y = o * lax.rsqrt(jnp.mean(o * o, -1, keepdims=True) + EPS) * ng_ref[...]
        zz = z_ref[0, :, vs]
        o_ref[0, :, vs] = y * (zz * jax.nn.sigmoid(zz))

    @pl.when(n == pl.num_programs(1) - 1)
    def _():
        sout_ref[0] = s_scr[...]


def _gdn_delta_rule(q, k, v, z, g, beta, S0, norm_g):
    B, T, _ = q.shape
    H, dk, dv = GDN_HEADS, GDN_DK, GDN_DV
    C = GDN_CHUNK if T % GDN_CHUNK == 0 else T
    assert C & (C - 1) == 0 and C % 8 == 0
    nC = T // C
    f32 = jnp.float32
    gcum = jnp.cumsum(g.astype(f32).reshape(B, nC, C, H), axis=2)
    col = jnp.concatenate([gcum.reshape(B, T, H), beta.astype(f32)], axis=-1)
    row = jnp.swapaxes(gcum, 2, 3)
    tok = lambda w: pl.BlockSpec((1, C, w), lambda b, n: (b, n, 0))
    o, s_out = pl.pallas_call(
        _gdn_chunk_kernel,
        grid=(B, nC),
        in_specs=[tok(H * dk), tok(H * dk), tok(H * dv), tok(H * dv), tok(2 * H),
                  pl.BlockSpec((1, 1, H, C), lambda b, n: (b, n, 0, 0)),
                  pl.BlockSpec((1, H, dk, dv), lambda b, n: (b, 0, 0, 0)),
                  pl.BlockSpec((1, dv), lambda b, n: (0, 0))],
        out_specs=[tok(H * dv), pl.BlockSpec((1, H, dk, dv), lambda b, n: (b, 0, 0, 0))],
        out_shape=[jax.ShapeDtypeStruct((B, T, H * dv), f32), jax.ShapeDtypeStruct((B, H, dk, dv), f32)],
        scratch_shapes=[pltpu.VMEM((H, dk, dv), f32)],
        compiler_params=pltpu.CompilerParams(dimension_semantics=("parallel", "arbitrary"), vmem_limit_bytes=VMEM_LIMIT),
        name="gdn_delta_rule",
    )(q, k, v, z, col, row, S0.astype(f32), norm_g.reshape(1, dv))
    return o, s_out


def nsa_seq(q_s, g_s, full_s, win_s, pos0, wpos0, cmp_pos, cmp_w):
    dt = q_s.dtype
    T = q_s.shape[0]
    G, R, dh = NSA_KV_GROUPS, NSA_HEADS // NSA_KV_GROUPS, HEAD_DIM
    Tk = full_s.shape[0]
    Tkp = -(-Tk // SEL_BLOCK) * SEL_BLOCK
    full_p = jnp.pad(full_s, ((0, Tkp - Tk), (0, 0), (0, 0), (0, 0)))
    n_cmp = Tkp // CMP_BLOCK
    n_sel = Tkp // SEL_BLOCK
    blocks = full_p[:, :2].reshape(n_cmp, CMP_BLOCK, 2, G, dh) + jnp.transpose(cmp_pos, (1, 0, 2))[:, :, None, :]
    kv_cmp = jnp.einsum('nlsgd,slde->nsge', blocks, cmp_w)
    k_cmp, v_cmp = kv_cmp[:, 0], kv_cmp[:, 1]
    cmp_end = (jnp.arange(n_cmp) + 1) * CMP_BLOCK - 1
    sel = jnp.transpose(full_p[:, 2:].reshape(n_sel, SEL_BLOCK, 2, G, dh), (2, 3, 0, 1, 4))
    k_selb, v_selb = sel[0], sel[1]
    n_top = min(TOP_N, n_sel)
    win_p = jnp.pad(win_s, ((WINDOW, 0), (0, 0), (0, 0), (0, 0)))
    qb = Q_BLOCK if T % Q_BLOCK == 0 else T
    nqb = T // qb
    lw = WINDOW + qb - 1
    scale = HEAD_DIM ** -0.5
    g_idx = jnp.arange(G)[None, :, None]
    blk_ids = jnp.arange(n_sel)

    def block(args):
        qi, gi, bi = args
        qstart = pos0 + bi * qb
        qpos = qstart + jnp.arange(qb)
        qg = qi.reshape(qb, G, R, dh) * scale
        s_c = jnp.einsum('qgrd,ngd->qgrn', qg, k_cmp)
        p_c = masked_softmax(s_c, (cmp_end[None, :] <= qpos[:, None])[:, None, None, :])
        o_c = jnp.einsum('qgrn,ngd->qgrd', p_c.astype(dt), v_cmp)
        imp = p_c.sum(2).reshape(qb, G, n_sel, SEL_BLOCK // CMP_BLOCK).sum(-1)
        cur = (qpos // SEL_BLOCK)[:, None] == blk_ids[None, :]
        causal_blk = (blk_ids * SEL_BLOCK)[None, :] <= qpos[:, None]
        imp = jnp.where(cur[:, None], jnp.inf, jnp.where(causal_blk[:, None], imp, -jnp.inf))
        top_v, top_i = lax.top_k(imp, n_top)
        k_sel = k_selb[g_idx, top_i]
        v_sel = v_selb[g_idx, top_i]
        kpos = top_i[..., None] * SEL_BLOCK + jnp.arange(SEL_BLOCK)
        m_s = (kpos <= qpos[:, None, None, None]) & (top_v > -jnp.inf)[..., None]
        s_s = jnp.einsum('qgrd,qgksd->qgrks', qg, k_sel).reshape(qb, G, R, n_top * SEL_BLOCK)
        p_s = masked_softmax(s_s, m_s.reshape(qb, G, 1, n_top * SEL_BLOCK))
        o_s = jnp.einsum('qgrm,qgmd->qgrd', p_s.astype(dt), v_sel.reshape(qb, G, n_top * SEL_BLOCK, dh))
        wblk = lax.dynamic_slice_in_dim(win_p, qstart - wpos0 + 1, lw, axis=0)
        wpos = qstart - WINDOW + 1 + jnp.arange(lw)
        m_w = (wpos[None] <= qpos[:, None]) & (wpos[None] > qpos[:, None] - WINDOW) & (wpos[None] >= wpos0)
        s_w = jnp.einsum('qgrd,kgd->qgrk', qg, wblk[:, 0])
        p_w = masked_softmax(s_w, m_w[:, None, None, :])
        o_w = jnp.einsum('qgrk,kgd->qgrd', p_w.astype(dt), wblk[:, 1])
        gg = gi.reshape(qb, G, R, 3)
        o = gg[..., 0:1] * o_c + gg[..., 1:2] * o_s + gg[..., 2:3] * o_w
        return o.reshape(qb, NSA_HEADS, dh)

    out = lax.map(block, (q_s.reshape(nqb, qb, NSA_HEADS, dh), g_s.reshape(nqb, qb, NSA_HEADS, 3), jnp.arange(nqb)))
    return out.reshape(T, NSA_HEADS, dh)


NSA_R = NSA_HEADS // NSA_KV_GROUPS
NSA_TK = 512
NSA_WSPAN = WINDOW + Q_BLOCK


def _softmax_rows(s, mask):
    s = jnp.where(mask, s, -jnp.inf)
    m = jnp.max(s, axis=0, keepdims=True)
    m = jnp.where(m == -jnp.inf, 0.0, m)
    p = jnp.exp(s - m)
    return p * (1.0 / jnp.maximum(jnp.sum(p, axis=0, keepdims=True), 1e-30))


def _nsa_cmp_kernel(x_ref, pos_ref, w_ref, o_ref):
    xb = (x_ref[0, 0, 0, 0] + pos_ref[0]).astype(jnp.bfloat16)
    o_ref[0, 0, 0, 0] = jnp.dot(xb, w_ref[0], preferred_element_type=jnp.float32)


def _nsa_compress(rows_cmp, cmp_pos, cmp_w):
    B, T, _, G, dh = rows_cmp.shape
    n_sel = T // SEL_BLOCK
    ld = CMP_BLOCK * dh
    x = rows_cmp.reshape(B, n_sel, 2, CMP_BLOCK, 2, G, dh)
    x = jnp.transpose(x, (0, 4, 5, 2, 1, 3, 6)).reshape(B, 2, G, 2, n_sel, ld)
    return pl.pallas_call(
        _nsa_cmp_kernel,
        grid=(B, 2, G, 2),
        in_specs=[pl.BlockSpec((1, 1, 1, 1, n_sel, ld), lambda b, s, g, p: (b, s, g, p, 0, 0)),
                  pl.BlockSpec((1, 1, ld), lambda b, s, g, p: (s, 0, 0)),
                  pl.BlockSpec((1, ld, dh), lambda b, s, g, p: (s, 0, 0))],
        out_specs=pl.BlockSpec((1, 1, 1, 1, n_sel, dh), lambda b, s, g, p: (b, s, g, p, 0, 0)),
        out_shape=jax.ShapeDtypeStruct((B, 2, G, 2, n_sel, dh), jnp.float32),
        compiler_params=pltpu.CompilerParams(dimension_semantics=("parallel",) * 4, vmem_limit_bytes=VMEM_LIMIT),
        name="nsa_compress",
    )(x, cmp_pos.reshape(2, 1, ld), cmp_w.reshape(2, ld, dh).astype(jnp.bfloat16))


def _nsa_prompt_kernel(q_ref, gt_ref, kc_ref, vct_ref, ks_ref, vst_ref, kw_ref, vwt_ref, o_ref,
                       sel_scr, m_scr, l_scr, acc_scr):
    f32, bf16 = jnp.float32, jnp.bfloat16
    i = pl.program_id(2)
    R, QB = NSA_R, Q_BLOCK
    W = R * QB
    T = ks_ref.shape[2]
    n_sel = T // SEL_BLOCK
    nt = (((1,), (1,)), ((), ()))
    qb = q_ref[0]
    q = jnp.concatenate([qb[:, r * HEAD_DIM:(r + 1) * HEAD_DIM] for r in range(R)], axis=0)
    q = (q * HEAD_DIM ** -0.5).astype(bf16)
    lane = lax.broadcasted_iota(jnp.int32, (1, W), 1)
    qpos = i * QB + (lane & (QB - 1))

    s = lax.dot_general(kc_ref[0, 0], q, nt, preferred_element_type=f32)
    row = lax.broadcasted_iota(jnp.int32, (2 * n_sel, 1), 0)
    cidx = jnp.where(row < n_sel, 2 * row, 2 * (row - n_sel) + 1)
    p_c = _softmax_rows(s, (cidx + 1) * CMP_BLOCK - 1 <= qpos)
    o_c = jnp.dot(vct_ref[0, 0], p_c.astype(bf16), preferred_element_type=f32)

    p_r = p_c[:, 0:QB]
    for r in range(1, R):
        p_r = p_r + p_c[:, r * QB:(r + 1) * QB]
    imp = p_r[:n_sel] + p_r[n_sel:]
    blk = lax.broadcasted_iota(jnp.int32, (n_sel, QB), 0)
    qp = qpos[:, :QB]
    imp = jnp.where(blk == qp // SEL_BLOCK, jnp.inf, jnp.where(blk * SEL_BLOCK <= qp, imp, -jnp.inf))
    chosen = [jnp.zeros((n_sel, QB), f32)]

    def pick(r, m, sel, chosen=chosen):
        chosen[0] = jnp.where(sel & (m > -jnp.inf), 1.0, chosen[0])

    _extract_top(imp, min(TOP_N, n_sel), pick)
    sel_scr[...] = jnp.concatenate([chosen[0]] * R, axis=1)

    m_scr[...] = jnp.full(m_scr.shape, -jnp.inf, f32)
    l_scr[...] = jnp.zeros(l_scr.shape, f32)
    acc_scr[...] = jnp.zeros(acc_scr.shape, f32)
    blk_per_tile = NSA_TK // SEL_BLOCK

    def tile_body(kt, carry):
        k0 = pl.multiple_of(kt * NSA_TK, NSA_TK)
        s = lax.dot_general(ks_ref[0, 0, pl.ds(k0, NSA_TK), :], q, nt, preferred_element_type=f32)
        selrows = sel_scr[pl.ds(pl.multiple_of(kt * blk_per_tile, blk_per_tile), blk_per_tile), :]
        kpos = k0 + lax.broadcasted_iota(jnp.int32, (NSA_TK, 1), 0)
        parts = []
        for j in range(blk_per_tile):
            rows = slice(j * SEL_BLOCK, (j + 1) * SEL_BLOCK)
            ok = (selrows[j:j + 1, :] > 0.0) & (kpos[rows] <= qpos)
            parts.append(jnp.where(ok, s[rows], -jnp.inf))
        _online_softmax_tile(jnp.concatenate(parts, axis=0), vst_ref[0, 0, :, pl.ds(k0, NSA_TK)], m_scr, l_scr, acc_scr)
        return carry

    lax.fori_loop(0, (i * QB + QB + NSA_TK - 1) // NSA_TK, tile_body, 0)
    o_s = acc_scr[...] * (1.0 / jnp.maximum(l_scr[...], 1e-30))

    w0 = pl.multiple_of(jnp.clip(i * QB + QB - NSA_WSPAN, 0, T - NSA_WSPAN), QB)
    s = lax.dot_general(kw_ref[0, 0, pl.ds(w0, NSA_WSPAN), :], q, nt, preferred_element_type=f32)
    kpos = w0 + lax.broadcasted_iota(jnp.int32, (NSA_WSPAN, 1), 0)
    p_w = _softmax_rows(s, (kpos <= qpos) & (kpos > qpos - WINDOW))
    o_w = jnp.dot(vwt_ref[0, 0, :, pl.ds(w0, NSA_WSPAN)], p_w.astype(bf16), preferred_element_type=f32)

    g = gt_ref[0, 0, 0]
    o = g[0:1] * o_c + g[1:2] * o_s + g[2:3] * o_w
    for r in range(R):
        o_ref[0, :, r * HEAD_DIM:(r + 1) * HEAD_DIM] = o[:, r * QB:(r + 1) * QB].T


def _nsa_prompt_attention(q, gates, new_rows, win, cmp_pos, cmp_w):
    B, T, _ = q.shape
    G, R, dh, QB = NSA_KV_GROUPS, NSA_R, HEAD_DIM, Q_BLOCK
    assert T % NSA_TK == 0 and T >= NSA_WSPAN and (T // SEL_BLOCK) % 8 == 0
    nqb = T // QB
    n_sel = T // SEL_BLOCK
    W = R * QB
    bf16 = jnp.bfloat16
    cmp = _nsa_compress(new_rows[:, :, :2], cmp_pos, cmp_w).reshape(B, 2, G, 2 * n_sel, dh)
    kc = cmp[:, 0].astype(bf16)
    vct = jnp.swapaxes(cmp[:, 1], -1, -2).astype(bf16)
    ks = jnp.transpose(new_rows[:, :, 2], (0, 2, 1, 3)).astype(bf16)
    vst = jnp.transpose(new_rows[:, :, 3], (0, 2, 3, 1)).astype(bf16)
    kw = jnp.transpose(win[:, :, 0], (0, 2, 1, 3)).astype(bf16)
    vwt = jnp.transpose(win[:, :, 1], (0, 2, 3, 1)).astype(bf16)
    gt = jnp.transpose(gates.reshape(B, nqb, QB, G, R, 3), (0, 3, 1, 5, 4, 2)).reshape(B, G, nqb, 3, W)
    gt = jnp.pad(gt, ((0, 0), (0, 0), (0, 0), (0, 5), (0, 0)))
    per_bg = lambda *shape: pl.BlockSpec((1, 1) + shape, lambda b, g, i: (b, g, 0, 0))
    return pl.pallas_call(
        _nsa_prompt_kernel,
        grid=(B, G, nqb),
        in_specs=[pl.BlockSpec((1, QB, R * dh), lambda b, g, i: (b, i, g)),
                  pl.BlockSpec((1, 1, 1, 8, W), lambda b, g, i: (b, g, i, 0, 0)),
                  per_bg(2 * n_sel, dh), per_bg(dh, 2 * n_sel), per_bg(T, dh), per_bg(dh, T), per_bg(T, dh), per_bg(dh, T)],
        out_specs=pl.BlockSpec((1, QB, R * dh), lambda b, g, i: (b, i, g)),
        out_shape=jax.ShapeDtypeStruct(q.shape, jnp.float32),
        scratch_shapes=[pltpu.VMEM((n_sel, W), jnp.float32), pltpu.VMEM((1, W), jnp.float32),
                        pltpu.VMEM((1, W), jnp.float32), pltpu.VMEM((dh, W), jnp.float32)],
        compiler_params=pltpu.CompilerParams(dimension_semantics=("parallel", "parallel", "arbitrary"),
                                             vmem_limit_bytes=VMEM_LIMIT),
        name="nsa_prompt",
    )(q, gt, kc, vct, ks, vst, kw, vwt)


def nsa_mixer(h, pos0, past, wbuf, w_in, cmp_pos, cmp_w, w_out):
    B, T, _ = h.shape
    G, dh = NSA_KV_GROUPS, HEAD_DIM
    qd = NSA_HEADS * dh
    kvd = 6 * G * dh
    pos = pos0 + jnp.arange(T, dtype=jnp.int32)
    proj = _mm(h, w_in)
    q = rope(proj[..., :qd].reshape(B, T, NSA_HEADS, dh), pos)
    kv = proj[..., qd:qd + kvd].reshape(B, T, 6, G, dh)
    gates = jax.nn.sigmoid(proj[..., qd + kvd:].reshape(B, T, NSA_HEADS, 3))
    keys = rope(kv[:, :, 0::2].reshape(B, T, 3 * G, dh), pos).reshape(B, T, 3, G, dh)
    kv = jnp.stack([keys[:, :, 0], kv[:, :, 1], keys[:, :, 1], kv[:, :, 3], keys[:, :, 2], kv[:, :, 5]], axis=2)
    new_rows = kv[:, :, :4]
    if past is None:
        full = new_rows
        win = kv[:, :, 4:]
        wpos0 = 0
    else:
        full = jnp.concatenate([past, new_rows], 1)
        win = jnp.concatenate([wbuf, kv[:, :, 4:]], 1)
        wpos0 = pos0 - wbuf.shape[1]
    if past is None:
        o = _nsa_prompt_attention(q.reshape(B, T, qd), gates, new_rows, win, cmp_pos, cmp_w)
    else:
        o = lax.map(lambda a: nsa_seq(a[0], a[1], a[2], a[3], pos0, wpos0, cmp_pos, cmp_w), (q, gates, full, win))
    new_win = win[:, -min(WINDOW, win.shape[1]):]
    return _mm(o.reshape(B, T, D_MODEL), w_out), new_rows, new_win


DIFF_QB = 256
DIFF_TK = 512


def _online_softmax_tile(s, vt_tile, m_scr, l_scr, acc_scr):
    m_old = m_scr[...]
    m_new = jnp.maximum(m_old, jnp.max(s, axis=0, keepdims=True))
    m_safe = jnp.where(m_new == -jnp.inf, 0.0, m_new)
    alpha = jnp.exp(m_old - m_safe)
    p = jnp.exp(s - m_safe)
    l_scr[...] = alpha * l_scr[...] + jnp.sum(p, axis=0, keepdims=True)
    acc_scr[...] = alpha * acc_scr[...] + jnp.dot(vt_tile, p.astype(jnp.bfloat16), preferred_element_type=jnp.float32)
    m_scr[...] = m_new


def _diff_prompt_kernel(lam_ref, q_ref, k_ref, vt_ref, g_ref, o_ref, m_scr, l_scr, acc_scr, *, out_scale):
    f32, bf16 = jnp.float32, jnp.bfloat16
    i = pl.program_id(2)
    QB, TK, dd = DIFF_QB, DIFF_TK, DIFF_DH
    W = 2 * QB
    nt = (((1,), (1,)), ((), ()))
    qb = q_ref[0] * dd ** -0.5
    col = lax.broadcasted_iota(jnp.int32, (1, 2 * dd), 1)
    q = jnp.concatenate([jnp.where(col < dd, qb, 0.0), jnp.where(col >= dd, qb, 0.0)], axis=0).astype(bf16)
    lane = lax.broadcasted_iota(jnp.int32, (1, W), 1)
    qpos = i * QB + (lane & (QB - 1))
    m_scr[...] = jnp.full(m_scr.shape, -jnp.inf, f32)
    l_scr[...] = jnp.zeros(l_scr.shape, f32)
    acc_scr[...] = jnp.zeros(acc_scr.shape, f32)
    n_full = (i * QB) // TK

    def full_tile(kt, carry):
        k0 = pl.multiple_of(kt * TK, TK)
        s = lax.dot_general(k_ref[0, pl.ds(k0, TK), :], q, nt, preferred_element_type=f32)
        _online_softmax_tile(s, vt_ref[0, 0, :, pl.ds(k0, TK)], m_scr, l_scr, acc_scr)
        return carry

    lax.fori_loop(0, n_full, full_tile, 0)
    k0 = pl.multiple_of(n_full * TK, TK)
    s = lax.dot_general(k_ref[0, pl.ds(k0, TK), :], q, nt, preferred_element_type=f32)
    kpos = k0 + lax.broadcasted_iota(jnp.int32, (TK, 1), 0)
    _online_softmax_tile(jnp.where(kpos <= qpos, s, -jnp.inf), vt_ref[0, 0, :, pl.ds(k0, TK)], m_scr, l_scr, acc_scr)
    o = acc_scr[...] * (1.0 / jnp.maximum(l_scr[...], 1e-30))
    o = o[:, :QB] - lam_ref[0, 0] * o[:, QB:]
    o = o * lax.rsqrt(jnp.mean(o * o, axis=0, keepdims=True) + EPS) * (g_ref[...] * out_scale)
    o_ref[0] = o.T


def _diff_prompt_attention(q, k, v, lam, subln_g, out_scale):
    B, T, _ = q.shape
    H, dv = DIFF_HEADS, 2 * DIFF_DH
    assert T % DIFF_TK == 0 and DIFF_TK % DIFF_QB == 0
    bf16 = jnp.bfloat16
    vt = jnp.transpose(v.reshape(B, T, H, dv), (0, 2, 3, 1)).astype(bf16)
    W = 2 * DIFF_QB
    return pl.pallas_call(
        functools.partial(_diff_prompt_kernel, out_scale=out_scale),
        grid=(B, H, T // DIFF_QB),
        in_specs=[pl.BlockSpec(memory_space=pltpu.SMEM),
                  pl.BlockSpec((1, DIFF_QB, dv), lambda b, h, i: (b, i, h)),
                  pl.BlockSpec((1, T, dv), lambda b, h, i: (b, 0, h)),
                  pl.BlockSpec((1, 1, dv, T), lambda b, h, i: (b, h, 0, 0)),
                  pl.BlockSpec((dv, 1), lambda b, h, i: (0, 0))],
        out_specs=pl.BlockSpec((1, DIFF_QB, dv), lambda b, h, i: (b, i, h)),
        out_shape=jax.ShapeDtypeStruct(q.shape, jnp.float32),
        scratch_shapes=[pltpu.VMEM((1, W), jnp.float32), pltpu.VMEM((1, W), jnp.float32), pltpu.VMEM((dv, W), jnp.float32)],
        compiler_params=pltpu.CompilerParams(dimension_semantics=("parallel", "parallel", "arbitrary"),
                                             vmem_limit_bytes=VMEM_LIMIT),
        name="diff_prompt",
    )(lam.reshape(1, 1), q, k.astype(bf16), vt, subln_g.reshape(dv, 1))


def diff_mixer(h, pos0, past, layer_idx, w_in, lq1, lk1, lq2, lk2, subln_g, w_out):
    B, T, _ = h.shape
    H, dd = DIFF_HEADS, DIFF_DH
    f32 = jnp.float32
    dt = h.dtype
    pos = pos0 + jnp.arange(T, dtype=jnp.int32)
    proj = _mm(h, w_in)
    q = rope(proj[..., :D_MODEL].reshape(B, T, 2 * H, dd), pos)
    k = rope(proj[..., D_MODEL:2 * D_MODEL].reshape(B, T, 2 * H, dd), pos)
    v = proj[..., 2 * D_MODEL:].reshape(B, T, 2 * H, dd)
    new_rows = jnp.stack([k, v], 2)
    lam_init = 0.8 - 0.6 * math.exp(-0.3 * layer_idx)
    lam = (jnp.exp(jnp.sum(lq1.astype(f32) * lk1.astype(f32))) - jnp.exp(jnp.sum(lq2.astype(f32) * lk2.astype(f32))) + lam_init)
    if past is None:
        o = _diff_prompt_attention(q.reshape(B, T, D_MODEL), k.reshape(B, T, D_MODEL), v.reshape(B, T, D_MODEL),
                                   lam, subln_g, 1.0 - lam_init)
        return _mm(o, w_out), new_rows
    kv = jnp.concatenate([past, new_rows], 1)
    Tk = kv.shape[1]
    kk = kv[:, :, 0].reshape(B, Tk, H, 2, dd)
    vv = kv[:, :, 1].reshape(B, Tk, H, 2 * dd)
    qg = q.reshape(B, T, H, 2, dd) * dd ** -0.5
    qb = Q_BLOCK if T % Q_BLOCK == 0 else T
    nqb = T // qb
    kpos = jnp.arange(Tk)

    def block(args):
        qi, bi = args
        qpos = pos0 + bi * qb + jnp.arange(qb)
        s = jnp.einsum('bqhcd,bkhcd->bhcqk', qi, kk)
        p = masked_softmax(s, kpos[None, :] <= qpos[:, None])
        a = p[:, :, 0] - lam * p[:, :, 1]
        return jnp.einsum('bhqk,bkhe->bqhe', a.astype(dt), vv)

    o = lax.map(block, (jnp.moveaxis(qg.reshape(B, nqb, qb, H, 2, dd), 1, 0), jnp.arange(nqb)))
    o = jnp.moveaxis(o, 0, 1).reshape(B, T, H, 2 * dd)
    o = rmsnorm(o, subln_g) * (1.0 - lam_init)
    return _mm(o.reshape(B, T, D_MODEL), w_out), new_rows


PEER_TB = 512
PEER_ET = 1024
_PEER_CAND = [(a, b) for a in range(PEER_TOPK) for b in range(PEER_TOPK) if (a + 1) * (b + 1) <= PEER_TOPK]


def _extract_top(s, n_iter, on_pick):
    rows = s.shape[0]
    iota = lax.broadcasted_iota(jnp.int32, s.shape, 0)
    for r in range(n_iter):
        m = jnp.max(s, axis=0, keepdims=True)
        idx = jnp.min(jnp.where(s == m, iota, rows), axis=0, keepdims=True)
        sel = iota == idx
        on_pick(r, m, sel)
        s = jnp.where(sel, -jnp.inf, s)


def _peer_route_kernel(x_ref, shift_ref, scale_ref, g_ref, wq_ref, k1_ref, k2_ref, aof_ref,
                       h_ref, c1_ref, cnt1_ref, rank2_ref, e2_ref, q_scr, v1_scr, v2_scr, cand_scr):
    f32 = jnp.float32
    x = x_ref[...]
    tb = x.shape[0] * x.shape[1]
    y = x * lax.rsqrt(jnp.mean(x * x, -1, keepdims=True) + EPS) * g_ref[...]
    h = (y * (1.0 + scale_ref[...]) + shift_ref[...]).reshape(tb, D_MODEL)
    hb = h.astype(jnp.bfloat16)
    h_ref[...] = hb
    q = jnp.dot(hb, wq_ref[...], preferred_element_type=f32).astype(jnp.bfloat16)
    for hh in range(PEER_HEADS):
        q_scr[hh] = q[:, hh * PEER_QDIM:(hh + 1) * PEER_QDIM]
    nt = (((1,), (1,)), ((), ()))
    n_chunk = tb // LANE
    cand_scr[...] = jnp.full(cand_scr.shape, -jnp.inf, f32)

    def body(it, carry):
        hh = it // n_chunk
        c0 = pl.multiple_of((it % n_chunk) * LANE, LANE)
        qc = q_scr[hh, pl.ds(c0, LANE), :]
        s1 = lax.dot_general(k1_ref[hh], qc, nt, preferred_element_type=f32)
        s2 = lax.dot_general(k2_ref[hh], qc, nt, preferred_element_type=f32)
        ranks = []
        for s, v_scr in ((s1, v1_scr), (s2, v2_scr)):
            rank = [jnp.full(s.shape, float(N_KEYS), f32)]

            def pick(r, m, sel, v_scr=v_scr, rank=rank):
                v_scr[r:r + 1, :] = m
                rank[0] = jnp.where(sel, float(r), rank[0])

            _extract_top(s, PEER_TOPK, pick)
            ranks.append(rank[0])
        for k, (a, b) in enumerate(_PEER_CAND):
            cand_scr[k:k + 1, :] = v1_scr[a:a + 1, :] + v2_scr[b:b + 1, :]
        top1 = v1_scr[0:1, :]
        top2 = v2_scr[0:1, :]
        top_val = top1 + top2
        a_of = aof_ref[...]
        iota16 = lax.broadcasted_iota(jnp.int32, (PEER_TOPK, LANE), 0)
        st = {"z": jnp.zeros((1, LANE), f32), "cnt": jnp.zeros((PEER_TOPK, LANE), f32)}

        def pick_c(r, m, sel, st=st):
            st["z"] = st["z"] + jnp.exp(m - top_val)
            a_sel = jnp.max(jnp.where(sel, a_of, 0), axis=0, keepdims=True)
            st["cnt"] = st["cnt"] + (iota16 == a_sel).astype(f32)

        _extract_top(cand_scr[...], PEER_TOPK, pick_c)
        inv_z = 1.0 / st["z"]
        cnt1 = jnp.zeros(s1.shape, f32)
        for a in range(PEER_TOPK):
            cnt1 = jnp.where(ranks[0] == float(a), st["cnt"][a:a + 1], cnt1)
        c1_ref[hh, :, pl.ds(c0, LANE)] = jnp.exp(s1 - top1) * inv_z
        cnt1_ref[hh, :, pl.ds(c0, LANE)] = cnt1
        rank2_ref[hh, :, pl.ds(c0, LANE)] = ranks[1]
        e2_ref[hh, :, pl.ds(c0, LANE)] = jnp.exp(s2 - top2)
        return carry

    lax.fori_loop(0, PEER_HEADS * n_chunk, body, 0)


def _peer_dense_kernel(hb_ref, u_ref, vt_ref, c1_ref, cnt1_ref, rank2_ref, e2_ref, x_ref, gm_ref,
                       o_ref, ht_ref, gh_ref, acc_ref):
    f32 = jnp.float32
    j = pl.program_id(1)
    tb = hb_ref.shape[0]

    @pl.when(j == 0)
    def _():
        acc_ref[...] = jnp.zeros_like(acc_ref)

    ht_ref[...] = lax.dot_general(u_ref[...], hb_ref[...], (((1,), (1,)), ((), ())), preferred_element_type=f32)

    def chunk_body(c, carry):
        lanes = pl.ds(pl.multiple_of(c * LANE, LANE), LANE)
        for r in range(PEER_ET // N_KEYS):
            g = jnp.zeros((N_KEYS, LANE), f32)
            for hh in range(PEER_HEADS):
                cnt_b = cnt1_ref[hh, r:r + 1, lanes]
                c_b = c1_ref[hh, r:r + 1, lanes]
                g = g + jnp.where(rank2_ref[hh, :, lanes] < cnt_b, e2_ref[hh, :, lanes] * c_b, 0.0)
            pre = ht_ref[r * N_KEYS:(r + 1) * N_KEYS, lanes]
            act = 0.5 * pre * (1.0 + lax.erf(pre * (2.0 ** -0.5)))
            gh_ref[r * N_KEYS:(r + 1) * N_KEYS, lanes] = (g * act).astype(jnp.bfloat16)
        return carry

    lax.fori_loop(0, tb // LANE, chunk_body, 0)
    acc_ref[...] += jnp.dot(vt_ref[...], gh_ref[...], preferred_element_type=f32)

    @pl.when(j == pl.num_programs(1) - 1)
    def _():
        upd = acc_ref[...].T.reshape(x_ref.shape)
        o_ref[...] = x_ref[...] + gm_ref[...] * upd


def _peer_sublayer(x, shift, scale, gate, norm_g, wq_b, k1p, k2p, u_b, vt_b):
    B, T, D = x.shape
    n = B * T
    tb = PEER_TB
    assert n % tb == 0
    if T % tb == 0:
        nbs, tper, per = 1, tb, T // tb
        xmap = lambda i, *_: (i // per, i % per, 0)
        mmap = lambda i, *_: (i // per, 0, 0)
    else:
        assert tb % T == 0 and T % 8 == 0
        nbs, tper = tb // T, T
        xmap = lambda i, *_: (i, 0, 0)
        mmap = lambda i, *_: (i, 0, 0)
    nblk = n // tb
    f32 = jnp.float32
    x_spec = pl.BlockSpec((nbs, tper, D), xmap)
    m_spec = pl.BlockSpec((nbs, 1, D), mmap)
    n_cand_pad = -(-len(_PEER_CAND) // 8) * 8
    a_of = jnp.asarray(np.broadcast_to(np.array([a for a, _ in _PEER_CAND] + [0] * (n_cand_pad - len(_PEER_CAND)),
                                                np.int32)[:, None], (n_cand_pad, LANE)))
    route_shape = jax.ShapeDtypeStruct((PEER_HEADS, N_KEYS, n), f32)
    route_spec = pl.BlockSpec((PEER_HEADS, N_KEYS, tb), lambda i: (0, 0, i))
    hb, c1, cnt1, rank2, e2 = pl.pallas_call(
        _peer_route_kernel,
        grid=(nblk,),
        in_specs=[x_spec, m_spec, m_spec,
                  pl.BlockSpec((1, D), lambda i: (0, 0)),
                  pl.BlockSpec((D, PEER_HEADS * PEER_QDIM), lambda i: (0, 0)),
                  pl.BlockSpec((PEER_HEADS, N_KEYS, PEER_QDIM), lambda i: (0, 0, 0)),
                  pl.BlockSpec((PEER_HEADS, N_KEYS, PEER_QDIM), lambda i: (0, 0, 0)),
                  pl.BlockSpec((n_cand_pad, LANE), lambda i: (0, 0))],
        out_specs=[pl.BlockSpec((tb, D), lambda i: (i, 0)), route_spec, route_spec, route_spec, route_spec],
        out_shape=[jax.ShapeDtypeStruct((n, D), jnp.bfloat16), route_shape, route_shape, route_shape, route_shape],
        scratch_shapes=[pltpu.VMEM((PEER_HEADS, tb, PEER_QDIM), jnp.bfloat16), pltpu.VMEM((PEER_TOPK, LANE), f32),
                        pltpu.VMEM((PEER_TOPK, LANE), f32), pltpu.VMEM((n_cand_pad, LANE), f32)],
        compiler_params=pltpu.CompilerParams(dimension_semantics=("parallel",), vmem_limit_bytes=VMEM_LIMIT),
        name="peer_route",
    )(x, shift, scale, norm_g.reshape(1, D), wq_b, k1p, k2p, a_of)

    rows = PEER_ET // N_KEYS
    sub_spec = pl.BlockSpec((PEER_HEADS, rows, tb), lambda i, j: (0, j, i))
    full_spec = pl.BlockSpec((PEER_HEADS, N_KEYS, tb), lambda i, j: (0, 0, i))
    return pl.pallas_call(
        _peer_dense_kernel,
        grid=(nblk, N_EXPERTS // PEER_ET),
        in_specs=[pl.BlockSpec((tb, D), lambda i, j: (i, 0)),
                  pl.BlockSpec((PEER_ET, D), lambda i, j: (j, 0)),
                  pl.BlockSpec((D, PEER_ET), lambda i, j: (0, j)),
                  sub_spec, sub_spec, full_spec, full_spec,
                  pl.BlockSpec((nbs, tper, D), lambda i, j: xmap(i)),
                  pl.BlockSpec((nbs, 1, D), lambda i, j: mmap(i))],
        out_specs=pl.BlockSpec((nbs, tper, D), lambda i, j: xmap(i)),
        out_shape=jax.ShapeDtypeStruct(x.shape, x.dtype),
        scratch_shapes=[pltpu.VMEM((PEER_ET, tb), f32), pltpu.VMEM((PEER_ET, tb), jnp.bfloat16),
                        pltpu.VMEM((D, tb), f32)],
        compiler_params=pltpu.CompilerParams(dimension_semantics=("parallel", "arbitrary"), vmem_limit_bytes=VMEM_LIMIT),
        name="peer_dense",
    )(hb, u_b, vt_b, c1, cnt1, rank2, e2, x, gate)


def _peer_weights(w_q, k1, k2, u_tab, v_tab):
    bf16 = jnp.bfloat16
    half = PEER_QDIM // 2
    k1p = jnp.pad(k1, ((0, 0), (0, 0), (0, half))).astype(bf16)
    k2p = jnp.pad(k2, ((0, 0), (0, 0), (half, 0))).astype(bf16)
    return w_q.astype(bf16), k1p, k2p, u_tab.astype(bf16), v_tab.astype(bf16).T


def kernel(x_prompt, x_sample, cache_nsa_kv, cache_diff_kv, state_nsa_window, state_gdn_S, state_gdn_conv,
           page_table, c_prompt, c_sample, ada_w, ada_b, norm_mix_g, norm_ffn_g, final_norm_g,
           gdn_w_in, gdn_conv_w, gdn_a_log, gdn_dt_bias, gdn_norm_g, gdn_w_out,
           nsa_w_in, nsa_cmp_pos, nsa_cmp_w, nsa_w_out,
           diff_w_in, diff_lq1, diff_lk1, diff_lq2, diff_lk2, diff_subln_g, diff_w_out,
           peer_w_q, peer_k1, peer_k2, peer_u, peer_v):
    past_len = page_table.shape[1] * cache_nsa_kv.shape[2]
    peer_w = [_peer_weights(peer_w_q[i], peer_k1[i], peer_k2[i], peer_u[i], peer_v[i]) for i in range(DEPTH)]

    def trunk(x, c, sample):
        B, T, _ = x.shape
        pos0 = past_len if sample else 0
        cs = jax.nn.silu(c)
        new_S, new_conv, new_nsa_kv, new_nsa_win, new_diff_kv = [], [], [], [], []
        for i in range(DEPTH):
            mod = (cs @ ada_w[i] + ada_b[i]).reshape(B, 6, 1, D_MODEL)
            h = rmsnorm(x, norm_mix_g[i]) * (1.0 + mod[:, 1]) + mod[:, 0]
            j = i // N_MIXERS
            kind = i % N_MIXERS
            if kind == 0:
                if sample:
                    S0 = state_gdn_S[j].astype(jnp.float32)
                    buf = state_gdn_conv[j]
                else:
                    S0 = jnp.zeros((B, GDN_HEADS, GDN_DK, GDN_DV), jnp.float32)
                    buf = jnp.zeros((B, CONV_W - 1, GDN_HEADS * (2 * GDN_DK + GDN_DV)), x.dtype)
                m, buf_n, S_n = gdn_mixer(h, buf, S0, gdn_w_in[j], gdn_conv_w[j], gdn_a_log[j], gdn_dt_bias[j], gdn_norm_g[j], gdn_w_out[j])
                new_S.append(S_n.astype(x.dtype))
                new_conv.append(buf_n)
            elif kind == 1:
                past = gather_pages(cache_nsa_kv[j], page_table) if sample else None
                wbuf = state_nsa_window[j] if sample else None
                m, kv_n, win_n = nsa_mixer(h, pos0, past, wbuf, nsa_w_in[j], nsa_cmp_pos[j], nsa_cmp_w[j], nsa_w_out[j])
                new_nsa_kv.append(kv_n)
                new_nsa_win.append(win_n)
            else:
                past = gather_pages(cache_diff_kv[j], page_table) if sample else None
                m, kv_n = diff_mixer(h, pos0, past, i, diff_w_in[j], diff_lq1[j], diff_lk1[j], diff_lq2[j], diff_lk2[j], diff_subln_g[j], diff_w_out[j])
                new_diff_kv.append(kv_n)
            x = x + mod[:, 2] * m
            x = _peer_sublayer(x, mod[:, 3], mod[:, 4], mod[:, 5], norm_ffn_g[i], *peer_w[i])
        y = rmsnorm(x, final_norm_g)
        return y, jnp.stack(new_S), jnp.stack(new_conv), jnp.stack(new_nsa_kv), jnp.stack(new_nsa_win), jnp.stack(new_diff_kv)

    y_prompt, p_gdn_S, p_gdn_conv, p_nsa_kv, p_nsa_win, p_diff_kv = trunk(x_prompt, c_prompt, False)
    y_sample, s_gdn_S, s_gdn_conv, s_nsa_kv, s_nsa_win, s_diff_kv = trunk(x_sample, c_sample, True)
    return (y_prompt, y_sample, p_gdn_S, p_gdn_conv, p_nsa_kv, p_nsa_win, p_diff_kv, s_gdn_S, s_gdn_conv, s_nsa_kv, s_nsa_win, s_diff_kv)
```

```python
import functools
import math

import jax
import jax.numpy as jnp
import numpy as np
from jax import lax
from jax.experimental import pallas as pl
from jax.experimental.pallas import tpu as pltpu

D_MODEL = 1024
DEPTH = 4
N_MIXERS = 3
HEAD_DIM = 128
GDN_HEADS = D_MODEL // HEAD_DIM
GDN_DK = HEAD_DIM
GDN_DV = HEAD_DIM
CONV_W = 4
GDN_CHUNK = 64
NSA_HEADS = D_MODEL // HEAD_DIM
NSA_KV_GROUPS = 2
CMP_BLOCK = 32
SEL_BLOCK = 64
TOP_N = 16
WINDOW = 512
Q_BLOCK = 128
DIFF_HEADS = D_MODEL // HEAD_DIM
DIFF_DH = D_MODEL // DIFF_HEADS // 2
PEER_HEADS = 8
N_KEYS = 128
N_EXPERTS = N_KEYS * N_KEYS
PEER_TOPK = 16
PEER_QDIM = 128
PEER_TOKEN_BLOCK = 256
ROPE_THETA = 10000.0
EPS = 1e-6

LANE = 128
VMEM_LIMIT = 48 * 1024 * 1024


def _mm_kernel(x_ref, w_ref, o_ref):
    o_ref[...] = jnp.dot(x_ref[...].astype(jnp.bfloat16), w_ref[...], preferred_element_type=jnp.float32)


def _mm(x, w, tm=256):
    lead = x.shape[:-1]
    K = x.shape[-1]
    N = w.shape[1]
    x2 = x.reshape(-1, K)
    M = x2.shape[0]
    n_pad = -(-N // 256) * 256
    wb = jnp.pad(w, ((0, 0), (0, n_pad - N))).astype(jnp.bfloat16)
    tm = min(tm, M)
    assert M % tm == 0
    out = pl.pallas_call(
        _mm_kernel,
        grid=(M // tm,),
        in_specs=[pl.BlockSpec((tm, K), lambda i: (i, 0)), pl.BlockSpec((K, n_pad), lambda i: (0, 0))],
        out_specs=pl.BlockSpec((tm, n_pad), lambda i: (i, 0)),
        out_shape=jax.ShapeDtypeStruct((M, n_pad), jnp.float32),
        compiler_params=pltpu.CompilerParams(dimension_semantics=("parallel",), vmem_limit_bytes=VMEM_LIMIT),
        name="proj_matmul",
    )(x2, wb)
    return out[:, :N].reshape(*lead, N)


def rmsnorm(x, g):
    xf = x.astype(jnp.float32)
    y = xf * lax.rsqrt(jnp.mean(xf * xf, -1, keepdims=True) + EPS)
    return (y * g.astype(jnp.float32)).astype(x.dtype)


def l2norm(x):
    xf = x.astype(jnp.float32)
    return (xf * lax.rsqrt(jnp.sum(xf * xf, -1, keepdims=True) + EPS)).astype(x.dtype)


def rope(x, pos):
    half = x.shape[-1] // 2
    inv = ROPE_THETA ** (-jnp.arange(half, dtype=jnp.float32) / half)
    ang = pos.astype(jnp.float32)[:, None] * inv[None, :]
    cos = jnp.cos(ang)[:, None, :]
    sin = jnp.sin(ang)[:, None, :]
    xf = x.astype(jnp.float32)
    x1, x2 = xf[..., :half], xf[..., half:]
    return jnp.concatenate([x1 * cos - x2 * sin, x2 * cos + x1 * sin], -1).astype(x.dtype)


def masked_softmax(s, mask):
    s = jnp.where(mask, s.astype(jnp.float32), -jnp.inf)
    m = jnp.max(s, -1, keepdims=True)
    m = jnp.where(jnp.isfinite(m), m, 0.0)
    p = jnp.exp(s - m)
    return p / jnp.maximum(jnp.sum(p, -1, keepdims=True), 1e-30)


def gather_pages(pool, page_table):
    g = pool[page_table]
    return g.reshape(g.shape[0], g.shape[1] * g.shape[2], *pool.shape[2:])


def causal_conv(x, buf, w):
    T = x.shape[1]
    xx = jnp.concatenate([buf, x], 1)
    y = xx[:, 0:T] * w[0]
    for j in range(1, CONV_W):
        y = y + xx[:, j:j + T] * w[j]
    return jax.nn.silu(y), xx[:, -(CONV_W - 1):]


def gated_delta_rule(q, k, v, g, beta, S0):
    f32 = jnp.float32
    B, T, H, dk = q.shape
    dv = v.shape[-1]
    C = GDN_CHUNK
    Tp = -(-T // C) * C

    def prep(a):
        a = a.astype(f32)
        a = jnp.pad(a, [(0, 0), (0, Tp - T)] + [(0, 0)] * (a.ndim - 2))
        a = a.reshape(B, Tp // C, C, *a.shape[2:])
        return jnp.moveaxis(a, (1, 3), (0, 2))

    qc = prep(q) * dk ** -0.5
    kc, vc, gc, bc = prep(k), prep(v), prep(g), prep(beta)
    gcum = jnp.cumsum(gc, -1)
    idx = jnp.arange(C)
    incl = idx[:, None] >= idx[None, :]
    strict = idx[:, None] > idx[None, :]
    decay = jnp.exp(jnp.where(incl, gcum[..., :, None] - gcum[..., None, :], -jnp.inf))
    kb = kc * bc[..., None]
    L = jnp.where(strict, jnp.einsum('nbhid,nbhjd->nbhij', kb, kc) * decay, 0.0)
    eye = jnp.eye(C, dtype=f32)
    A = eye + L
    Tinv = lax.linalg.triangular_solve(A, jnp.broadcast_to(eye, A.shape), left_side=True, lower=True, unit_diagonal=True)
    u = Tinv @ (vc * bc[..., None])
    w = Tinv @ (kb * jnp.exp(gcum)[..., None])
    a_intra = jnp.einsum('nbhid,nbhjd->nbhij', qc, kc) * decay

    def step(S, xs):
        q_i, k_i, u_i, w_i, a_i, g_i = xs
        v_new = u_i - w_i @ S
        o = (q_i * jnp.exp(g_i)[..., None]) @ S + a_i @ v_new
        g_last = g_i[..., -1:]
        S = S * jnp.exp(g_last)[..., None] + jnp.einsum('bhck,bhcv->bhkv', k_i * jnp.exp(g_last - g_i)[..., None], v_new)
        return S, o

    S, o = lax.scan(step, S0.astype(f32), (qc, kc, u, w, a_intra, gcum))
    o = jnp.moveaxis(o, (0, 2), (1, 3)).reshape(B, Tp, H, dv)[:, :T]
    return o, S


def gdn_mixer(h, conv_buf, S0, w_in, conv_w, a_log, dt_bias, norm_g, w_out):
    B, T, _ = h.shape
    H, dk, dv = GDN_HEADS, GDN_DK, GDN_DV
    n_qkv = H * (2 * dk + dv)
    f32 = jnp.float32
    proj = _mm(h, w_in)
    qkv, conv_new = causal_conv(proj[..., :n_qkv], conv_buf, conv_w)
    z = proj[..., n_qkv:n_qkv + H * dv].reshape(B, T, H, dv)
    b_raw = proj[..., n_qkv + H * dv:n_qkv + H * dv + H]
    a_raw = proj[..., n_qkv + H * dv + H:]
    q = l2norm(qkv[..., :H * dk].reshape(B, T, H, dk))
    k = l2norm(qkv[..., H * dk:2 * H * dk].reshape(B, T, H, dk))
    v = qkv[..., 2 * H * dk:].reshape(B, T, H, dv)
    beta = jax.nn.sigmoid(b_raw.astype(f32))
    g = -jnp.exp(a_log.astype(f32)) * jax.nn.softplus(a_raw.astype(f32) + dt_bias.astype(f32))
    o, S = _gdn_delta_rule(q.reshape(B, T, H * dk), k.reshape(B, T, H * dk), v.reshape(B, T, H * dv),
                           z.reshape(B, T, H * dv), g, beta, S0, norm_g)
    return _mm(o, w_out), conv_new, S


def _dot_split3(a, b):
    f32, bf16 = jnp.float32, jnp.bfloat16
    ah = a.astype(bf16)
    al = (a - ah.astype(f32)).astype(bf16)
    bh = b.astype(bf16)
    bl = (b - bh.astype(f32)).astype(bf16)
    dot = functools.partial(jnp.dot, preferred_element_type=f32)
    return dot(ah, bh) + dot(ah, bl) + dot(al, bh)


def _gdn_chunk_kernel(q_ref, k_ref, v_ref, z_ref, col_ref, row_ref, s0_ref, ng_ref, o_ref, sout_ref, s_scr):
    f32, bf16 = jnp.float32, jnp.bfloat16
    n = pl.program_id(1)
    C = q_ref.shape[1]
    H, dk, dv = GDN_HEADS, GDN_DK, GDN_DV
    dot = functools.partial(jnp.dot, preferred_element_type=f32)
    nt = (((1,), (1,)), ((), ()))

    @pl.when(n == 0)
    def _():
        s_scr[...] = s0_ref[0]

    ii = lax.broadcasted_iota(jnp.int32, (C, C), 0)
    jj = lax.broadcasted_iota(jnp.int32, (C, C), 1)
    incl = ii >= jj
    strict = ii > jj
    eye = (ii == jj).astype(f32)
    for h in range(H):
        ks = slice(h * dk, (h + 1) * dk)
        vs = slice(h * dv, (h + 1) * dv)
        qh = q_ref[0, :, ks] * dk ** -0.5
        kh = k_ref[0, :, ks]
        vh = v_ref[0, :, vs]
        gc = col_ref[0, :, h:h + 1]
        bc = col_ref[0, :, H + h:H + h + 1]
        gr = row_ref[0, 0, h:h + 1, :]
        decay = jnp.exp(jnp.where(incl, gc - gr, -jnp.inf))
        kb = kh * bc
        kcb = kh.astype(bf16)
        lower = jnp.where(strict, lax.dot_general(kb.astype(bf16), kcb, nt, preferred_element_type=f32) * decay, 0.0)
        mpow = -lower
        tinv = eye + mpow
        for _ in range(C.bit_length() - 2):
            mpow = _dot_split3(mpow, mpow)
            tinv = tinv + _dot_split3(tinv, mpow)
        eg = jnp.exp(gc)
        uw = dot(tinv.astype(bf16), jnp.concatenate([vh * bc, kb * eg], axis=1).astype(bf16))
        u, w = uw[:, :dv], uw[:, dv:]
        a_intra = lax.dot_general(qh.astype(bf16), kcb, nt, preferred_element_type=f32) * decay
        s_old = s_scr[h]
        s_b = s_old.astype(bf16)
        v_new = u - dot(w.astype(bf16), s_b)
        v_nb = v_new.astype(bf16)
        o = dot((qh * eg).astype(bf16), s_b) + dot(a_intra.astype(bf16), v_nb)
        g_last = gr[:, C - 1:C]
        kd = (kh * jnp.exp(g_last - gc)).T.astype(bf16)
        s_scr[h] = s_old * jnp.exp(g_last) + dot(kd, v_nb)
        y = o * lax.rsqrt(jnp.mean(o * o, -1, keepdims=True) + EPS) * ng_ref[...]
        zz = z_ref[0, :, vs]
        o_ref[0, :, vs] = y * (zz * jax.nn.sigmoid(zz))

    @pl.when(n == pl.num_programs(1) - 1)
    def _():
        sout_ref[0] = s_scr[...]


def _gdn_delta_rule(q, k, v, z, g, beta, S0, norm_g):
    B, T, _ = q.shape
    H, dk, dv = GDN_HEADS, GDN_DK, GDN_DV
    C = GDN_CHUNK if T % GDN_CHUNK == 0 else T
    assert C & (C - 1) == 0 and C % 8 == 0
    nC = T // C
    f32 = jnp.float32
    gcum = jnp.cumsum(g.astype(f32).reshape(B, nC, C, H), axis=2)
    col = jnp.concatenate([gcum.reshape(B, T, H), beta.astype(f32)], axis=-1)
    row = jnp.swapaxes(gcum, 2, 3)
    tok = lambda w: pl.BlockSpec((1, C, w), lambda b, n: (b, n, 0))
    o, s_out = pl.pallas_call(
        _gdn_chunk_kernel,
        grid=(B, nC),
        in_specs=[tok(H * dk), tok(H * dk), tok(H * dv), tok(H * dv), tok(2 * H),
                  pl.BlockSpec((1, 1, H, C), lambda b, n: (b, n, 0, 0)),
                  pl.BlockSpec((1, H, dk, dv), lambda b, n: (b, 0, 0, 0)),
                  pl.BlockSpec((1, dv), lambda b, n: (0, 0))],
        out_specs=[tok(H * dv), pl.BlockSpec((1, H, dk, dv), lambda b, n: (b, 0, 0, 0))],
        out_shape=[jax.ShapeDtypeStruct((B, T, H * dv), f32), jax.ShapeDtypeStruct((B, H, dk, dv), f32)],
        scratch_shapes=[pltpu.VMEM((H, dk, dv), f32)],
        compiler_params=pltpu.CompilerParams(dimension_semantics=("parallel", "arbitrary"), vmem_limit_bytes=VMEM_LIMIT),
        name="gdn_delta_rule",
    )(q, k, v, z, col, row, S0.astype(f32), norm_g.reshape(1, dv))
    return o, s_out


def nsa_seq(q_s, g_s, full_s, win_s, pos0, wpos0, cmp_pos, cmp_w):
    dt = q_s.dtype
    T = q_s.shape[0]
    G, R, dh = NSA_KV_GROUPS, NSA_HEADS // NSA_KV_GROUPS, HEAD_DIM
    Tk = full_s.shape[0]
    Tkp = -(-Tk // SEL_BLOCK) * SEL_BLOCK
    full_p = jnp.pad(full_s, ((0, Tkp - Tk), (0, 0), (0, 0), (0, 0)))
    n_cmp = Tkp // CMP_BLOCK
    n_sel = Tkp // SEL_BLOCK
    blocks = full_p[:, :2].reshape(n_cmp, CMP_BLOCK, 2, G, dh) + jnp.transpose(cmp_pos, (1, 0, 2))[:, :, None, :]
    kv_cmp = jnp.einsum('nlsgd,slde->nsge', blocks, cmp_w)
    k_cmp, v_cmp = kv_cmp[:, 0], kv_cmp[:, 1]
    cmp_end = (jnp.arange(n_cmp) + 1) * CMP_BLOCK - 1
    sel = jnp.transpose(full_p[:, 2:].reshape(n_sel, SEL_BLOCK, 2, G, dh), (2, 3, 0, 1, 4))
    k_selb, v_selb = sel[0], sel[1]
    n_top = min(TOP_N, n_sel)
    win_p = jnp.pad(win_s, ((WINDOW, 0), (0, 0), (0, 0), (0, 0)))
    qb = Q_BLOCK if T % Q_BLOCK == 0 else T
    nqb = T // qb
    lw = WINDOW + qb - 1
    scale = HEAD_DIM ** -0.5
    g_idx = jnp.arange(G)[None, :, None]
    blk_ids = jnp.arange(n_sel)

    def block(args):
        qi, gi, bi = args
        qstart = pos0 + bi * qb
        qpos = qstart + jnp.arange(qb)
        qg = qi.reshape(qb, G, R, dh) * scale
        s_c = jnp.einsum('qgrd,ngd->qgrn', qg, k_cmp)
        p_c = masked_softmax(s_c, (cmp_end[None, :] <= qpos[:, None])[:, None, None, :])
        o_c = jnp.einsum('qgrn,ngd->qgrd', p_c.astype(dt), v_cmp)
        imp = p_c.sum(2).reshape(qb, G, n_sel, SEL_BLOCK // CMP_BLOCK).sum(-1)
        cur = (qpos // SEL_BLOCK)[:, None] == blk_ids[None, :]
        causal_blk = (blk_ids * SEL_BLOCK)[None, :] <= qpos[:, None]
        imp = jnp.where(cur[:, None], jnp.inf, jnp.where(causal_blk[:, None], imp, -jnp.inf))
        top_v, top_i = lax.top_k(imp, n_top)
        k_sel = k_selb[g_idx, top_i]
        v_sel = v_selb[g_idx, top_i]
        kpos = top_i[..., None] * SEL_BLOCK + jnp.arange(SEL_BLOCK)
        m_s = (kpos <= qpos[:, None, None, None]) & (top_v > -jnp.inf)[..., None]
        s_s = jnp.einsum('qgrd,qgksd->qgrks', qg, k_sel).reshape(qb, G, R, n_top * SEL_BLOCK)
        p_s = masked_softmax(s_s, m_s.reshape(qb, G, 1, n_top * SEL_BLOCK))
        o_s = jnp.einsum('qgrm,qgmd->qgrd', p_s.astype(dt), v_sel.reshape(qb, G, n_top * SEL_BLOCK, dh))
        wblk = lax.dynamic_slice_in_dim(win_p, qstart - wpos0 + 1, lw, axis=0)
        wpos = qstart - WINDOW + 1 + jnp.arange(lw)
        m_w = (wpos[None] <= qpos[:, None]) & (wpos[None] > qpos[:, None] - WINDOW) & (wpos[None] >= wpos0)
        s_w = jnp.einsum('qgrd,kgd->qgrk', qg, wblk[:, 0])
        p_w = masked_softmax(s_w, m_w[:, None, None, :])
        o_w = jnp.einsum('qgrk,kgd->qgrd', p_w.astype(dt), wblk[:, 1])
        gg = gi.reshape(qb, G, R, 3)
        o = gg[..., 0:1] * o_c + gg[..., 1:2] * o_s + gg[..., 2:3] * o_w
        return o.reshape(qb, NSA_HEADS, dh)

    out = lax.map(block, (q_s.reshape(nqb, qb, NSA_HEADS, dh), g_s.reshape(nqb, qb, NSA_HEADS, 3), jnp.arange(nqb)))
    return out.reshape(T, NSA_HEADS, dh)


NSA_R = NSA_HEADS // NSA_KV_GROUPS
NSA_TK = 512
NSA_WSPAN = WINDOW + Q_BLOCK


def _softmax_rows(s, mask):
    s = jnp.where(mask, s, -jnp.inf)
    m = jnp.max(s, axis=0, keepdims=True)
    m = jnp.where(m == -jnp.inf, 0.0, m)
    p = jnp.exp(s - m)
    return p * (1.0 / jnp.maximum(jnp.sum(p, axis=0, keepdims=True), 1e-30))


def _nsa_cmp_kernel(x_ref, pos_ref, w_ref, o_ref):
    xb = (x_ref[0, 0, 0, 0] + pos_ref[0]).astype(jnp.bfloat16)
    o_ref[0, 0, 0, 0] = jnp.dot(xb, w_ref[0], preferred_element_type=jnp.float32)


def _nsa_compress(rows_cmp, cmp_pos, cmp_w):
    B, T, _, G, dh = rows_cmp.shape
    n_sel = T // SEL_BLOCK
    ld = CMP_BLOCK * dh
    x = rows_cmp.reshape(B, n_sel, 2, CMP_BLOCK, 2, G, dh)
    x = jnp.transpose(x, (0, 4, 5, 2, 1, 3, 6)).reshape(B, 2, G, 2, n_sel, ld)
    return pl.pallas_call(
        _nsa_cmp_kernel,
        grid=(B, 2, G, 2),
        in_specs=[pl.BlockSpec((1, 1, 1, 1, n_sel, ld), lambda b, s, g, p: (b, s, g, p, 0, 0)),
                  pl.BlockSpec((1, 1, ld), lambda b, s, g, p: (s, 0, 0)),
                  pl.BlockSpec((1, ld, dh), lambda b, s, g, p: (s, 0, 0))],
        out_specs=pl.BlockSpec((1, 1, 1, 1, n_sel, dh), lambda b, s, g, p: (b, s, g, p, 0, 0)),
        out_shape=jax.ShapeDtypeStruct((B, 2, G, 2, n_sel, dh), jnp.float32),
        compiler_params=pltpu.CompilerParams(dimension_semantics=("parallel",) * 4, vmem_limit_bytes=VMEM_LIMIT),
        name="nsa_compress",
    )(x, cmp_pos.reshape(2, 1, ld), cmp_w.reshape(2, ld, dh).astype(jnp.bfloat16))


def _nsa_prompt_kernel(q_ref, gt_ref, kc_ref, vct_ref, ks_ref, vst_ref, kw_ref, vwt_ref, o_ref,
                       sel_scr, m_scr, l_scr, acc_scr):
    f32, bf16 = jnp.float32, jnp.bfloat16
    i = pl.program_id(2)
    R, QB = NSA_R, Q_BLOCK
    W = R * QB
    T = ks_ref.shape[2]
    n_sel = T // SEL_BLOCK
    nt = (((1,), (1,)), ((), ()))
    qb = q_ref[0]
    q = jnp.concatenate([qb[:, r * HEAD_DIM:(r + 1) * HEAD_DIM] for r in range(R)], axis=0)
    q = (q * HEAD_DIM ** -0.5).astype(bf16)
    lane = lax.broadcasted_iota(jnp.int32, (1, W), 1)
    qpos = i * QB + (lane & (QB - 1))

    s = lax.dot_general(kc_ref[0, 0], q, nt, preferred_element_type=f32)
    row = lax.broadcasted_iota(jnp.int32, (2 * n_sel, 1), 0)
    cidx = jnp.where(row < n_sel, 2 * row, 2 * (row - n_sel) + 1)
    p_c = _softmax_rows(s, (cidx + 1) * CMP_BLOCK - 1 <= qpos)
    o_c = jnp.dot(vct_ref[0, 0], p_c.astype(bf16), preferred_element_type=f32)

    p_r = p_c[:, 0:QB]
    for r in range(1, R):
        p_r = p_r + p_c[:, r * QB:(r + 1) * QB]
    imp = p_r[:n_sel] + p_r[n_sel:]
    blk = lax.broadcasted_iota(jnp.int32, (n_sel, QB), 0)
    qp = qpos[:, :QB]
    imp = jnp.where(blk == qp // SEL_BLOCK, jnp.inf, jnp.where(blk * SEL_BLOCK <= qp, imp, -jnp.inf))
    chosen = [jnp.zeros((n_sel, QB), f32)]

    def pick(r, m, sel, chosen=chosen):
        chosen[0] = jnp.where(sel & (m > -jnp.inf), 1.0, chosen[0])

    _extract_top(imp, min(TOP_N, n_sel), pick)
    sel_scr[...] = jnp.concatenate([chosen[0]] * R, axis=1)

    m_scr[...] = jnp.full(m_scr.shape, -jnp.inf, f32)
    l_scr[...] = jnp.zeros(l_scr.shape, f32)
    acc_scr[...] = jnp.zeros(acc_scr.shape, f32)
    blk_per_tile = NSA_TK // SEL_BLOCK

    def tile_body(kt, carry):
        k0 = pl.multiple_of(kt * NSA_TK, NSA_TK)
        s = lax.dot_general(ks_ref[0, 0, pl.ds(k0, NSA_TK), :], q, nt, preferred_element_type=f32)
        selrows = sel_scr[pl.ds(pl.multiple_of(kt * blk_per_tile, blk_per_tile), blk_per_tile), :]
        kpos = k0 + lax.broadcasted_iota(jnp.int32, (NSA_TK, 1), 0)
        parts = []
        for j in range(blk_per_tile):
            rows = slice(j * SEL_BLOCK, (j + 1) * SEL_BLOCK)
            ok = (selrows[j:j + 1, :] > 0.0) & (kpos[rows] <= qpos)
            parts.append(jnp.where(ok, s[rows], -jnp.inf))
        _online_softmax_tile(jnp.concatenate(parts, axis=0), vst_ref[0, 0, :, pl.ds(k0, NSA_TK)], m_scr, l_scr, acc_scr)
        return carry

    lax.fori_loop(0, (i * QB + QB + NSA_TK - 1) // NSA_TK, tile_body, 0)
    o_s = acc_scr[...] * (1.0 / jnp.maximum(l_scr[...], 1e-30))

    w0 = pl.multiple_of(jnp.clip(i * QB + QB - NSA_WSPAN, 0, T - NSA_WSPAN), QB)
    s = lax.dot_general(kw_ref[0, 0, pl.ds(w0, NSA_WSPAN), :], q, nt, preferred_element_type=f32)
    kpos = w0 + lax.broadcasted_iota(jnp.int32, (NSA_WSPAN, 1), 0)
    p_w = _softmax_rows(s, (kpos <= qpos) & (kpos > qpos - WINDOW))
    o_w = jnp.dot(vwt_ref[0, 0, :, pl.ds(w0, NSA_WSPAN)], p_w.astype(bf16), preferred_element_type=f32)

    g = gt_ref[0, 0, 0]
    o = g[0:1] * o_c + g[1:2] * o_s + g[2:3] * o_w
    for r in range(R):
        o_ref[0, :, r * HEAD_DIM:(r + 1) * HEAD_DIM] = o[:, r * QB:(r + 1) * QB].T


def _nsa_prompt_attention(q, gates, new_rows, win, cmp_pos, cmp_w):
    B, T, _ = q.shape
    G, R, dh, QB = NSA_KV_GROUPS, NSA_R, HEAD_DIM, Q_BLOCK
    assert T % NSA_TK == 0 and T >= NSA_WSPAN and (T // SEL_BLOCK) % 8 == 0
    nqb = T // QB
    n_sel = T // SEL_BLOCK
    W = R * QB
    bf16 = jnp.bfloat16
    cmp = _nsa_compress(new_rows[:, :, :2], cmp_pos, cmp_w).reshape(B, 2, G, 2 * n_sel, dh)
    kc = cmp[:, 0].astype(bf16)
    vct = jnp.swapaxes(cmp[:, 1], -1, -2).astype(bf16)
    ks = jnp.transpose(new_rows[:, :, 2], (0, 2, 1, 3)).astype(bf16)
    vst = jnp.transpose(new_rows[:, :, 3], (0, 2, 3, 1)).astype(bf16)
    kw = jnp.transpose(win[:, :, 0], (0, 2, 1, 3)).astype(bf16)
    vwt = jnp.transpose(win[:, :, 1], (0, 2, 3, 1)).astype(bf16)
    gt = jnp.transpose(gates.reshape(B, nqb, QB, G, R, 3), (0, 3, 1, 5, 4, 2)).reshape(B, G, nqb, 3, W)
    gt = jnp.pad(gt, ((0, 0), (0, 0), (0, 0), (0, 5), (0, 0)))
    per_bg = lambda *shape: pl.BlockSpec((1, 1) + shape, lambda b, g, i: (b, g, 0, 0))
    return pl.pallas_call(
        _nsa_prompt_kernel,
        grid=(B, G, nqb),
        in_specs=[pl.BlockSpec((1, QB, R * dh), lambda b, g, i: (b, i, g)),
                  pl.BlockSpec((1, 1, 1, 8, W), lambda b, g, i: (b, g, i, 0, 0)),
                  per_bg(2 * n_sel, dh), per_bg(dh, 2 * n_sel), per_bg(T, dh), per_bg(dh, T), per_bg(T, dh), per_bg(dh, T)],
        out_specs=pl.BlockSpec((1, QB, R * dh), lambda b, g, i: (b, i, g)),
        out_shape=jax.ShapeDtypeStruct(q.shape, jnp.float32),
        scratch_shapes=[pltpu.VMEM((n_sel, W), jnp.float32), pltpu.VMEM((1, W), jnp.float32),
                        pltpu.VMEM((1, W), jnp.float32), pltpu.VMEM((dh, W), jnp.float32)],
        compiler_params=pltpu.CompilerParams(dimension_semantics=("parallel", "parallel", "arbitrary"),
                                             vmem_limit_bytes=VMEM_LIMIT),
        name="nsa_prompt",
    )(q, gt, kc, vct, ks, vst, kw, vwt)


_TN = (((0,), (0,)), ((), ()))
_NT = (((1,), (1,)), ((), ()))


def _nsa_sample_kernel(pt_ref, xc_ref, *refs, n_pages, past_len):
    pages = refs[:n_pages]
    q_ref, rows_ref, wnew_ref, wbuf_ref, gt_ref, pos_ref, w_ref, o_ref, sel_scr = refs[n_pages:]
    f32, bf16 = jnp.float32, jnp.bfloat16
    G, R, dh = NSA_KV_GROUPS, NSA_R, HEAD_DIM
    T = q_ref.shape[1]
    n_blk = past_len // SEL_BLOCK
    n_w = wbuf_ref.shape[1]
    lane = lax.broadcasted_iota(jnp.int32, (1, LANE), 1)
    qidx = lane & (T - 1)
    qpos = past_len + qidx
    new_row = lax.broadcasted_iota(jnp.int32, (LANE, 1), 0)
    new_ok = (new_row < T) & (new_row <= qidx)

    def new_tile(ref, col):
        return jnp.concatenate([ref[0, :, col * dh:(col + 1) * dh], jnp.zeros((LANE - T, dh), f32)], axis=0).astype(bf16)

    cmp = []
    for s in range(2):
        xs = jnp.concatenate(
            [jnp.concatenate([xc_ref[0, :, l * 4 * dh + (s * G + g) * dh:l * 4 * dh + (s * G + g + 1) * dh]
                              for l in range(CMP_BLOCK)], axis=1) for g in range(G)], axis=0)
        cmp.append(jnp.dot((xs + pos_ref[s]).astype(bf16), w_ref[s], preferred_element_type=f32))
    for g in range(G):
        q = jnp.concatenate([q_ref[0, :, (g * R + r) * dh:(g * R + r + 1) * dh] for r in range(R)]
                            + [jnp.zeros((LANE - R * T, dh), f32)], axis=0)
        q = (q * dh ** -0.5).astype(bf16)
        kc = cmp[0][g * 2 * n_blk:(g + 1) * 2 * n_blk]
        vc = cmp[1][g * 2 * n_blk:(g + 1) * 2 * n_blk]
        s_c = lax.dot_general(kc.astype(bf16), q, _NT, preferred_element_type=f32)
        p_c = _softmax_rows(s_c, jnp.full(s_c.shape, True))
        o_c = lax.dot_general(vc.astype(bf16), p_c.astype(bf16), _TN, preferred_element_type=f32)
        p_r = p_c
        for r in range(1, R):
            p_r = p_r + pltpu.roll(p_c, LANE - r * T, axis=1)
        imp = p_r[:n_blk] + p_r[n_blk:]
        chosen = [jnp.zeros((n_blk, LANE), f32)]

        def pick(_, m, sel, chosen=chosen):
            chosen[0] = jnp.where(sel, 1.0, chosen[0])

        _extract_top(imp, min(TOP_N, n_blk + 1) - 1, pick)
        ch = jnp.where(lane < T, chosen[0], 0.0)
        ch4 = ch
        for r in range(1, R):
            ch4 = ch4 + pltpu.roll(ch, r * T, axis=1)
        sel_scr[...] = ch4
        ks = jnp.concatenate([pg[0, :, g * dh:(g + 1) * dh] for pg in pages], axis=0).astype(bf16)
        vs = jnp.concatenate([pg[0, :, (G + g) * dh:(G + g + 1) * dh] for pg in pages], axis=0).astype(bf16)
        s_p = lax.dot_general(ks, q, _NT, preferred_element_type=f32)
        s_p = jnp.concatenate([jnp.where(sel_scr[j:j + 1, :] > 0.0, s_p[j * SEL_BLOCK:(j + 1) * SEL_BLOCK], -jnp.inf)
                               for j in range(n_blk)], axis=0)
        s_n = lax.dot_general(new_tile(rows_ref, 2 * G + g), q, _NT, preferred_element_type=f32)
        s_all = jnp.concatenate([s_p, jnp.where(new_ok, s_n, -jnp.inf)], axis=0)
        p_s = _softmax_rows(s_all, s_all > -jnp.inf).astype(bf16)
        o_s = (lax.dot_general(vs, p_s[:past_len], _TN, preferred_element_type=f32)
               + lax.dot_general(new_tile(rows_ref, 3 * G + g), p_s[past_len:], _TN, preferred_element_type=f32))
        kw = wbuf_ref[0, :, g * dh:(g + 1) * dh].astype(bf16)
        vw = wbuf_ref[0, :, (G + g) * dh:(G + g + 1) * dh].astype(bf16)
        s_w = lax.dot_general(kw, q, _NT, preferred_element_type=f32)
        wpos = past_len - n_w + lax.broadcasted_iota(jnp.int32, (n_w, 1), 0)
        s_w = jnp.where(wpos > qpos - WINDOW, s_w, -jnp.inf)
        s_n = lax.dot_general(new_tile(wnew_ref, g), q, _NT, preferred_element_type=f32)
        s_all = jnp.concatenate([s_w, jnp.where(new_ok, s_n, -jnp.inf)], axis=0)
        p_w = _softmax_rows(s_all, s_all > -jnp.inf).astype(bf16)
        o_w = (lax.dot_general(vw, p_w[:n_w], _TN, preferred_element_type=f32)
               + lax.dot_general(new_tile(wnew_ref, G + g), p_w[n_w:], _TN, preferred_element_type=f32))
        gt = gt_ref[0, g]
        o = (gt[0:1] * o_c + gt[1:2] * o_s + gt[2:3] * o_w).T
        for r in range(R):
            o_ref[0, :, (g * R + r) * dh:(g * R + r + 1) * dh] = o[r * T:(r + 1) * T]


def _nsa_sample_attention(q, gates, new_rows, win_new, cache, page_table, wbuf, cmp_pos, cmp_w):
    B, T, _ = q.shape
    G, R, dh = NSA_KV_GROUPS, NSA_R, HEAD_DIM
    n_pool, page = cache.shape[:2]
    n_pages = page_table.shape[1]
    past_len = n_pages * page
    n_w = wbuf.shape[1]
    assert T == 8 and R * T <= LANE and page % SEL_BLOCK == 0 and n_w + T > WINDOW
    assert past_len % SEL_BLOCK == 0 and T <= CMP_BLOCK
    row_w = 4 * G * dh
    per_page = page // SEL_BLOCK
    f32 = jnp.float32
    xc = cache.reshape(n_pool, per_page, 2, CMP_BLOCK, row_w)[page_table][..., :2 * G * dh]
    xc = jnp.transpose(xc, (0, 3, 1, 2, 4, 5)).reshape(B, 2 * n_pages * per_page, CMP_BLOCK * 2 * G * dh)
    n_blk = past_len // SEL_BLOCK
    gt = jnp.transpose(gates.reshape(B, T, G, R, 3), (0, 2, 4, 3, 1)).reshape(B, G, 3, R * T)
    gt = jnp.pad(gt, ((0, 0), (0, 0), (0, 5), (0, LANE - R * T)))
    cache2 = cache.reshape(n_pool, page, row_w)
    page_spec = lambda p: pl.BlockSpec((1, page, 2 * G * dh), lambda b, pt: (pt[b, p], 0, 1))
    seq = lambda *shape: pl.BlockSpec((1,) + shape, lambda b, pt: (b,) + (0,) * len(shape))
    whole = lambda *shape: pl.BlockSpec(shape, lambda b, pt: (0,) * len(shape))
    grid_spec = pltpu.PrefetchScalarGridSpec(
        num_scalar_prefetch=1,
        grid=(B,),
        in_specs=[seq(2 * n_blk, CMP_BLOCK * 2 * G * dh)] + [page_spec(p) for p in range(n_pages)]
                 + [seq(T, R * G * dh), seq(T, row_w), seq(T, 2 * G * dh), seq(n_w, 2 * G * dh), seq(G, 8, LANE),
                    whole(2, 1, CMP_BLOCK * dh), whole(2, CMP_BLOCK * dh, dh)],
        out_specs=seq(T, R * G * dh),
        scratch_shapes=[pltpu.VMEM((n_blk, LANE), f32)],
    )
    return pl.pallas_call(
        functools.partial(_nsa_sample_kernel, n_pages=n_pages, past_len=past_len),
        grid_spec=grid_spec,
        out_shape=jax.ShapeDtypeStruct(q.shape, f32),
        compiler_params=pltpu.CompilerParams(dimension_semantics=("parallel",), vmem_limit_bytes=VMEM_LIMIT),
        name="nsa_sample",
    )(page_table, xc, *([cache2] * n_pages), q, new_rows.reshape(B, T, row_w), win_new.reshape(B, T, 2 * G * dh),
      wbuf.reshape(B, n_w, 2 * G * dh), gt, cmp_pos.reshape(2, 1, CMP_BLOCK * dh),
      cmp_w.reshape(2, CMP_BLOCK * dh, dh).astype(jnp.bfloat16))


def nsa_mixer(h, pos0, past, wbuf, w_in, cmp_pos, cmp_w, w_out):
    B, T, _ = h.shape
    G, dh = NSA_KV_GROUPS, HEAD_DIM
    qd = NSA_HEADS * dh
    kvd = 6 * G * dh
    pos = pos0 + jnp.arange(T, dtype=jnp.int32)
    proj = _mm(h, w_in)
    q = rope(proj[..., :qd].reshape(B, T, NSA_HEADS, dh), pos)
    kv = proj[..., qd:qd + kvd].reshape(B, T, 6, G, dh)
    gates = jax.nn.sigmoid(proj[..., qd + kvd:].reshape(B, T, NSA_HEADS, 3))
    keys = rope(kv[:, :, 0::2].reshape(B, T, 3 * G, dh), pos).reshape(B, T, 3, G, dh)
    kv = jnp.stack([keys[:, :, 0], kv[:, :, 1], keys[:, :, 1], kv[:, :, 3], keys[:, :, 2], kv[:, :, 5]], axis=2)
    new_rows = kv[:, :, :4]
    win = kv[:, :, 4:] if past is None else jnp.concatenate([wbuf, kv[:, :, 4:]], 1)
    if past is None:
        o = _nsa_prompt_attention(q.reshape(B, T, qd), gates, new_rows, win, cmp_pos, cmp_w)
    else:
        cache, page_table = past
        o = _nsa_sample_attention(q.reshape(B, T, qd), gates, new_rows, kv[:, :, 4:], cache, page_table, wbuf,
                                  cmp_pos, cmp_w)
    new_win = win[:, -min(WINDOW, win.shape[1]):]
    return _mm(o.reshape(B, T, D_MODEL), w_out), new_rows, new_win


DIFF_QB = 256
DIFF_TK = 512


def _online_softmax_tile(s, vt_tile, m_scr, l_scr, acc_scr):
    m_old = m_scr[...]
    m_new = jnp.maximum(m_old, jnp.max(s, axis=0, keepdims=True))
    m_safe = jnp.where(m_new == -jnp.inf, 0.0, m_new)
    alpha = jnp.exp(m_old - m_safe)
    p = jnp.exp(s - m_safe)
    l_scr[...] = alpha * l_scr[...] + jnp.sum(p, axis=0, keepdims=True)
    acc_scr[...] = alpha * acc_scr[...] + jnp.dot(vt_tile, p.astype(jnp.bfloat16), preferred_element_type=jnp.float32)
    m_scr[...] = m_new


def _diff_prompt_kernel(lam_ref, q_ref, k_ref, vt_ref, g_ref, o_ref, m_scr, l_scr, acc_scr, *, out_scale):
    f32, bf16 = jnp.float32, jnp.bfloat16
    i = pl.program_id(2)
    QB, TK, dd = DIFF_QB, DIFF_TK, DIFF_DH
    W = 2 * QB
    nt = (((1,), (1,)), ((), ()))
    qb = q_ref[0] * dd ** -0.5
    col = lax.broadcasted_iota(jnp.int32, (1, 2 * dd), 1)
    q = jnp.concatenate([jnp.where(col < dd, qb, 0.0), jnp.where(col >= dd, qb, 0.0)], axis=0).astype(bf16)
    lane = lax.broadcasted_iota(jnp.int32, (1, W), 1)
    qpos = i * QB + (lane & (QB - 1))
    m_scr[...] = jnp.full(m_scr.shape, -jnp.inf, f32)
    l_scr[...] = jnp.zeros(l_scr.shape, f32)
    acc_scr[...] = jnp.zeros(acc_scr.shape, f32)
    n_full = (i * QB) // TK

    def full_tile(kt, carry):
        k0 = pl.multiple_of(kt * TK, TK)
        s = lax.dot_general(k_ref[0, pl.ds(k0, TK), :], q, nt, preferred_element_type=f32)
        _online_softmax_tile(s, vt_ref[0, 0, :, pl.ds(k0, TK)], m_scr, l_scr, acc_scr)
        return carry

    lax.fori_loop(0, n_full, full_tile, 0)
    k0 = pl.multiple_of(n_full * TK, TK)
    s = lax.dot_general(k_ref[0, pl.ds(k0, TK), :], q, nt, preferred_element_type=f32)
    kpos = k0 + lax.broadcasted_iota(jnp.int32, (TK, 1), 0)
    _online_softmax_tile(jnp.where(kpos <= qpos, s, -jnp.inf), vt_ref[0, 0, :, pl.ds(k0, TK)], m_scr, l_scr, acc_scr)
    o = acc_scr[...] * (1.0 / jnp.maximum(l_scr[...], 1e-30))
    o = o[:, :QB] - lam_ref[0, 0] * o[:, QB:]
    o = o * lax.rsqrt(jnp.mean(o * o, axis=0, keepdims=True) + EPS) * (g_ref[...] * out_scale)
    o_ref[0] = o.T


def _diff_prompt_attention(q, k, v, lam, subln_g, out_scale):
    B, T, _ = q.shape
    H, dv = DIFF_HEADS, 2 * DIFF_DH
    assert T % DIFF_TK == 0 and DIFF_TK % DIFF_QB == 0
    bf16 = jnp.bfloat16
    vt = jnp.transpose(v.reshape(B, T, H, dv), (0, 2, 3, 1)).astype(bf16)
    W = 2 * DIFF_QB
    return pl.pallas_call(
        functools.partial(_diff_prompt_kernel, out_scale=out_scale),
        grid=(B, H, T // DIFF_QB),
        in_specs=[pl.BlockSpec(memory_space=pltpu.SMEM),
                  pl.BlockSpec((1, DIFF_QB, dv), lambda b, h, i: (b, i, h)),
                  pl.BlockSpec((1, T, dv), lambda b, h, i: (b, 0, h)),
                  pl.BlockSpec((1, 1, dv, T), lambda b, h, i: (b, h, 0, 0)),
                  pl.BlockSpec((dv, 1), lambda b, h, i: (0, 0))],
        out_specs=pl.BlockSpec((1, DIFF_QB, dv), lambda b, h, i: (b, i, h)),
        out_shape=jax.ShapeDtypeStruct(q.shape, jnp.float32),
        scratch_shapes=[pltpu.VMEM((1, W), jnp.float32), pltpu.VMEM((1, W), jnp.float32), pltpu.VMEM((dv, W), jnp.float32)],
        compiler_params=pltpu.CompilerParams(dimension_semantics=("parallel", "parallel", "arbitrary"),
                                             vmem_limit_bytes=VMEM_LIMIT),
        name="diff_prompt",
    )(lam.reshape(1, 1), q, k.astype(bf16), vt, subln_g.reshape(dv, 1))


def _diff_sample_kernel(pt_ref, lam_ref, *refs, n_pages, out_scale):
    pages = refs[:n_pages]
    q_ref, new_ref, g_ref, o_ref = refs[n_pages:]
    f32, bf16 = jnp.float32, jnp.bfloat16
    H, dd = DIFF_HEADS, DIFF_DH
    T = q_ref.shape[1]
    D = H * 2 * dd
    qt = jnp.concatenate([q_ref[0] * dd ** -0.5] * (LANE // T), axis=0)
    rowi = lax.broadcasted_iota(jnp.int32, (LANE, D), 0)
    coli = lax.broadcasted_iota(jnp.int32, (LANE, D), 1)
    qbd = jnp.where(coli // dd == rowi // T, qt, 0.0).astype(bf16)
    lane = lax.broadcasted_iota(jnp.int32, (1, LANE), 1)
    qidx = lane & (T - 1)
    s = [lax.dot_general(pg[0, :, 0:D].astype(bf16), qbd, _NT, preferred_element_type=f32) for pg in pages]
    new_row = lax.broadcasted_iota(jnp.int32, (LANE, 1), 0)
    pad = jnp.zeros((LANE - T, D), f32)
    k_new = jnp.concatenate([new_ref[0, :, 0:D], pad], axis=0).astype(bf16)
    v_new = jnp.concatenate([new_ref[0, :, D:2 * D], pad], axis=0).astype(bf16)
    s_n = lax.dot_general(k_new, qbd, _NT, preferred_element_type=f32)
    s_n = jnp.where((new_row < T) & (new_row <= qidx), s_n, -jnp.inf)
    m = jnp.max(s_n, axis=0, keepdims=True)
    for sp in s:
        m = jnp.maximum(m, jnp.max(sp, axis=0, keepdims=True))
    p_n = jnp.exp(s_n - m)
    l = jnp.sum(p_n, axis=0, keepdims=True)
    acc = lax.dot_general(v_new, p_n.astype(bf16), _TN, preferred_element_type=f32)
    for pg, sp in zip(pages, s):
        p = jnp.exp(sp - m)
        l = l + jnp.sum(p, axis=0, keepdims=True)
        acc = acc + lax.dot_general(pg[0, :, D:2 * D].astype(bf16), p.astype(bf16), _TN, preferred_element_type=f32)
    inv_l = 1.0 / jnp.maximum(l, 1e-30)
    lam = lam_ref[0, 0]
    for h in range(H):
        a = acc[h * 2 * dd:(h + 1) * 2 * dd, :] * inv_l
        d = a - lam * pltpu.roll(a, LANE - T, axis=1)
        if h:
            d = pltpu.roll(d, LANE - h * 2 * T, axis=1)
        d = d * lax.rsqrt(jnp.mean(d * d, axis=0, keepdims=True) + EPS) * (g_ref[...] * out_scale)
        o_ref[0, :, h * 2 * dd:(h + 1) * 2 * dd] = d.T[0:T]


def _diff_sample_attention(q, new_rows, cache, page_table, lam, subln_g, out_scale):
    B, T, D = q.shape
    n_pool, page = cache.shape[:2]
    n_pages = page_table.shape[1]
    assert T == 8 and 2 * DIFF_HEADS * T == LANE
    f32 = jnp.float32
    seq = lambda *shape: pl.BlockSpec((1,) + shape, lambda b, pt: (b,) + (0,) * len(shape))
    page_spec = lambda p: pl.BlockSpec((1, page, 2 * D), lambda b, pt: (pt[b, p], 0, 0))
    grid_spec = pltpu.PrefetchScalarGridSpec(
        num_scalar_prefetch=1,
        grid=(B,),
        in_specs=[pl.BlockSpec(memory_space=pltpu.SMEM)] + [page_spec(p) for p in range(n_pages)]
                 + [seq(T, D), seq(T, 2 * D), pl.BlockSpec((2 * DIFF_DH, 1), lambda b, pt: (0, 0))],
        out_specs=seq(T, D),
    )
    return pl.pallas_call(
        functools.partial(_diff_sample_kernel, n_pages=n_pages, out_scale=out_scale),
        grid_spec=grid_spec,
        out_shape=jax.ShapeDtypeStruct(q.shape, f32),
        compiler_params=pltpu.CompilerParams(dimension_semantics=("parallel",), vmem_limit_bytes=56 * 1024 * 1024),
        name="diff_sample",
    )(page_table, lam.reshape(1, 1), *([cache.reshape(n_pool, page, 2 * D)] * n_pages), q,
      new_rows.reshape(B, T, 2 * D), subln_g.reshape(2 * DIFF_DH, 1))


def diff_mixer(h, pos0, past, layer_idx, w_in, lq1, lk1, lq2, lk2, subln_g, w_out):
    B, T, _ = h.shape
    H, dd = DIFF_HEADS, DIFF_DH
    f32 = jnp.float32
    dt = h.dtype
    pos = pos0 + jnp.arange(T, dtype=jnp.int32)
    proj = _mm(h, w_in)
    q = rope(proj[..., :D_MODEL].reshape(B, T, 2 * H, dd), pos)
    k = rope(proj[..., D_MODEL:2 * D_MODEL].reshape(B, T, 2 * H, dd), pos)
    v = proj[..., 2 * D_MODEL:].reshape(B, T, 2 * H, dd)
    new_rows = jnp.stack([k, v], 2)
    lam_init = 0.8 - 0.6 * math.exp(-0.3 * layer_idx)
    lam = (jnp.exp(jnp.sum(lq1.astype(f32) * lk1.astype(f32))) - jnp.exp(jnp.sum(lq2.astype(f32) * lk2.astype(f32))) + lam_init)
    if past is None:
        o = _diff_prompt_attention(q.reshape(B, T, D_MODEL), k.reshape(B, T, D_MODEL), v.reshape(B, T, D_MODEL),
                                   lam, subln_g, 1.0 - lam_init)
        return _mm(o, w_out), new_rows
    cache, page_table = past
    o = _diff_sample_attention(q.reshape(B, T, D_MODEL), new_rows, cache, page_table, lam, subln_g, 1.0 - lam_init)
    return _mm(o, w_out), new_rows


PEER_TB = 512
PEER_ET = 1024
_PEER_CAND = [(a, b) for a in range(PEER_TOPK) for b in range(PEER_TOPK) if (a + 1) * (b + 1) <= PEER_TOPK]


def _extract_top(s, n_iter, on_pick):
    rows = s.shape[0]
    iota = lax.broadcasted_iota(jnp.int32, s.shape, 0)
    for r in range(n_iter):
        m = jnp.max(s, axis=0, keepdims=True)
        idx = jnp.min(jnp.where(s == m, iota, rows), axis=0, keepdims=True)
        sel = iota == idx
        on_pick(r, m, sel)
        s = jnp.where(sel, -jnp.inf, s)


def _peer_route_kernel(x_ref, shift_ref, scale_ref, g_ref, wq_ref, k1_ref, k2_ref, aof_ref,
                       h_ref, c1_ref, cnt1_ref, rank2_ref, e2_ref, q_scr, v1_scr, v2_scr, cand_scr):
    f32 = jnp.float32
    x = x_ref[...]
    tb = x.shape[0] * x.shape[1]
    y = x * lax.rsqrt(jnp.mean(x * x, -1, keepdims=True) + EPS) * g_ref[...]
    h = (y * (1.0 + scale_ref[...]) + shift_ref[...]).reshape(tb, D_MODEL)
    hb = h.astype(jnp.bfloat16)
    h_ref[...] = hb
    q = jnp.dot(hb, wq_ref[...], preferred_element_type=f32).astype(jnp.bfloat16)
    for hh in range(PEER_HEADS):
        q_scr[hh] = q[:, hh * PEER_QDIM:(hh + 1) * PEER_QDIM]
    nt = (((1,), (1,)), ((), ()))
    n_chunk = tb // LANE
    cand_scr[...] = jnp.full(cand_scr.shape, -jnp.inf, f32)

    def body(it, carry):
        hh = it // n_chunk
        c0 = pl.multiple_of((it % n_chunk) * LANE, LANE)
        qc = q_scr[hh, pl.ds(c0, LANE), :]
        s1 = lax.dot_general(k1_ref[hh], qc, nt, preferred_element_type=f32)
        s2 = lax.dot_general(k2_ref[hh], qc, nt, preferred_element_type=f32)
        ranks = []
        for s, v_scr in ((s1, v1_scr), (s2, v2_scr)):
            rank = [jnp.full(s.shape, float(N_KEYS), f32)]

            def pick(r, m, sel, v_scr=v_scr, rank=rank):
                v_scr[r:r + 1, :] = m
                rank[0] = jnp.where(sel, float(r), rank[0])

            _extract_top(s, PEER_TOPK, pick)
            ranks.append(rank[0])
        for k, (a, b) in enumerate(_PEER_CAND):
            cand_scr[k:k + 1, :] = v1_scr[a:a + 1, :] + v2_scr[b:b + 1, :]
        top1 = v1_scr[0:1, :]
        top2 = v2_scr[0:1, :]
        top_val = top1 + top2
        a_of = aof_ref[...]
        iota16 = lax.broadcasted_iota(jnp.int32, (PEER_TOPK, LANE), 0)
        st = {"z": jnp.zeros((1, LANE), f32), "cnt": jnp.zeros((PEER_TOPK, LANE), f32)}

        def pick_c(r, m, sel, st=st):
            st["z"] = st["z"] + jnp.exp(m - top_val)
            a_sel = jnp.max(jnp.where(sel, a_of, 0), axis=0, keepdims=True)
            st["cnt"] = st["cnt"] + (iota16 == a_sel).astype(f32)

        _extract_top(cand_scr[...], PEER_TOPK, pick_c)
        inv_z = 1.0 / st["z"]
        cnt1 = jnp.zeros(s1.shape, f32)
        for a in range(PEER_TOPK):
            cnt1 = jnp.where(ranks[0] == float(a), st["cnt"][a:a + 1], cnt1)
        c1_ref[hh, :, pl.ds(c0, LANE)] = jnp.exp(s1 - top1) * inv_z
        cnt1_ref[hh, :, pl.ds(c0, LANE)] = cnt1
        rank2_ref[hh, :, pl.ds(c0, LANE)] = ranks[1]
        e2_ref[hh, :, pl.ds(c0, LANE)] = jnp.exp(s2 - top2)
        return carry

    lax.fori_loop(0, PEER_HEADS * n_chunk, body, 0)


def _peer_dense_kernel(hb_ref, u_ref, vt_ref, c1_ref, cnt1_ref, rank2_ref, e2_ref, x_ref, gm_ref,
                       o_ref, hta_ref, htb_ref, gha_ref, ghb_ref, acc_ref):
    f32 = jnp.float32
    j = pl.program_id(1)
    tb = hb_ref.shape[0]

    @pl.when(j == 0)
    def _():
        acc_ref[...] = jnp.zeros_like(acc_ref)

    bf16 = jnp.bfloat16
    n_rows = PEER_ET // N_KEYS
    sub = 32

    def gates_times_act(ht_half, gh_half, t0):
        for c0 in range(0, half, LANE):
            lanes = slice(t0 + c0, t0 + c0 + LANE)
            for s0 in range(0, N_KEYS, sub):
                g = [jnp.zeros((sub, LANE), f32) for _ in range(n_rows)]
                for hh in range(PEER_HEADS):
                    rk = rank2_ref[hh, s0:s0 + sub, lanes]
                    ev = e2_ref[hh, s0:s0 + sub, lanes]
                    for r in range(n_rows):
                        g[r] = g[r] + jnp.where(rk < cnt1_ref[hh, r:r + 1, lanes], ev * c1_ref[hh, r:r + 1, lanes], 0.0)
                for r in range(n_rows):
                    rows = slice(r * N_KEYS + s0, r * N_KEYS + s0 + sub)
                    pre = ht_half[rows, c0:c0 + LANE]
                    act = 0.5 * pre * (1.0 + lax.erf(pre * (2.0 ** -0.5)))
                    gh_half[rows, c0:c0 + LANE] = (g[r] * act).astype(bf16)

    half = tb // 2
    nt = (((1,), (1,)), ((), ()))
    hta_ref[...] = lax.dot_general(u_ref[...], hb_ref[0:half, :], nt, preferred_element_type=f32)
    htb_ref[...] = lax.dot_general(u_ref[...], hb_ref[half:tb, :], nt, preferred_element_type=f32)
    gates_times_act(hta_ref, gha_ref, 0)
    acc_ref[:, 0:half] += jnp.dot(vt_ref[...], gha_ref[...], preferred_element_type=f32)
    gates_times_act(htb_ref, ghb_ref, half)
    acc_ref[:, half:tb] += jnp.dot(vt_ref[...], ghb_ref[...], preferred_element_type=f32)

    @pl.when(j == pl.num_programs(1) - 1)
    def _():
        upd = acc_ref[...].T.reshape(x_ref.shape)
        o_ref[...] = x_ref[...] + gm_ref[...] * upd


def _peer_sublayer(x, shift, scale, gate, norm_g, wq_b, k1p, k2p, u_b, vt_b):
    B, T, D = x.shape
    n = B * T
    tb = PEER_TB
    assert n % tb == 0
    if T % tb == 0:
        nbs, tper, per = 1, tb, T // tb
        xmap = lambda i, *_: (i // per, i % per, 0)
        mmap = lambda i, *_: (i // per, 0, 0)
    else:
        assert tb % T == 0 and T % 8 == 0
        nbs, tper = tb // T, T
        xmap = lambda i, *_: (i, 0, 0)
        mmap = lambda i, *_: (i, 0, 0)
    nblk = n // tb
    f32 = jnp.float32
    x_spec = pl.BlockSpec((nbs, tper, D), xmap)
    m_spec = pl.BlockSpec((nbs, 1, D), mmap)
    n_cand_pad = -(-len(_PEER_CAND) // 8) * 8
    a_of = jnp.asarray(np.broadcast_to(np.array([a for a, _ in _PEER_CAND] + [0] * (n_cand_pad - len(_PEER_CAND)),
                                                np.int32)[:, None], (n_cand_pad, LANE)))
    route_shape = lambda dt: jax.ShapeDtypeStruct((PEER_HEADS, N_KEYS, n), dt)
    route_spec = pl.BlockSpec((PEER_HEADS, N_KEYS, tb), lambda i: (0, 0, i))
    hb, c1, cnt1, rank2, e2 = pl.pallas_call(
        _peer_route_kernel,
        grid=(nblk,),
        in_specs=[x_spec, m_spec, m_spec,
                  pl.BlockSpec((1, D), lambda i: (0, 0)),
                  pl.BlockSpec((D, PEER_HEADS * PEER_QDIM), lambda i: (0, 0)),
                  pl.BlockSpec((PEER_HEADS, N_KEYS, PEER_QDIM), lambda i: (0, 0, 0)),
                  pl.BlockSpec((PEER_HEADS, N_KEYS, PEER_QDIM), lambda i: (0, 0, 0)),
                  pl.BlockSpec((n_cand_pad, LANE), lambda i: (0, 0))],
        out_specs=[pl.BlockSpec((tb, D), lambda i: (i, 0)), route_spec, route_spec, route_spec, route_spec],
        out_shape=[jax.ShapeDtypeStruct((n, D), jnp.bfloat16)] + [route_shape(f32)] * 4,
        scratch_shapes=[pltpu.VMEM((PEER_HEADS, tb, PEER_QDIM), jnp.bfloat16), pltpu.VMEM((PEER_TOPK, LANE), f32),
                        pltpu.VMEM((PEER_TOPK, LANE), f32), pltpu.VMEM((n_cand_pad, LANE), f32)],
        compiler_params=pltpu.CompilerParams(dimension_semantics=("parallel",), vmem_limit_bytes=VMEM_LIMIT),
        name="peer_route",
    )(x, shift, scale, norm_g.reshape(1, D), wq_b, k1p, k2p, a_of)

    rows = PEER_ET // N_KEYS
    sub_spec = pl.BlockSpec((PEER_HEADS, rows, tb), lambda i, j: (0, j, i))
    full_spec = pl.BlockSpec((PEER_HEADS, N_KEYS, tb), lambda i, j: (0, 0, i))
    return pl.pallas_call(
        _peer_dense_kernel,
        grid=(nblk, N_EXPERTS // PEER_ET),
        in_specs=[pl.BlockSpec((tb, D), lambda i, j: (i, 0)),
                  pl.BlockSpec((PEER_ET, D), lambda i, j: (j, 0)),
                  pl.BlockSpec((D, PEER_ET), lambda i, j: (0, j)),
                  sub_spec, sub_spec, full_spec, full_spec,
                  pl.BlockSpec((nbs, tper, D), lambda i, j: xmap(i)),
                  pl.BlockSpec((nbs, 1, D), lambda i, j: mmap(i))],
        out_specs=pl.BlockSpec((nbs, tper, D), lambda i, j: xmap(i)),
        out_shape=jax.ShapeDtypeStruct(x.shape, x.dtype),
        scratch_shapes=[pltpu.VMEM((PEER_ET, tb // 2), f32), pltpu.VMEM((PEER_ET, tb // 2), f32),
                        pltpu.VMEM((PEER_ET, tb // 2), jnp.bfloat16), pltpu.VMEM((PEER_ET, tb // 2), jnp.bfloat16),
                        pltpu.VMEM((D, tb), f32)],
        compiler_params=pltpu.CompilerParams(dimension_semantics=("parallel", "arbitrary"), vmem_limit_bytes=VMEM_LIMIT),
        name="peer_dense",
    )(hb, u_b, vt_b, c1, cnt1, rank2, e2, x, gate)


def _peer_weights(w_q, k1, k2, u_tab, v_tab):
    bf16 = jnp.bfloat16
    half = PEER_QDIM // 2
    k1p = jnp.pad(k1, ((0, 0), (0, 0), (0, half))).astype(bf16)
    k2p = jnp.pad(k2, ((0, 0), (0, 0), (half, 0))).astype(bf16)
    return w_q.astype(bf16), k1p, k2p, u_tab.astype(bf16), v_tab.astype(bf16).T


def kernel(x_prompt, x_sample, cache_nsa_kv, cache_diff_kv, state_nsa_window, state_gdn_S, state_gdn_conv,
           page_table, c_prompt, c_sample, ada_w, ada_b, norm_mix_g, norm_ffn_g, final_norm_g,
           gdn_w_in, gdn_conv_w, gdn_a_log, gdn_dt_bias, gdn_norm_g, gdn_w_out,
           nsa_w_in, nsa_cmp_pos, nsa_cmp_w, nsa_w_out,
           diff_w_in, diff_lq1, diff_lk1, diff_lq2, diff_lk2, diff_subln_g, diff_w_out,
           peer_w_q, peer_k1, peer_k2, peer_u, peer_v):
    past_len = page_table.shape[1] * cache_nsa_kv.shape[2]
    peer_w = [_peer_weights(peer_w_q[i], peer_k1[i], peer_k2[i], peer_u[i], peer_v[i]) for i in range(DEPTH)]

    def trunk(x, c, sample):
        B, T, _ = x.shape
        pos0 = past_len if sample else 0
        cs = jax.nn.silu(c)
        new_S, new_conv, new_nsa_kv, new_nsa_win, new_diff_kv = [], [], [], [], []
        for i in range(DEPTH):
            mod = (cs @ ada_w[i] + ada_b[i]).reshape(B, 6, 1, D_MODEL)
            h = rmsnorm(x, norm_mix_g[i]) * (1.0 + mod[:, 1]) + mod[:, 0]
            j = i // N_MIXERS
            kind = i % N_MIXERS
            if kind == 0:
                if sample:
                    S0 = state_gdn_S[j].astype(jnp.float32)
                    buf = state_gdn_conv[j]
                else:
                    S0 = jnp.zeros((B, GDN_HEADS, GDN_DK, GDN_DV), jnp.float32)
                    buf = jnp.zeros((B, CONV_W - 1, GDN_HEADS * (2 * GDN_DK + GDN_DV)), x.dtype)
                m, buf_n, S_n = gdn_mixer(h, buf, S0, gdn_w_in[j], gdn_conv_w[j], gdn_a_log[j], gdn_dt_bias[j], gdn_norm_g[j], gdn_w_out[j])
                new_S.append(S_n.astype(x.dtype))
                new_conv.append(buf_n)
            elif kind == 1:
                past = (cache_nsa_kv[j], page_table) if sample else None
                wbuf = state_nsa_window[j] if sample else None
                m, kv_n, win_n = nsa_mixer(h, pos0, past, wbuf, nsa_w_in[j], nsa_cmp_pos[j], nsa_cmp_w[j], nsa_w_out[j])
                new_nsa_kv.append(kv_n)
                new_nsa_win.append(win_n)
            else:
                past = (cache_diff_kv[j], page_table) if sample else None
                m, kv_n = diff_mixer(h, pos0, past, i, diff_w_in[j], diff_lq1[j], diff_lk1[j], diff_lq2[j], diff_lk2[j], diff_subln_g[j], diff_w_out[j])
                new_diff_kv.append(kv_n)
            x = x + mod[:, 2] * m
            x = _peer_sublayer(x, mod[:, 3], mod[:, 4], mod[:, 5], norm_ffn_g[i], *peer_w[i])
        y = rmsnorm(x, final_norm_g)
        return y, jnp.stack(new_S), jnp.stack(new_conv), jnp.stack(new_nsa_kv), jnp.stack(new_nsa_win), jnp.stack(new_diff_kv)

    y_prompt, p_gdn_S, p_gdn_conv, p_nsa_kv, p_nsa_win, p_diff_kv = trunk(x_prompt, c_prompt, False)
    y_sample, s_gdn_S, s_gdn_conv, s_nsa_kv, s_nsa_win, s_diff_kv = trunk(x_sample, c_sample, True)
    return (y_prompt, y_sample, p_gdn_S, p_gdn_conv, p_nsa_kv, p_nsa_win, p_diff_kv, s_gdn_S, s_gdn_conv, s_nsa_kv, s_nsa_win, s_diff_kv)
```

```python
import functools
import math

import jax
import jax.numpy as jnp
import numpy as np
from jax import lax
from jax.experimental import pallas as pl
from jax.experimental.pallas import tpu as pltpu

D_MODEL = 1024
DEPTH = 4
N_MIXERS = 3
HEAD_DIM = 128
GDN_HEADS = D_MODEL // HEAD_DIM
GDN_DK = HEAD_DIM
GDN_DV = HEAD_DIM
CONV_W = 4
GDN_CHUNK = 64
NSA_HEADS = D_MODEL // HEAD_DIM
NSA_KV_GROUPS = 2
CMP_BLOCK = 32
SEL_BLOCK = 64
TOP_N = 16
WINDOW = 512
Q_BLOCK = 128
DIFF_HEADS = D_MODEL // HEAD_DIM
DIFF_DH = D_MODEL // DIFF_HEADS // 2
PEER_HEADS = 8
N_KEYS = 128
N_EXPERTS = N_KEYS * N_KEYS
PEER_TOPK = 16
PEER_QDIM = 128
PEER_TOKEN_BLOCK = 256
ROPE_THETA = 10000.0
EPS = 1e-6

LANE = 128
VMEM_LIMIT = 48 * 1024 * 1024


def _mm_kernel(x_ref, w_ref, o_ref):
    o_ref[...] = jnp.dot(x_ref[...].astype(jnp.bfloat16), w_ref[...], preferred_element_type=jnp.float32)


def _mm(x, w, tm=256):
    lead = x.shape[:-1]
    K = x.shape[-1]
    N = w.shape[1]
    x2 = x.reshape(-1, K)
    M = x2.shape[0]
    n_pad = -(-N // 256) * 256
    wb = jnp.pad(w, ((0, 0), (0, n_pad - N))).astype(jnp.bfloat16)
    tm = min(tm, M)
    assert M % tm == 0
    out = pl.pallas_call(
        _mm_kernel,
        grid=(M // tm,),
        in_specs=[pl.BlockSpec((tm, K), lambda i: (i, 0)), pl.BlockSpec((K, n_pad), lambda i: (0, 0))],
        out_specs=pl.BlockSpec((tm, n_pad), lambda i: (i, 0)),
        out_shape=jax.ShapeDtypeStruct((M, n_pad), jnp.float32),
        compiler_params=pltpu.CompilerParams(dimension_semantics=("parallel",), vmem_limit_bytes=VMEM_LIMIT),
        name="proj_matmul",
    )(x2, wb)
    return out[:, :N].reshape(*lead, N)


def rmsnorm(x, g):
    xf = x.astype(jnp.float32)
    y = xf * lax.rsqrt(jnp.mean(xf * xf, -1, keepdims=True) + EPS)
    return (y * g.astype(jnp.float32)).astype(x.dtype)


def l2norm(x):
    xf = x.astype(jnp.float32)
    return (xf * lax.rsqrt(jnp.sum(xf * xf, -1, keepdims=True) + EPS)).astype(x.dtype)


def rope(x, pos):
    half = x.shape[-1] // 2
    inv = ROPE_THETA ** (-jnp.arange(half, dtype=jnp.float32) / half)
    ang = pos.astype(jnp.float32)[:, None] * inv[None, :]
    cos = jnp.cos(ang)[:, None, :]
    sin = jnp.sin(ang)[:, None, :]
    xf = x.astype(jnp.float32)
    x1, x2 = xf[..., :half], xf[..., half:]
    return jnp.concatenate([x1 * cos - x2 * sin, x2 * cos + x1 * sin], -1).astype(x.dtype)


def masked_softmax(s, mask):
    s = jnp.where(mask, s.astype(jnp.float32), -jnp.inf)
    m = jnp.max(s, -1, keepdims=True)
    m = jnp.where(jnp.isfinite(m), m, 0.0)
    p = jnp.exp(s - m)
    return p / jnp.maximum(jnp.sum(p, -1, keepdims=True), 1e-30)


def gather_pages(pool, page_table):
    g = pool[page_table]
    return g.reshape(g.shape[0], g.shape[1] * g.shape[2], *pool.shape[2:])


def causal_conv(x, buf, w):
    T = x.shape[1]
    xx = jnp.concatenate([buf, x], 1)
    y = xx[:, 0:T] * w[0]
    for j in range(1, CONV_W):
        y = y + xx[:, j:j + T] * w[j]
    return jax.nn.silu(y), xx[:, -(CONV_W - 1):]


def gated_delta_rule(q, k, v, g, beta, S0):
    f32 = jnp.float32
    B, T, H, dk = q.shape
    dv = v.shape[-1]
    C = GDN_CHUNK
    Tp = -(-T // C) * C

    def prep(a):
        a = a.astype(f32)
        a = jnp.pad(a, [(0, 0), (0, Tp - T)] + [(0, 0)] * (a.ndim - 2))
        a = a.reshape(B, Tp // C, C, *a.shape[2:])
        return jnp.moveaxis(a, (1, 3), (0, 2))

    qc = prep(q) * dk ** -0.5
    kc, vc, gc, bc = prep(k), prep(v), prep(g), prep(beta)
    gcum = jnp.cumsum(gc, -1)
    idx = jnp.arange(C)
    incl = idx[:, None] >= idx[None, :]
    strict = idx[:, None] > idx[None, :]
    decay = jnp.exp(jnp.where(incl, gcum[..., :, None] - gcum[..., None, :], -jnp.inf))
    kb = kc * bc[..., None]
    L = jnp.where(strict, jnp.einsum('nbhid,nbhjd->nbhij', kb, kc) * decay, 0.0)
    eye = jnp.eye(C, dtype=f32)
    A = eye + L
    Tinv = lax.linalg.triangular_solve(A, jnp.broadcast_to(eye, A.shape), left_side=True, lower=True, unit_diagonal=True)
    u = Tinv @ (vc * bc[..., None])
    w = Tinv @ (kb * jnp.exp(gcum)[..., None])
    a_intra = jnp.einsum('nbhid,nbhjd->nbhij', qc, kc) * decay

    def step(S, xs):
        q_i, k_i, u_i, w_i, a_i, g_i = xs
        v_new = u_i - w_i @ S
        o = (q_i * jnp.exp(g_i)[..., None]) @ S + a_i @ v_new
        g_last = g_i[..., -1:]
        S = S * jnp.exp(g_last)[..., None] + jnp.einsum('bhck,bhcv->bhkv', k_i * jnp.exp(g_last - g_i)[..., None], v_new)
        return S, o

    S, o = lax.scan(step, S0.astype(f32), (qc, kc, u, w, a_intra, gcum))
    o = jnp.moveaxis(o, (0, 2), (1, 3)).reshape(B, Tp, H, dv)[:, :T]
    return o, S


def gdn_mixer(h, conv_buf, S0, w_in, conv_w, a_log, dt_bias, norm_g, w_out):
    B, T, _ = h.shape
    H, dk, dv = GDN_HEADS, GDN_DK, GDN_DV
    n_qkv = H * (2 * dk + dv)
    f32 = jnp.float32
    proj = _mm(h, w_in)
    qkv, conv_new = causal_conv(proj[..., :n_qkv], conv_buf, conv_w)
    z = proj[..., n_qkv:n_qkv + H * dv].reshape(B, T, H, dv)
    b_raw = proj[..., n_qkv + H * dv:n_qkv + H * dv + H]
    a_raw = proj[..., n_qkv + H * dv + H:]
    q = l2norm(qkv[..., :H * dk].reshape(B, T, H, dk))
    k = l2norm(qkv[..., H * dk:2 * H * dk].reshape(B, T, H, dk))
    v = qkv[..., 2 * H * dk:].reshape(B, T, H, dv)
    beta = jax.nn.sigmoid(b_raw.astype(f32))
    g = -jnp.exp(a_log.astype(f32)) * jax.nn.softplus(a_raw.astype(f32) + dt_bias.astype(f32))
    o, S = _gdn_delta_rule(q.reshape(B, T, H * dk), k.reshape(B, T, H * dk), v.reshape(B, T, H * dv),
                           z.reshape(B, T, H * dv), g, beta, S0, norm_g)
    return _mm(o, w_out), conv_new, S


def _dot_split3(a, b):
    f32, bf16 = jnp.float32, jnp.bfloat16
    ah = a.astype(bf16)
    al = (a - ah.astype(f32)).astype(bf16)
    bh = b.astype(bf16)
    bl = (b - bh.astype(f32)).astype(bf16)
    dot = functools.partial(jnp.dot, preferred_element_type=f32)
    return dot(ah, bh) + dot(ah, bl) + dot(al, bh)


def _gdn_chunk_kernel(q_ref, k_ref, v_ref, z_ref, col_ref, row_ref, s0_ref, ng_ref, o_ref, sout_ref, s_scr):
    f32, bf16 = jnp.float32, jnp.bfloat16
    n = pl.program_id(1)
    C = q_ref.shape[1]
    H, dk, dv = GDN_HEADS, GDN_DK, GDN_DV
    dot = functools.partial(jnp.dot, preferred_element_type=f32)
    nt = (((1,), (1,)), ((), ()))

    @pl.when(n == 0)
    def _():
        s_scr[...] = s0_ref[0]

    ii = lax.broadcasted_iota(jnp.int32, (C, C), 0)
    jj = lax.broadcasted_iota(jnp.int32, (C, C), 1)
    incl = ii >= jj
    strict = ii > jj
    eye = (ii == jj).astype(f32)
    heads = range(H)
    ks = [slice(h * dk, (h + 1) * dk) for h in heads]
    vs = [slice(h * dv, (h + 1) * dv) for h in heads]
    gc = [col_ref[0, :, h:h + 1] for h in heads]
    bc = [col_ref[0, :, H + h:H + h + 1] for h in heads]
    gr = [row_ref[0, 0, h:h + 1, :] for h in heads]
    decay = [jnp.exp(jnp.where(incl, gc[h] - gr[h], -jnp.inf)) for h in heads]
    kcb = [k_ref[0, :, ks[h]].astype(bf16) for h in heads]
    kb = [k_ref[0, :, ks[h]] * bc[h] for h in heads]
    mpow = [-jnp.where(strict, lax.dot_general(kb[h].astype(bf16), kcb[h], nt, preferred_element_type=f32) * decay[h], 0.0)
            for h in heads]
    tinv = [eye + mpow[h] for h in heads]
    for _ in range(C.bit_length() - 2):
        mpow = [_dot_split3(mpow[h], mpow[h]) for h in heads]
        tinv = [tinv[h] + _dot_split3(tinv[h], mpow[h]) for h in heads]
    eg = [jnp.exp(gc[h]) for h in heads]
    uw = [dot(tinv[h].astype(bf16), jnp.concatenate([v_ref[0, :, vs[h]] * bc[h], kb[h] * eg[h]], axis=1).astype(bf16))
          for h in heads]
    qh = [q_ref[0, :, ks[h]] * dk ** -0.5 for h in heads]
    a_intra = [(lax.dot_general(qh[h].astype(bf16), kcb[h], nt, preferred_element_type=f32) * decay[h]).astype(bf16)
               for h in heads]
    s_b = [s_scr[h].astype(bf16) for h in heads]
    v_nb = [(uw[h][:, :dv] - dot(uw[h][:, dv:].astype(bf16), s_b[h])).astype(bf16) for h in heads]
    o = [dot((qh[h] * eg[h]).astype(bf16), s_b[h]) + dot(a_intra[h], v_nb[h]) for h in heads]
    for h in heads:
        g_last = gr[h][:, C - 1:C]
        kd = (k_ref[0, :, ks[h]] * jnp.exp(g_last - gc[h])).astype(bf16)
        s_scr[h] = s_scr[h] * jnp.exp(g_last) + lax.dot_general(kd, v_nb[h], _TN, preferred_element_type=f32)
    for h in heads:
        y = o[h] * lax.rsqrt(jnp.mean(o[h] * o[h], -1, keepdims=True) + EPS) * ng_ref[...]
        zz = z_ref[0, :, vs[h]]
        o_ref[0, :, vs[h]] = y * (zz * jax.nn.sigmoid(zz))

    @pl.when(n == pl.num_programs(1) - 1)
    def _():
        sout_ref[0] = s_scr[...]


def _gdn_delta_rule(q, k, v, z, g, beta, S0, norm_g):
    B, T, _ = q.shape
    H, dk, dv = GDN_HEADS, GDN_DK, GDN_DV
    C = GDN_CHUNK if T % GDN_CHUNK == 0 else T
    assert C & (C - 1) == 0 and C % 8 == 0
    nC = T // C
    f32 = jnp.float32
    gcum = jnp.cumsum(g.astype(f32).reshape(B, nC, C, H), axis=2)
    col = jnp.concatenate([gcum.reshape(B, T, H), beta.astype(f32)], axis=-1)
    row = jnp.swapaxes(gcum, 2, 3)
    tok = lambda w: pl.BlockSpec((1, C, w), lambda b, n: (b, n, 0))
    o, s_out = pl.pallas_call(
        _gdn_chunk_kernel,
        grid=(B, nC),
        in_specs=[tok(H * dk), tok(H * dk), tok(H * dv), tok(H * dv), tok(2 * H),
                  pl.BlockSpec((1, 1, H, C), lambda b, n: (b, n, 0, 0)),
                  pl.BlockSpec((1, H, dk, dv), lambda b, n: (b, 0, 0, 0)),
                  pl.BlockSpec((1, dv), lambda b, n: (0, 0))],
        out_specs=[tok(H * dv), pl.BlockSpec((1, H, dk, dv), lambda b, n: (b, 0, 0, 0))],
        out_shape=[jax.ShapeDtypeStruct((B, T, H * dv), f32), jax.ShapeDtypeStruct((B, H, dk, dv), f32)],
        scratch_shapes=[pltpu.VMEM((H, dk, dv), f32)],
        compiler_params=pltpu.CompilerParams(dimension_semantics=("parallel", "arbitrary"), vmem_limit_bytes=VMEM_LIMIT),
        name="gdn_delta_rule",
    )(q, k, v, z, col, row, S0.astype(f32), norm_g.reshape(1, dv))
    return o, s_out


def nsa_seq(q_s, g_s, full_s, win_s, pos0, wpos0, cmp_pos, cmp_w):
    dt = q_s.dtype
    T = q_s.shape[0]
    G, R, dh = NSA_KV_GROUPS, NSA_HEADS // NSA_KV_GROUPS, HEAD_DIM
    Tk = full_s.shape[0]
    Tkp = -(-Tk // SEL_BLOCK) * SEL_BLOCK
    full_p = jnp.pad(full_s, ((0, Tkp - Tk), (0, 0), (0, 0), (0, 0)))
    n_cmp = Tkp // CMP_BLOCK
    n_sel = Tkp // SEL_BLOCK
    blocks = full_p[:, :2].reshape(n_cmp, CMP_BLOCK, 2, G, dh) + jnp.transpose(cmp_pos, (1, 0, 2))[:, :, None, :]
    kv_cmp = jnp.einsum('nlsgd,slde->nsge', blocks, cmp_w)
    k_cmp, v_cmp = kv_cmp[:, 0], kv_cmp[:, 1]
    cmp_end = (jnp.arange(n_cmp) + 1) * CMP_BLOCK - 1
    sel = jnp.transpose(full_p[:, 2:].reshape(n_sel, SEL_BLOCK, 2, G, dh), (2, 3, 0, 1, 4))
    k_selb, v_selb = sel[0], sel[1]
    n_top = min(TOP_N, n_sel)
    win_p = jnp.pad(win_s, ((WINDOW, 0), (0, 0), (0, 0), (0, 0)))
    qb = Q_BLOCK if T % Q_BLOCK == 0 else T
    nqb = T // qb
    lw = WINDOW + qb - 1
    scale = HEAD_DIM ** -0.5
    g_idx = jnp.arange(G)[None, :, None]
    blk_ids = jnp.arange(n_sel)

    def block(args):
        qi, gi, bi = args
        qstart = pos0 + bi * qb
        qpos = qstart + jnp.arange(qb)
        qg = qi.reshape(qb, G, R, dh) * scale
        s_c = jnp.einsum('qgrd,ngd->qgrn', qg, k_cmp)
        p_c = masked_softmax(s_c, (cmp_end[None, :] <= qpos[:, None])[:, None, None, :])
        o_c = jnp.einsum('qgrn,ngd->qgrd', p_c.astype(dt), v_cmp)
        imp = p_c.sum(2).reshape(qb, G, n_sel, SEL_BLOCK // CMP_BLOCK).sum(-1)
        cur = (qpos // SEL_BLOCK)[:, None] == blk_ids[None, :]
        causal_blk = (blk_ids * SEL_BLOCK)[None, :] <= qpos[:, None]
        imp = jnp.where(cur[:, None], jnp.inf, jnp.where(causal_blk[:, None], imp, -jnp.inf))
        top_v, top_i = lax.top_k(imp, n_top)
        k_sel = k_selb[g_idx, top_i]
        v_sel = v_selb[g_idx, top_i]
        kpos = top_i[..., None] * SEL_BLOCK + jnp.arange(SEL_BLOCK)
        m_s = (kpos <= qpos[:, None, None, None]) & (top_v > -jnp.inf)[..., None]
        s_s = jnp.einsum('qgrd,qgksd->qgrks', qg, k_sel).reshape(qb, G, R, n_top * SEL_BLOCK)
        p_s = masked_softmax(s_s, m_s.reshape(qb, G, 1, n_top * SEL_BLOCK))
        o_s = jnp.einsum('qgrm,qgmd->qgrd', p_s.astype(dt), v_sel.reshape(qb, G, n_top * SEL_BLOCK, dh))
        wblk = lax.dynamic_slice_in_dim(win_p, qstart - wpos0 + 1, lw, axis=0)
        wpos = qstart - WINDOW + 1 + jnp.arange(lw)
        m_w = (wpos[None] <= qpos[:, None]) & (wpos[None] > qpos[:, None] - WINDOW) & (wpos[None] >= wpos0)
        s_w = jnp.einsum('qgrd,kgd->qgrk', qg, wblk[:, 0])
        p_w = masked_softmax(s_w, m_w[:, None, None, :])
        o_w = jnp.einsum('qgrk,kgd->qgrd', p_w.astype(dt), wblk[:, 1])
        gg = gi.reshape(qb, G, R, 3)
        o = gg[..., 0:1] * o_c + gg[..., 1:2] * o_s + gg[..., 2:3] * o_w
        return o.reshape(qb, NSA_HEADS, dh)

    out = lax.map(block, (q_s.reshape(nqb, qb, NSA_HEADS, dh), g_s.reshape(nqb, qb, NSA_HEADS, 3), jnp.arange(nqb)))
    return out.reshape(T, NSA_HEADS, dh)


NSA_R = NSA_HEADS // NSA_KV_GROUPS
NSA_TK = 512
NSA_WSPAN = WINDOW + Q_BLOCK


def _softmax_rows(s, mask):
    s = jnp.where(mask, s, -jnp.inf)
    m = jnp.max(s, axis=0, keepdims=True)
    m = jnp.where(m == -jnp.inf, 0.0, m)
    p = jnp.exp(s - m)
    return p * (1.0 / jnp.maximum(jnp.sum(p, axis=0, keepdims=True), 1e-30))


def _nsa_cmp_kernel(x_ref, pos_ref, w_ref, o_ref):
    xb = (x_ref[0, 0, 0, 0] + pos_ref[0]).astype(jnp.bfloat16)
    o_ref[0, 0, 0, 0] = jnp.dot(xb, w_ref[0], preferred_element_type=jnp.float32)


def _nsa_compress(rows_cmp, cmp_pos, cmp_w):
    B, T, _, G, dh = rows_cmp.shape
    n_sel = T // SEL_BLOCK
    ld = CMP_BLOCK * dh
    x = rows_cmp.reshape(B, n_sel, 2, CMP_BLOCK, 2, G, dh)
    x = jnp.transpose(x, (0, 4, 5, 2, 1, 3, 6)).reshape(B, 2, G, 2, n_sel, ld)
    return pl.pallas_call(
        _nsa_cmp_kernel,
        grid=(B, 2, G, 2),
        in_specs=[pl.BlockSpec((1, 1, 1, 1, n_sel, ld), lambda b, s, g, p: (b, s, g, p, 0, 0)),
                  pl.BlockSpec((1, 1, ld), lambda b, s, g, p: (s, 0, 0)),
                  pl.BlockSpec((1, ld, dh), lambda b, s, g, p: (s, 0, 0))],
        out_specs=pl.BlockSpec((1, 1, 1, 1, n_sel, dh), lambda b, s, g, p: (b, s, g, p, 0, 0)),
        out_shape=jax.ShapeDtypeStruct((B, 2, G, 2, n_sel, dh), jnp.float32),
        compiler_params=pltpu.CompilerParams(dimension_semantics=("parallel",) * 4, vmem_limit_bytes=VMEM_LIMIT),
        name="nsa_compress",
    )(x, cmp_pos.reshape(2, 1, ld), cmp_w.reshape(2, ld, dh).astype(jnp.bfloat16))


def _nsa_prompt_kernel(q_ref, gt_ref, kc_ref, vct_ref, ks_ref, vst_ref, kw_ref, vwt_ref, o_ref,
                       sel_scr, m_scr, l_scr, acc_scr):
    f32, bf16 = jnp.float32, jnp.bfloat16
    i = pl.program_id(2)
    R, QB = NSA_R, Q_BLOCK
    W = R * QB
    T = ks_ref.shape[2]
    n_sel = T // SEL_BLOCK
    nt = (((1,), (1,)), ((), ()))
    qb = q_ref[0]
    q = jnp.concatenate([qb[:, r * HEAD_DIM:(r + 1) * HEAD_DIM] for r in range(R)], axis=0)
    q = (q * HEAD_DIM ** -0.5).astype(bf16)
    lane = lax.broadcasted_iota(jnp.int32, (1, W), 1)
    qpos = i * QB + (lane & (QB - 1))

    s = lax.dot_general(kc_ref[0, 0], q, nt, preferred_element_type=f32)
    row = lax.broadcasted_iota(jnp.int32, (2 * n_sel, 1), 0)
    cidx = jnp.where(row < n_sel, 2 * row, 2 * (row - n_sel) + 1)
    p_c = _softmax_rows(s, (cidx + 1) * CMP_BLOCK - 1 <= qpos)
    o_c = jnp.dot(vct_ref[0, 0], p_c.astype(bf16), preferred_element_type=f32)

    p_r = p_c[:, 0:QB]
    for r in range(1, R):
        p_r = p_r + p_c[:, r * QB:(r + 1) * QB]
    imp = p_r[:n_sel] + p_r[n_sel:]
    blk = lax.broadcasted_iota(jnp.int32, (n_sel, QB), 0)
    qp = qpos[:, :QB]
    imp = jnp.where(blk == qp // SEL_BLOCK, jnp.inf, jnp.where(blk * SEL_BLOCK <= qp, imp, -jnp.inf))
    chosen = [jnp.zeros((n_sel, QB), f32)]

    def pick(r, m, sel, chosen=chosen):
        chosen[0] = jnp.where(sel & (m > -jnp.inf), 1.0, chosen[0])

    _extract_top(imp, min(TOP_N, n_sel), pick)
    sel_scr[...] = jnp.concatenate([chosen[0]] * R, axis=1)

    m_scr[...] = jnp.full(m_scr.shape, -jnp.inf, f32)
    l_scr[...] = jnp.zeros(l_scr.shape, f32)
    acc_scr[...] = jnp.zeros(acc_scr.shape, f32)
    blk_per_tile = NSA_TK // SEL_BLOCK

    def tile_body(kt, carry):
        k0 = pl.multiple_of(kt * NSA_TK, NSA_TK)
        s = lax.dot_general(ks_ref[0, 0, pl.ds(k0, NSA_TK), :], q, nt, preferred_element_type=f32)
        selrows = sel_scr[pl.ds(pl.multiple_of(kt * blk_per_tile, blk_per_tile), blk_per_tile), :]
        kpos = k0 + lax.broadcasted_iota(jnp.int32, (NSA_TK, 1), 0)
        parts = []
        for j in range(blk_per_tile):
            rows = slice(j * SEL_BLOCK, (j + 1) * SEL_BLOCK)
            ok = (selrows[j:j + 1, :] > 0.0) & (kpos[rows] <= qpos)
            parts.append(jnp.where(ok, s[rows], -jnp.inf))
        _online_softmax_tile(jnp.concatenate(parts, axis=0), vst_ref[0, 0, :, pl.ds(k0, NSA_TK)], m_scr, l_scr, acc_scr)
        return carry

    lax.fori_loop(0, (i * QB + QB + NSA_TK - 1) // NSA_TK, tile_body, 0)
    o_s = acc_scr[...] * (1.0 / jnp.maximum(l_scr[...], 1e-30))

    w0 = pl.multiple_of(jnp.clip(i * QB + QB - NSA_WSPAN, 0, T - NSA_WSPAN), QB)
    s = lax.dot_general(kw_ref[0, 0, pl.ds(w0, NSA_WSPAN), :], q, nt, preferred_element_type=f32)
    kpos = w0 + lax.broadcasted_iota(jnp.int32, (NSA_WSPAN, 1), 0)
    p_w = _softmax_rows(s, (kpos <= qpos) & (kpos > qpos - WINDOW))
    o_w = jnp.dot(vwt_ref[0, 0, :, pl.ds(w0, NSA_WSPAN)], p_w.astype(bf16), preferred_element_type=f32)

    g = gt_ref[0, 0, 0]
    o = g[0:1] * o_c + g[1:2] * o_s + g[2:3] * o_w
    for r in range(R):
        o_ref[0, :, r * HEAD_DIM:(r + 1) * HEAD_DIM] = o[:, r * QB:(r + 1) * QB].T


def _nsa_prompt_attention(q, gates, new_rows, win, cmp_pos, cmp_w):
    B, T, _ = q.shape
    G, R, dh, QB = NSA_KV_GROUPS, NSA_R, HEAD_DIM, Q_BLOCK
    assert T % NSA_TK == 0 and T >= NSA_WSPAN and (T // SEL_BLOCK) % 8 == 0
    nqb = T // QB
    n_sel = T // SEL_BLOCK
    W = R * QB
    bf16 = jnp.bfloat16
    cmp = _nsa_compress(new_rows[:, :, :2], cmp_pos, cmp_w).reshape(B, 2, G, 2 * n_sel, dh)
    kc = cmp[:, 0].astype(bf16)
    vct = jnp.swapaxes(cmp[:, 1], -1, -2).astype(bf16)
    ks = jnp.transpose(new_rows[:, :, 2], (0, 2, 1, 3)).astype(bf16)
    vst = jnp.transpose(new_rows[:, :, 3], (0, 2, 3, 1)).astype(bf16)
    kw = jnp.transpose(win[:, :, 0], (0, 2, 1, 3)).astype(bf16)
    vwt = jnp.transpose(win[:, :, 1], (0, 2, 3, 1)).astype(bf16)
    gt = jnp.transpose(gates.reshape(B, nqb, QB, G, R, 3), (0, 3, 1, 5, 4, 2)).reshape(B, G, nqb, 3, W)
    gt = jnp.pad(gt, ((0, 0), (0, 0), (0, 0), (0, 5), (0, 0)))
    per_bg = lambda *shape: pl.BlockSpec((1, 1) + shape, lambda b, g, i: (b, g, 0, 0))
    return pl.pallas_call(
        _nsa_prompt_kernel,
        grid=(B, G, nqb),
        in_specs=[pl.BlockSpec((1, QB, R * dh), lambda b, g, i: (b, i, g)),
                  pl.BlockSpec((1, 1, 1, 8, W), lambda b, g, i: (b, g, i, 0, 0)),
                  per_bg(2 * n_sel, dh), per_bg(dh, 2 * n_sel), per_bg(T, dh), per_bg(dh, T), per_bg(T, dh), per_bg(dh, T)],
        out_specs=pl.BlockSpec((1, QB, R * dh), lambda b, g, i: (b, i, g)),
        out_shape=jax.ShapeDtypeStruct(q.shape, jnp.float32),
        scratch_shapes=[pltpu.VMEM((n_sel, W), jnp.float32), pltpu.VMEM((1, W), jnp.float32),
                        pltpu.VMEM((1, W), jnp.float32), pltpu.VMEM((dh, W), jnp.float32)],
        compiler_params=pltpu.CompilerParams(dimension_semantics=("parallel", "parallel", "arbitrary"),
                                             vmem_limit_bytes=VMEM_LIMIT),
        name="nsa_prompt",
    )(q, gt, kc, vct, ks, vst, kw, vwt)


_TN = (((0,), (0,)), ((), ()))
_NT = (((1,), (1,)), ((), ()))


def _nsa_sample_kernel(pt_ref, xc_ref, *refs, n_pages, past_len):
    pages = refs[:n_pages]
    q_ref, rows_ref, wnew_ref, wbuf_ref, gt_ref, pos_ref, w_ref, o_ref, sel_scr = refs[n_pages:]
    f32, bf16 = jnp.float32, jnp.bfloat16
    G, R, dh = NSA_KV_GROUPS, NSA_R, HEAD_DIM
    T = q_ref.shape[1]
    n_blk = past_len // SEL_BLOCK
    n_w = wbuf_ref.shape[1]
    lane = lax.broadcasted_iota(jnp.int32, (1, LANE), 1)
    qidx = lane & (T - 1)
    qpos = past_len + qidx
    new_row = lax.broadcasted_iota(jnp.int32, (LANE, 1), 0)
    new_ok = (new_row < T) & (new_row <= qidx)

    def new_tile(ref, col):
        return jnp.concatenate([ref[0, :, col * dh:(col + 1) * dh], jnp.zeros((LANE - T, dh), f32)], axis=0).astype(bf16)

    cmp = []
    for s in range(2):
        xs = jnp.concatenate(
            [jnp.concatenate([xc_ref[0, :, l * 4 * dh + (s * G + g) * dh:l * 4 * dh + (s * G + g + 1) * dh]
                              for l in range(CMP_BLOCK)], axis=1) for g in range(G)], axis=0)
        cmp.append(jnp.dot((xs + pos_ref[s]).astype(bf16), w_ref[s], preferred_element_type=f32))
    for g in range(G):
        q = jnp.concatenate([q_ref[0, :, (g * R + r) * dh:(g * R + r + 1) * dh] for r in range(R)]
                            + [jnp.zeros((LANE - R * T, dh), f32)], axis=0)
        q = (q * dh ** -0.5).astype(bf16)
        kc = cmp[0][g * 2 * n_blk:(g + 1) * 2 * n_blk]
        vc = cmp[1][g * 2 * n_blk:(g + 1) * 2 * n_blk]
        s_c = lax.dot_general(kc.astype(bf16), q, _NT, preferred_element_type=f32)
        p_c = _softmax_rows(s_c, jnp.full(s_c.shape, True))
        o_c = lax.dot_general(vc.astype(bf16), p_c.astype(bf16), _TN, preferred_element_type=f32)
        p_r = p_c
        for r in range(1, R):
            p_r = p_r + pltpu.roll(p_c, LANE - r * T, axis=1)
        imp = p_r[:n_blk] + p_r[n_blk:]
        chosen = [jnp.zeros((n_blk, LANE), f32)]

        def pick(_, m, sel, chosen=chosen):
            chosen[0] = jnp.where(sel, 1.0, chosen[0])

        _extract_top(imp, min(TOP_N, n_blk + 1) - 1, pick)
        ch = jnp.where(lane < T, chosen[0], 0.0)
        ch4 = ch
        for r in range(1, R):
            ch4 = ch4 + pltpu.roll(ch, r * T, axis=1)
        sel_scr[...] = ch4
        ks = jnp.concatenate([pg[0, :, g * dh:(g + 1) * dh] for pg in pages], axis=0).astype(bf16)
        vs = jnp.concatenate([pg[0, :, (G + g) * dh:(G + g + 1) * dh] for pg in pages], axis=0).astype(bf16)
        s_p = lax.dot_general(ks, q, _NT, preferred_element_type=f32)
        s_p = jnp.concatenate([jnp.where(sel_scr[j:j + 1, :] > 0.0, s_p[j * SEL_BLOCK:(j + 1) * SEL_BLOCK], -jnp.inf)
                               for j in range(n_blk)], axis=0)
        s_n = lax.dot_general(new_tile(rows_ref, 2 * G + g), q, _NT, preferred_element_type=f32)
        s_all = jnp.concatenate([s_p, jnp.where(new_ok, s_n, -jnp.inf)], axis=0)
        p_s = _softmax_rows(s_all, s_all > -jnp.inf).astype(bf16)
        o_s = (lax.dot_general(vs, p_s[:past_len], _TN, preferred_element_type=f32)
               + lax.dot_general(new_tile(rows_ref, 3 * G + g), p_s[past_len:], _TN, preferred_element_type=f32))
        kw = wbuf_ref[0, :, g * dh:(g + 1) * dh].astype(bf16)
        vw = wbuf_ref[0, :, (G + g) * dh:(G + g + 1) * dh].astype(bf16)
        s_w = lax.dot_general(kw, q, _NT, preferred_element_type=f32)
        wpos = past_len - n_w + lax.broadcasted_iota(jnp.int32, (n_w, 1), 0)
        s_w = jnp.where(wpos > qpos - WINDOW, s_w, -jnp.inf)
        s_n = lax.dot_general(new_tile(wnew_ref, g), q, _NT, preferred_element_type=f32)
        s_all = jnp.concatenate([s_w, jnp.where(new_ok, s_n, -jnp.inf)], axis=0)
        p_w = _softmax_rows(s_all, s_all > -jnp.inf).astype(bf16)
        o_w = (lax.dot_general(vw, p_w[:n_w], _TN, preferred_element_type=f32)
               + lax.dot_general(new_tile(wnew_ref, G + g), p_w[n_w:], _TN, preferred_element_type=f32))
        gt = gt_ref[0, g]
        o = (gt[0:1] * o_c + gt[1:2] * o_s + gt[2:3] * o_w).T
        for r in range(R):
            o_ref[0, :, (g * R + r) * dh:(g * R + r + 1) * dh] = o[r * T:(r + 1) * T]


def _nsa_sample_attention(q, gates, new_rows, win_new, cache, page_table, wbuf, cmp_pos, cmp_w):
    B, T, _ = q.shape
    G, R, dh = NSA_KV_GROUPS, NSA_R, HEAD_DIM
    n_pool, page = cache.shape[:2]
    n_pages = page_table.shape[1]
    past_len = n_pages * page
    n_w = wbuf.shape[1]
    assert T == 8 and R * T <= LANE and page % SEL_BLOCK == 0 and n_w + T > WINDOW
    assert past_len % SEL_BLOCK == 0 and T <= CMP_BLOCK
    row_w = 4 * G * dh
    per_page = page // SEL_BLOCK
    f32 = jnp.float32
    xc = cache.reshape(n_pool, per_page, 2, CMP_BLOCK, row_w)[page_table][..., :2 * G * dh]
    xc = jnp.transpose(xc, (0, 3, 1, 2, 4, 5)).reshape(B, 2 * n_pages * per_page, CMP_BLOCK * 2 * G * dh)
    n_blk = past_len // SEL_BLOCK
    gt = jnp.transpose(gates.reshape(B, T, G, R, 3), (0, 2, 4, 3, 1)).reshape(B, G, 3, R * T)
    gt = jnp.pad(gt, ((0, 0), (0, 0), (0, 5), (0, LANE - R * T)))
    cache2 = cache.reshape(n_pool, page, row_w)
    page_spec = lambda p: pl.BlockSpec((1, page, 2 * G * dh), lambda b, pt: (pt[b, p], 0, 1))
    seq = lambda *shape: pl.BlockSpec((1,) + shape, lambda b, pt: (b,) + (0,) * len(shape))
    whole = lambda *shape: pl.BlockSpec(shape, lambda b, pt: (0,) * len(shape))
    grid_spec = pltpu.PrefetchScalarGridSpec(
        num_scalar_prefetch=1,
        grid=(B,),
        in_specs=[seq(2 * n_blk, CMP_BLOCK * 2 * G * dh)] + [page_spec(p) for p in range(n_pages)]
                 + [seq(T, R * G * dh), seq(T, row_w), seq(T, 2 * G * dh), seq(n_w, 2 * G * dh), seq(G, 8, LANE),
                    whole(2, 1, CMP_BLOCK * dh), whole(2, CMP_BLOCK * dh, dh)],
        out_specs=seq(T, R * G * dh),
        scratch_shapes=[pltpu.VMEM((n_blk, LANE), f32)],
    )
    return pl.pallas_call(
        functools.partial(_nsa_sample_kernel, n_pages=n_pages, past_len=past_len),
        grid_spec=grid_spec,
        out_shape=jax.ShapeDtypeStruct(q.shape, f32),
        compiler_params=pltpu.CompilerParams(dimension_semantics=("parallel",), vmem_limit_bytes=VMEM_LIMIT),
        name="nsa_sample",
    )(page_table, xc, *([cache2] * n_pages), q, new_rows.reshape(B, T, row_w), win_new.reshape(B, T, 2 * G * dh),
      wbuf.reshape(B, n_w, 2 * G * dh), gt, cmp_pos.reshape(2, 1, CMP_BLOCK * dh),
      cmp_w.reshape(2, CMP_BLOCK * dh, dh).astype(jnp.bfloat16))


def nsa_mixer(h, pos0, past, wbuf, w_in, cmp_pos, cmp_w, w_out):
    B, T, _ = h.shape
    G, dh = NSA_KV_GROUPS, HEAD_DIM
    qd = NSA_HEADS * dh
    kvd = 6 * G * dh
    pos = pos0 + jnp.arange(T, dtype=jnp.int32)
    proj = _mm(h, w_in)
    q = rope(proj[..., :qd].reshape(B, T, NSA_HEADS, dh), pos)
    kv = proj[..., qd:qd + kvd].reshape(B, T, 6, G, dh)
    gates = jax.nn.sigmoid(proj[..., qd + kvd:].reshape(B, T, NSA_HEADS, 3))
    keys = rope(kv[:, :, 0::2].reshape(B, T, 3 * G, dh), pos).reshape(B, T, 3, G, dh)
    kv = jnp.stack([keys[:, :, 0], kv[:, :, 1], keys[:, :, 1], kv[:, :, 3], keys[:, :, 2], kv[:, :, 5]], axis=2)
    new_rows = kv[:, :, :4]
    win = kv[:, :, 4:] if past is None else jnp.concatenate([wbuf, kv[:, :, 4:]], 1)
    if past is None:
        o = _nsa_prompt_attention(q.reshape(B, T, qd), gates, new_rows, win, cmp_pos, cmp_w)
    else:
        cache, page_table = past
        o = _nsa_sample_attention(q.reshape(B, T, qd), gates, new_rows, kv[:, :, 4:], cache, page_table, wbuf,
                                  cmp_pos, cmp_w)
    new_win = win[:, -min(WINDOW, win.shape[1]):]
    return _mm(o.reshape(B, T, D_MODEL), w_out), new_rows, new_win


DIFF_QB = 256
DIFF_TK = 512


def _online_softmax_tile(s, vt_tile, m_scr, l_scr, acc_scr):
    m_old = m_scr[...]
    m_new = jnp.maximum(m_old, jnp.max(s, axis=0, keepdims=True))
    m_safe = jnp.where(m_new == -jnp.inf, 0.0, m_new)
    alpha = jnp.exp(m_old - m_safe)
    p = jnp.exp(s - m_safe)
    l_scr[...] = alpha * l_scr[...] + jnp.sum(p, axis=0, keepdims=True)
    acc_scr[...] = alpha * acc_scr[...] + jnp.dot(vt_tile, p.astype(jnp.bfloat16), preferred_element_type=jnp.float32)
    m_scr[...] = m_new


def _diff_prompt_kernel(lam_ref, q_ref, k_ref, vt_ref, g_ref, o_ref, m_scr, l_scr, acc_scr, *, out_scale):
    f32, bf16 = jnp.float32, jnp.bfloat16
    i = pl.program_id(2)
    QB, TK, dd = DIFF_QB, DIFF_TK, DIFF_DH
    W = 2 * QB
    nt = (((1,), (1,)), ((), ()))
    qb = q_ref[0] * dd ** -0.5
    col = lax.broadcasted_iota(jnp.int32, (1, 2 * dd), 1)
    q = jnp.concatenate([jnp.where(col < dd, qb, 0.0), jnp.where(col >= dd, qb, 0.0)], axis=0).astype(bf16)
    lane = lax.broadcasted_iota(jnp.int32, (1, W), 1)
    qpos = i * QB + (lane & (QB - 1))
    m_scr[...] = jnp.full(m_scr.shape, -jnp.inf, f32)
    l_scr[...] = jnp.zeros(l_scr.shape, f32)
    acc_scr[...] = jnp.zeros(acc_scr.shape, f32)
    n_full = (i * QB) // TK

    def full_tile(kt, carry):
        k0 = pl.multiple_of(kt * TK, TK)
        s = lax.dot_general(k_ref[0, pl.ds(k0, TK), :], q, nt, preferred_element_type=f32)
        _online_softmax_tile(s, vt_ref[0, 0, :, pl.ds(k0, TK)], m_scr, l_scr, acc_scr)
        return carry

    lax.fori_loop(0, n_full, full_tile, 0)
    k0 = pl.multiple_of(n_full * TK, TK)
    s = lax.dot_general(k_ref[0, pl.ds(k0, TK), :], q, nt, preferred_element_type=f32)
    kpos = k0 + lax.broadcasted_iota(jnp.int32, (TK, 1), 0)
    _online_softmax_tile(jnp.where(kpos <= qpos, s, -jnp.inf), vt_ref[0, 0, :, pl.ds(k0, TK)], m_scr, l_scr, acc_scr)
    o = acc_scr[...] * (1.0 / jnp.maximum(l_scr[...], 1e-30))
    o = o[:, :QB] - lam_ref[0, 0] * o[:, QB:]
    o = o * lax.rsqrt(jnp.mean(o * o, axis=0, keepdims=True) + EPS) * (g_ref[...] * out_scale)
    o_ref[0] = o.T


def _diff_prompt_attention(q, k, v, lam, subln_g, out_scale):
    B, T, _ = q.shape
    H, dv = DIFF_HEADS, 2 * DIFF_DH
    assert T % DIFF_TK == 0 and DIFF_TK % DIFF_QB == 0
    bf16 = jnp.bfloat16
    vt = jnp.transpose(v.reshape(B, T, H, dv), (0, 2, 3, 1)).astype(bf16)
    W = 2 * DIFF_QB
    return pl.pallas_call(
        functools.partial(_diff_prompt_kernel, out_scale=out_scale),
        grid=(B, H, T // DIFF_QB),
        in_specs=[pl.BlockSpec(memory_space=pltpu.SMEM),
                  pl.BlockSpec((1, DIFF_QB, dv), lambda b, h, i: (b, i, h)),
                  pl.BlockSpec((1, T, dv), lambda b, h, i: (b, 0, h)),
                  pl.BlockSpec((1, 1, dv, T), lambda b, h, i: (b, h, 0, 0)),
                  pl.BlockSpec((dv, 1), lambda b, h, i: (0, 0))],
        out_specs=pl.BlockSpec((1, DIFF_QB, dv), lambda b, h, i: (b, i, h)),
        out_shape=jax.ShapeDtypeStruct(q.shape, jnp.float32),
        scratch_shapes=[pltpu.VMEM((1, W), jnp.float32), pltpu.VMEM((1, W), jnp.float32), pltpu.VMEM((dv, W), jnp.float32)],
        compiler_params=pltpu.CompilerParams(dimension_semantics=("parallel", "parallel", "arbitrary"),
                                             vmem_limit_bytes=VMEM_LIMIT),
        name="diff_prompt",
    )(lam.reshape(1, 1), q, k.astype(bf16), vt, subln_g.reshape(dv, 1))


def _diff_sample_kernel(pt_ref, lam_ref, *refs, n_pages, out_scale):
    pages = refs[:n_pages]
    q_ref, new_ref, g_ref, o_ref = refs[n_pages:]
    f32, bf16 = jnp.float32, jnp.bfloat16
    H, dd = DIFF_HEADS, DIFF_DH
    T = q_ref.shape[1]
    D = H * 2 * dd
    qt = jnp.concatenate([q_ref[0] * dd ** -0.5] * (LANE // T), axis=0)
    rowi = lax.broadcasted_iota(jnp.int32, (LANE, D), 0)
    coli = lax.broadcasted_iota(jnp.int32, (LANE, D), 1)
    qbd = jnp.where(coli // dd == rowi // T, qt, 0.0).astype(bf16)
    lane = lax.broadcasted_iota(jnp.int32, (1, LANE), 1)
    qidx = lane & (T - 1)
    s = [lax.dot_general(pg[0, :, 0:D].astype(bf16), qbd, _NT, preferred_element_type=f32) for pg in pages]
    new_row = lax.broadcasted_iota(jnp.int32, (LANE, 1), 0)
    pad = jnp.zeros((LANE - T, D), f32)
    k_new = jnp.concatenate([new_ref[0, :, 0:D], pad], axis=0).astype(bf16)
    v_new = jnp.concatenate([new_ref[0, :, D:2 * D], pad], axis=0).astype(bf16)
    s_n = lax.dot_general(k_new, qbd, _NT, preferred_element_type=f32)
    s_n = jnp.where((new_row < T) & (new_row <= qidx), s_n, -jnp.inf)
    m = jnp.max(s_n, axis=0, keepdims=True)
    for sp in s:
        m = jnp.maximum(m, jnp.max(sp, axis=0, keepdims=True))
    p_n = jnp.exp(s_n - m)
    l = jnp.sum(p_n, axis=0, keepdims=True)
    acc = lax.dot_general(v_new, p_n.astype(bf16), _TN, preferred_element_type=f32)
    for pg, sp in zip(pages, s):
        p = jnp.exp(sp - m)
        l = l + jnp.sum(p, axis=0, keepdims=True)
        acc = acc + lax.dot_general(pg[0, :, D:2 * D].astype(bf16), p.astype(bf16), _TN, preferred_element_type=f32)
    inv_l = 1.0 / jnp.maximum(l, 1e-30)
    lam = lam_ref[0, 0]
    for h in range(H):
        a = acc[h * 2 * dd:(h + 1) * 2 * dd, :] * inv_l
        d = a - lam * pltpu.roll(a, LANE - T, axis=1)
        if h:
            d = pltpu.roll(d, LANE - h * 2 * T, axis=1)
        d = d * lax.rsqrt(jnp.mean(d * d, axis=0, keepdims=True) + EPS) * (g_ref[...] * out_scale)
        o_ref[0, :, h * 2 * dd:(h + 1) * 2 * dd] = d.T[0:T]


def _diff_sample_attention(q, new_rows, cache, page_table, lam, subln_g, out_scale):
    B, T, D = q.shape
    n_pool, page = cache.shape[:2]
    n_pages = page_table.shape[1]
    assert T == 8 and 2 * DIFF_HEADS * T == LANE
    f32 = jnp.float32
    seq = lambda *shape: pl.BlockSpec((1,) + shape, lambda b, pt: (b,) + (0,) * len(shape))
    page_spec = lambda p: pl.BlockSpec((1, page, 2 * D), lambda b, pt: (pt[b, p], 0, 0))
    grid_spec = pltpu.PrefetchScalarGridSpec(
        num_scalar_prefetch=1,
        grid=(B,),
        in_specs=[pl.BlockSpec(memory_space=pltpu.SMEM)] + [page_spec(p) for p in range(n_pages)]
                 + [seq(T, D), seq(T, 2 * D), pl.BlockSpec((2 * DIFF_DH, 1), lambda b, pt: (0, 0))],
        out_specs=seq(T, D),
    )
    return pl.pallas_call(
        functools.partial(_diff_sample_kernel, n_pages=n_pages, out_scale=out_scale),
        grid_spec=grid_spec,
        out_shape=jax.ShapeDtypeStruct(q.shape, f32),
        compiler_params=pltpu.CompilerParams(dimension_semantics=("parallel",), vmem_limit_bytes=56 * 1024 * 1024),
        name="diff_sample",
    )(page_table, lam.reshape(1, 1), *([cache.reshape(n_pool, page, 2 * D)] * n_pages), q,
      new_rows.reshape(B, T, 2 * D), subln_g.reshape(2 * DIFF_DH, 1))


def diff_mixer(h, pos0, past, layer_idx, w_in, lq1, lk1, lq2, lk2, subln_g, w_out):
    B, T, _ = h.shape
    H, dd = DIFF_HEADS, DIFF_DH
    f32 = jnp.float32
    dt = h.dtype
    pos = pos0 + jnp.arange(T, dtype=jnp.int32)
    proj = _mm(h, w_in)
    q = rope(proj[..., :D_MODEL].reshape(B, T, 2 * H, dd), pos)
    k = rope(proj[..., D_MODEL:2 * D_MODEL].reshape(B, T, 2 * H, dd), pos)
    v = proj[..., 2 * D_MODEL:].reshape(B, T, 2 * H, dd)
    new_rows = jnp.stack([k, v], 2)
    lam_init = 0.8 - 0.6 * math.exp(-0.3 * layer_idx)
    lam = (jnp.exp(jnp.sum(lq1.astype(f32) * lk1.astype(f32))) - jnp.exp(jnp.sum(lq2.astype(f32) * lk2.astype(f32))) + lam_init)
    if past is None:
        o = _diff_prompt_attention(q.reshape(B, T, D_MODEL), k.reshape(B, T, D_MODEL), v.reshape(B, T, D_MODEL),
                                   lam, subln_g, 1.0 - lam_init)
        return _mm(o, w_out), new_rows
    cache, page_table = past
    o = _diff_sample_attention(q.reshape(B, T, D_MODEL), new_rows, cache, page_table, lam, subln_g, 1.0 - lam_init)
    return _mm(o, w_out), new_rows


PEER_TB = 512
PEER_ET = 1024
_PEER_CAND = [(a, b) for a in range(PEER_TOPK) for b in range(PEER_TOPK) if (a + 1) * (b + 1) <= PEER_TOPK]


def _extract_top(s, n_iter, on_pick, break_ties=True):
    rows = s.shape[0]
    iota = lax.broadcasted_iota(jnp.int32, s.shape, 0)
    for r in range(n_iter):
        m = jnp.max(s, axis=0, keepdims=True)
        if break_ties:
            idx = jnp.min(jnp.where(s == m, iota, rows), axis=0, keepdims=True)
            sel = iota == idx
        else:
            sel = s == m
        on_pick(r, m, sel)
        s = jnp.where(sel, -jnp.inf, s)


def _peer_route_kernel(x_ref, shift_ref, scale_ref, g_ref, wq_ref, k1_ref, k2_ref, aof_ref,
                       h_ref, c1_ref, cnt1_ref, rank2_ref, e2_ref, q_scr, v1_scr, v2_scr, cand_scr):
    f32 = jnp.float32
    x = x_ref[...]
    tb = x.shape[0] * x.shape[1]
    y = x * lax.rsqrt(jnp.mean(x * x, -1, keepdims=True) + EPS) * g_ref[...]
    h = (y * (1.0 + scale_ref[...]) + shift_ref[...]).reshape(tb, D_MODEL)
    hb = h.astype(jnp.bfloat16)
    h_ref[...] = hb
    q = jnp.dot(hb, wq_ref[...], preferred_element_type=f32).astype(jnp.bfloat16)
    for hh in range(PEER_HEADS):
        q_scr[hh] = q[:, hh * PEER_QDIM:(hh + 1) * PEER_QDIM]
    nt = (((1,), (1,)), ((), ()))
    n_chunk = tb // LANE
    cand_scr[...] = jnp.full(cand_scr.shape, -jnp.inf, f32)

    def process(it, exact):
        hh = it // n_chunk
        c0 = pl.multiple_of((it % n_chunk) * LANE, LANE)
        qc = q_scr[hh, pl.ds(c0, LANE), :]
        s1 = lax.dot_general(k1_ref[hh], qc, nt, preferred_element_type=f32)
        s2 = lax.dot_general(k2_ref[hh], qc, nt, preferred_element_type=f32)
        ranks = []
        for s, v_scr in ((s1, v1_scr), (s2, v2_scr)):
            rank = [jnp.full(s.shape, float(N_KEYS), f32)]

            def pick(r, m, sel, v_scr=v_scr, rank=rank):
                v_scr[r:r + 1, :] = m
                rank[0] = jnp.where(sel, float(r), rank[0])

            _extract_top(s, PEER_TOPK, pick, break_ties=exact)
            ranks.append(rank[0])
        for k, (a, b) in enumerate(_PEER_CAND):
            cand_scr[k:k + 1, :] = v1_scr[a:a + 1, :] + v2_scr[b:b + 1, :]
        top1 = v1_scr[0:1, :]
        top2 = v2_scr[0:1, :]
        top_val = top1 + top2
        a_of = aof_ref[...]
        iota16 = lax.broadcasted_iota(jnp.int32, (PEER_TOPK, LANE), 0)
        st = {"z": jnp.zeros((1, LANE), f32), "cnt": jnp.zeros((PEER_TOPK, LANE), f32), "taken": jnp.zeros((1, LANE), f32)}

        def pick_c(r, m, sel, st=st):
            st["z"] = st["z"] + jnp.exp(m - top_val)
            a_sel = jnp.max(jnp.where(sel, a_of, 0), axis=0, keepdims=True)
            st["cnt"] = st["cnt"] + (iota16 == a_sel).astype(f32)
            if not exact:
                st["taken"] = st["taken"] + jnp.sum(sel.astype(f32), axis=0, keepdims=True)

        _extract_top(cand_scr[...], PEER_TOPK, pick_c, break_ties=exact)
        inv_z = 1.0 / st["z"]
        cnt1 = jnp.zeros(s1.shape, f32)
        for a in range(PEER_TOPK):
            cnt1 = jnp.where(ranks[0] == float(a), st["cnt"][a:a + 1], cnt1)
        c1_ref[hh, :, pl.ds(c0, LANE)] = jnp.exp(s1 - top1) * inv_z
        cnt1_ref[hh, :, pl.ds(c0, LANE)] = cnt1
        rank2_ref[hh, :, pl.ds(c0, LANE)] = ranks[1]
        e2_ref[hh, :, pl.ds(c0, LANE)] = jnp.exp(s2 - top2)
        if exact:
            return None
        want = float(PEER_TOPK)
        tied = st["taken"] != want
        for rank in ranks:
            tied = tied | (jnp.sum((rank < want).astype(f32), axis=0, keepdims=True) != want)
        return tied

    def body(it, carry):
        tied = process(it, exact=False)

        @pl.when(jnp.max(tied.astype(f32)) > 0.0)
        def _():
            process(it, exact=True)

        return carry

    lax.fori_loop(0, PEER_HEADS * n_chunk, body, 0)


def _peer_dense_kernel(hb_ref, u_ref, vt_ref, c1_ref, cnt1_ref, rank2_ref, e2_ref, x_ref, gm_ref,
                       o_ref, hta_ref, htb_ref, gha_ref, ghb_ref, acc_ref):
    f32 = jnp.float32
    j = pl.program_id(1)
    tb = hb_ref.shape[0]

    @pl.when(j == 0)
    def _():
        acc_ref[...] = jnp.zeros_like(acc_ref)

    bf16 = jnp.bfloat16
    n_rows = PEER_ET // N_KEYS
    sub = 32

    def gates_times_act(ht_half, gh_half, t0):
        for c0 in range(0, half, LANE):
            lanes = slice(t0 + c0, t0 + c0 + LANE)
            for s0 in range(0, N_KEYS, sub):
                g = [jnp.zeros((sub, LANE), f32) for _ in range(n_rows)]
                for hh in range(PEER_HEADS):
                    rk = rank2_ref[hh, s0:s0 + sub, lanes]
                    ev = e2_ref[hh, s0:s0 + sub, lanes]
                    for r in range(n_rows):
                        g[r] = g[r] + jnp.where(rk < cnt1_ref[hh, r:r + 1, lanes], ev * c1_ref[hh, r:r + 1, lanes], 0.0)
                for r in range(n_rows):
                    rows = slice(r * N_KEYS + s0, r * N_KEYS + s0 + sub)
                    pre = ht_half[rows, c0:c0 + LANE]
                    act = 0.5 * pre * (1.0 + lax.erf(pre * (2.0 ** -0.5)))
                    gh_half[rows, c0:c0 + LANE] = (g[r] * act).astype(bf16)

    half = tb // 2
    nt = (((1,), (1,)), ((), ()))
    hta_ref[...] = lax.dot_general(u_ref[...], hb_ref[0:half, :], nt, preferred_element_type=f32)
    htb_ref[...] = lax.dot_general(u_ref[...], hb_ref[half:tb, :], nt, preferred_element_type=f32)
    gates_times_act(hta_ref, gha_ref, 0)
    acc_ref[:, 0:half] += jnp.dot(vt_ref[...], gha_ref[...], preferred_element_type=f32)
    gates_times_act(htb_ref, ghb_ref, half)
    acc_ref[:, half:tb] += jnp.dot(vt_ref[...], ghb_ref[...], preferred_element_type=f32)

    @pl.when(j == pl.num_programs(1) - 1)
    def _():
        upd = acc_ref[...].T.reshape(x_ref.shape)
        o_ref[...] = x_ref[...] + gm_ref[...] * upd


def _peer_sublayer(x, shift, scale, gate, norm_g, wq_b, k1p, k2p, u_b, vt_b):
    B, T, D = x.shape
    n = B * T
    tb = PEER_TB
    assert n % tb == 0
    if T % tb == 0:
        nbs, tper, per = 1, tb, T // tb
        xmap = lambda i, *_: (i // per, i % per, 0)
        mmap = lambda i, *_: (i // per, 0, 0)
    else:
        assert tb % T == 0 and T % 8 == 0
        nbs, tper = tb // T, T
        xmap = lambda i, *_: (i, 0, 0)
        mmap = lambda i, *_: (i, 0, 0)
    nblk = n // tb
    f32 = jnp.float32
    x_spec = pl.BlockSpec((nbs, tper, D), xmap)
    m_spec = pl.BlockSpec((nbs, 1, D), mmap)
    n_cand_pad = -(-len(_PEER_CAND) // 8) * 8
    a_of = jnp.asarray(np.broadcast_to(np.array([a for a, _ in _PEER_CAND] + [0] * (n_cand_pad - len(_PEER_CAND)),
                                                np.int32)[:, None], (n_cand_pad, LANE)))
    route_shape = lambda dt: jax.ShapeDtypeStruct((PEER_HEADS, N_KEYS, n), dt)
    route_spec = pl.BlockSpec((PEER_HEADS, N_KEYS, tb), lambda i: (0, 0, i))
    hb, c1, cnt1, rank2, e2 = pl.pallas_call(
        _peer_route_kernel,
        grid=(nblk,),
        in_specs=[x_spec, m_spec, m_spec,
                  pl.BlockSpec((1, D), lambda i: (0, 0)),
                  pl.BlockSpec((D, PEER_HEADS * PEER_QDIM), lambda i: (0, 0)),
                  pl.BlockSpec((PEER_HEADS, N_KEYS, PEER_QDIM), lambda i: (0, 0, 0)),
                  pl.BlockSpec((PEER_HEADS, N_KEYS, PEER_QDIM), lambda i: (0, 0, 0)),
                  pl.BlockSpec((n_cand_pad, LANE), lambda i: (0, 0))],
        out_specs=[pl.BlockSpec((tb, D), lambda i: (i, 0)), route_spec, route_spec, route_spec, route_spec],
        out_shape=[jax.ShapeDtypeStruct((n, D), jnp.bfloat16)] + [route_shape(f32)] * 4,
        scratch_shapes=[pltpu.VMEM((PEER_HEADS, tb, PEER_QDIM), jnp.bfloat16), pltpu.VMEM((PEER_TOPK, LANE), f32),
                        pltpu.VMEM((PEER_TOPK, LANE), f32), pltpu.VMEM((n_cand_pad, LANE), f32)],
        compiler_params=pltpu.CompilerParams(dimension_semantics=("parallel",), vmem_limit_bytes=VMEM_LIMIT),
        name="peer_route",
    )(x, shift, scale, norm_g.reshape(1, D), wq_b, k1p, k2p, a_of)

    rows = PEER_ET // N_KEYS
    sub_spec = pl.BlockSpec((PEER_HEADS, rows, tb), lambda i, j: (0, j, i))
    full_spec = pl.BlockSpec((PEER_HEADS, N_KEYS, tb), lambda i, j: (0, 0, i))
    return pl.pallas_call(
        _peer_dense_kernel,
        grid=(nblk, N_EXPERTS // PEER_ET),
        in_specs=[pl.BlockSpec((tb, D), lambda i, j: (i, 0)),
                  pl.BlockSpec((PEER_ET, D), lambda i, j: (j, 0)),
                  pl.BlockSpec((D, PEER_ET), lambda i, j: (0, j)),
                  sub_spec, sub_spec, full_spec, full_spec,
                  pl.BlockSpec((nbs, tper, D), lambda i, j: xmap(i)),
                  pl.BlockSpec((nbs, 1, D), lambda i, j: mmap(i))],
        out_specs=pl.BlockSpec((nbs, tper, D), lambda i, j: xmap(i)),
        out_shape=jax.ShapeDtypeStruct(x.shape, x.dtype),
        scratch_shapes=[pltpu.VMEM((PEER_ET, tb // 2), f32), pltpu.VMEM((PEER_ET, tb // 2), f32),
                        pltpu.VMEM((PEER_ET, tb // 2), jnp.bfloat16), pltpu.VMEM((PEER_ET, tb // 2), jnp.bfloat16),
                        pltpu.VMEM((D, tb), f32)],
        compiler_params=pltpu.CompilerParams(dimension_semantics=("parallel", "arbitrary"), vmem_limit_bytes=VMEM_LIMIT),
        name="peer_dense",
    )(hb, u_b, vt_b, c1, cnt1, rank2, e2, x, gate)


def _peer_weights(w_q, k1, k2, u_tab, v_tab):
    bf16 = jnp.bfloat16
    half = PEER_QDIM // 2
    k1p = jnp.pad(k1, ((0, 0), (0, 0), (0, half))).astype(bf16)
    k2p = jnp.pad(k2, ((0, 0), (0, 0), (half, 0))).astype(bf16)
    return w_q.astype(bf16), k1p, k2p, u_tab.astype(bf16), v_tab.astype(bf16).T


def _layer_pool(cache, j, page_table):
    return cache.reshape((-1,) + cache.shape[2:]), page_table + j * cache.shape[1]


def kernel(x_prompt, x_sample, cache_nsa_kv, cache_diff_kv, state_nsa_window, state_gdn_S, state_gdn_conv,
           page_table, c_prompt, c_sample, ada_w, ada_b, norm_mix_g, norm_ffn_g, final_norm_g,
           gdn_w_in, gdn_conv_w, gdn_a_log, gdn_dt_bias, gdn_norm_g, gdn_w_out,
           nsa_w_in, nsa_cmp_pos, nsa_cmp_w, nsa_w_out,
           diff_w_in, diff_lq1, diff_lk1, diff_lq2, diff_lk2, diff_subln_g, diff_w_out,
           peer_w_q, peer_k1, peer_k2, peer_u, peer_v):
    past_len = page_table.shape[1] * cache_nsa_kv.shape[2]
    peer_w = [_peer_weights(peer_w_q[i], peer_k1[i], peer_k2[i], peer_u[i], peer_v[i]) for i in range(DEPTH)]

    def trunk(x, c, sample):
        B, T, _ = x.shape
        pos0 = past_len if sample else 0
        cs = jax.nn.silu(c)
        new_S, new_conv, new_nsa_kv, new_nsa_win, new_diff_kv = [], [], [], [], []
        for i in range(DEPTH):
            mod = (cs @ ada_w[i] + ada_b[i]).reshape(B, 6, 1, D_MODEL)
            h = rmsnorm(x, norm_mix_g[i]) * (1.0 + mod[:, 1]) + mod[:, 0]
            j = i // N_MIXERS
            kind = i % N_MIXERS
            if kind == 0:
                if sample:
                    S0 = state_gdn_S[j].astype(jnp.float32)
                    buf = state_gdn_conv[j]
                else:
                    S0 = jnp.zeros((B, GDN_HEADS, GDN_DK, GDN_DV), jnp.float32)
                    buf = jnp.zeros((B, CONV_W - 1, GDN_HEADS * (2 * GDN_DK + GDN_DV)), x.dtype)
                m, buf_n, S_n = gdn_mixer(h, buf, S0, gdn_w_in[j], gdn_conv_w[j], gdn_a_log[j], gdn_dt_bias[j], gdn_norm_g[j], gdn_w_out[j])
                new_S.append(S_n.astype(x.dtype))
                new_conv.append(buf_n)
            elif kind == 1:
                past = _layer_pool(cache_nsa_kv, j, page_table) if sample else None
                wbuf = state_nsa_window[j] if sample else None
                m, kv_n, win_n = nsa_mixer(h, pos0, past, wbuf, nsa_w_in[j], nsa_cmp_pos[j], nsa_cmp_w[j], nsa_w_out[j])
                new_nsa_kv.append(kv_n)
                new_nsa_win.append(win_n)
            else:
                past = _layer_pool(cache_diff_kv, j, page_table) if sample else None
                m, kv_n = diff_mixer(h, pos0, past, i, diff_w_in[j], diff_lq1[j], diff_lk1[j], diff_lq2[j], diff_lk2[j], diff_subln_g[j], diff_w_out[j])
                new_diff_kv.append(kv_n)
            x = x + mod[:, 2] * m
            x = _peer_sublayer(x, mod[:, 3], mod[:, 4], mod[:, 5], norm_ffn_g[i], *peer_w[i])
        y = rmsnorm(x, final_norm_g)
        return y, jnp.stack(new_S), jnp.stack(new_conv), jnp.stack(new_nsa_kv), jnp.stack(new_nsa_win), jnp.stack(new_diff_kv)

    y_prompt, p_gdn_S, p_gdn_conv, p_nsa_kv, p_nsa_win, p_diff_kv = trunk(x_prompt, c_prompt, False)
    y_sample, s_gdn_S, s_gdn_conv, s_nsa_kv, s_nsa_win, s_diff_kv = trunk(x_sample, c_sample, True)
    return (y_prompt, y_sample, p_gdn_S, p_gdn_conv, p_nsa_kv, p_nsa_win, p_diff_kv, s_gdn_S, s_gdn_conv, s_nsa_kv, s_nsa_win, s_diff_kv)
```

```python
import functools
import math

import jax
import jax.numpy as jnp
import numpy as np
from jax import lax
from jax.experimental import pallas as pl
from jax.experimental.pallas import tpu as pltpu

D_MODEL = 1024
DEPTH = 4
N_MIXERS = 3
HEAD_DIM = 128
GDN_HEADS = D_MODEL // HEAD_DIM
GDN_DK = HEAD_DIM
GDN_DV = HEAD_DIM
CONV_W = 4
GDN_CHUNK = 64
NSA_HEADS = D_MODEL // HEAD_DIM
NSA_KV_GROUPS = 2
CMP_BLOCK = 32
SEL_BLOCK = 64
TOP_N = 16
WINDOW = 512
Q_BLOCK = 128
DIFF_HEADS = D_MODEL // HEAD_DIM
DIFF_DH = D_MODEL // DIFF_HEADS // 2
PEER_HEADS = 8
N_KEYS = 128
N_EXPERTS = N_KEYS * N_KEYS
PEER_TOPK = 16
PEER_QDIM = 128
PEER_TOKEN_BLOCK = 256
ROPE_THETA = 10000.0
EPS = 1e-6

LANE = 128
VMEM_LIMIT = 48 * 1024 * 1024


def _mm_kernel(x_ref, w_ref, o_ref):
    o_ref[...] = jnp.dot(x_ref[...].astype(jnp.bfloat16), w_ref[...], preferred_element_type=jnp.float32)


def _mm(x, w, tm=256):
    lead = x.shape[:-1]
    K = x.shape[-1]
    N = w.shape[1]
    x2 = x.reshape(-1, K)
    M = x2.shape[0]
    n_pad = -(-N // 256) * 256
    wb = jnp.pad(w, ((0, 0), (0, n_pad - N))).astype(jnp.bfloat16)
    tm = min(tm, M)
    assert M % tm == 0
    out = pl.pallas_call(
        _mm_kernel,
        grid=(M // tm,),
        in_specs=[pl.BlockSpec((tm, K), lambda i: (i, 0)), pl.BlockSpec((K, n_pad), lambda i: (0, 0))],
        out_specs=pl.BlockSpec((tm, n_pad), lambda i: (i, 0)),
        out_shape=jax.ShapeDtypeStruct((M, n_pad), jnp.float32),
        compiler_params=pltpu.CompilerParams(dimension_semantics=("parallel",), vmem_limit_bytes=VMEM_LIMIT),
        name="proj_matmul",
    )(x2, wb)
    return out[:, :N].reshape(*lead, N)


def rmsnorm(x, g):
    xf = x.astype(jnp.float32)
    y = xf * lax.rsqrt(jnp.mean(xf * xf, -1, keepdims=True) + EPS)
    return (y * g.astype(jnp.float32)).astype(x.dtype)


def l2norm(x):
    xf = x.astype(jnp.float32)
    return (xf * lax.rsqrt(jnp.sum(xf * xf, -1, keepdims=True) + EPS)).astype(x.dtype)


def rope(x, pos):
    half = x.shape[-1] // 2
    inv = ROPE_THETA ** (-jnp.arange(half, dtype=jnp.float32) / half)
    ang = pos.astype(jnp.float32)[:, None] * inv[None, :]
    cos = jnp.cos(ang)[:, None, :]
    sin = jnp.sin(ang)[:, None, :]
    xf = x.astype(jnp.float32)
    x1, x2 = xf[..., :half], xf[..., half:]
    return jnp.concatenate([x1 * cos - x2 * sin, x2 * cos + x1 * sin], -1).astype(x.dtype)


def masked_softmax(s, mask):
    s = jnp.where(mask, s.astype(jnp.float32), -jnp.inf)
    m = jnp.max(s, -1, keepdims=True)
    m = jnp.where(jnp.isfinite(m), m, 0.0)
    p = jnp.exp(s - m)
    return p / jnp.maximum(jnp.sum(p, -1, keepdims=True), 1e-30)


def gather_pages(pool, page_table):
    g = pool[page_table]
    return g.reshape(g.shape[0], g.shape[1] * g.shape[2], *pool.shape[2:])


def causal_conv(x, buf, w):
    T = x.shape[1]
    xx = jnp.concatenate([buf, x], 1)
    y = xx[:, 0:T] * w[0]
    for j in range(1, CONV_W):
        y = y + xx[:, j:j + T] * w[j]
    return jax.nn.silu(y), xx[:, -(CONV_W - 1):]


def gated_delta_rule(q, k, v, g, beta, S0):
    f32 = jnp.float32
    B, T, H, dk = q.shape
    dv = v.shape[-1]
    C = GDN_CHUNK
    Tp = -(-T // C) * C

    def prep(a):
        a = a.astype(f32)
        a = jnp.pad(a, [(0, 0), (0, Tp - T)] + [(0, 0)] * (a.ndim - 2))
        a = a.reshape(B, Tp // C, C, *a.shape[2:])
        return jnp.moveaxis(a, (1, 3), (0, 2))

    qc = prep(q) * dk ** -0.5
    kc, vc, gc, bc = prep(k), prep(v), prep(g), prep(beta)
    gcum = jnp.cumsum(gc, -1)
    idx = jnp.arange(C)
    incl = idx[:, None] >= idx[None, :]
    strict = idx[:, None] > idx[None, :]
    decay = jnp.exp(jnp.where(incl, gcum[..., :, None] - gcum[..., None, :], -jnp.inf))
    kb = kc * bc[..., None]
    L = jnp.where(strict, jnp.einsum('nbhid,nbhjd->nbhij', kb, kc) * decay, 0.0)
    eye = jnp.eye(C, dtype=f32)
    A = eye + L
    Tinv = lax.linalg.triangular_solve(A, jnp.broadcast_to(eye, A.shape), left_side=True, lower=True, unit_diagonal=True)
    u = Tinv @ (vc * bc[..., None])
    w = Tinv @ (kb * jnp.exp(gcum)[..., None])
    a_intra = jnp.einsum('nbhid,nbhjd->nbhij', qc, kc) * decay

    def step(S, xs):
        q_i, k_i, u_i, w_i, a_i, g_i = xs
        v_new = u_i - w_i @ S
        o = (q_i * jnp.exp(g_i)[..., None]) @ S + a_i @ v_new
        g_last = g_i[..., -1:]
        S = S * jnp.exp(g_last)[..., None] + jnp.einsum('bhck,bhcv->bhkv', k_i * jnp.exp(g_last - g_i)[..., None], v_new)
        return S, o

    S, o = lax.scan(step, S0.astype(f32), (qc, kc, u, w, a_intra, gcum))
    o = jnp.moveaxis(o, (0, 2), (1, 3)).reshape(B, Tp, H, dv)[:, :T]
    return o, S


def gdn_mixer(h, conv_buf, S0, w_in, conv_w, a_log, dt_bias, norm_g, w_out):
    B, T, _ = h.shape
    H, dk, dv = GDN_HEADS, GDN_DK, GDN_DV
    n_qkv = H * (2 * dk + dv)
    f32 = jnp.float32
    proj = _mm(h, w_in)
    qkv, conv_new = causal_conv(proj[..., :n_qkv], conv_buf, conv_w)
    z = proj[..., n_qkv:n_qkv + H * dv].reshape(B, T, H, dv)
    b_raw = proj[..., n_qkv + H * dv:n_qkv + H * dv + H]
    a_raw = proj[..., n_qkv + H * dv + H:]
    q = l2norm(qkv[..., :H * dk].reshape(B, T, H, dk))
    k = l2norm(qkv[..., H * dk:2 * H * dk].reshape(B, T, H, dk))
    v = qkv[..., 2 * H * dk:].reshape(B, T, H, dv)
    beta = jax.nn.sigmoid(b_raw.astype(f32))
    g = -jnp.exp(a_log.astype(f32)) * jax.nn.softplus(a_raw.astype(f32) + dt_bias.astype(f32))
    o, S = _gdn_delta_rule(q.reshape(B, T, H * dk), k.reshape(B, T, H * dk), v.reshape(B, T, H * dv),
                           z.reshape(B, T, H * dv), g, beta, S0, norm_g)
    return _mm(o, w_out), conv_new, S


def _dot_split3(a, b):
    f32, bf16 = jnp.float32, jnp.bfloat16
    ah = a.astype(bf16)
    al = (a - ah.astype(f32)).astype(bf16)
    bh = b.astype(bf16)
    bl = (b - bh.astype(f32)).astype(bf16)
    dot = functools.partial(jnp.dot, preferred_element_type=f32)
    return dot(ah, bh) + dot(ah, bl) + dot(al, bh)


def _gdn_chunk_kernel(q_ref, k_ref, v_ref, z_ref, col_ref, row_ref, s0_ref, ng_ref, o_ref, sout_ref, s_scr):
    f32, bf16 = jnp.float32, jnp.bfloat16
    n = pl.program_id(1)
    C = q_ref.shape[1]
    H, dk, dv = GDN_HEADS, GDN_DK, GDN_DV
    dot = functools.partial(jnp.dot, preferred_element_type=f32)
    nt = (((1,), (1,)), ((), ()))

    @pl.when(n == 0)
    def _():
        s_scr[...] = s0_ref[0]

    ii = lax.broadcasted_iota(jnp.int32, (C, C), 0)
    jj = lax.broadcasted_iota(jnp.int32, (C, C), 1)
    incl = ii >= jj
    strict = ii > jj
    eye = (ii == jj).astype(f32)
    heads = range(H)
    ks = [slice(h * dk, (h + 1) * dk) for h in heads]
    vs = [slice(h * dv, (h + 1) * dv) for h in heads]
    gc = [col_ref[0, :, h:h + 1] for h in heads]
    bc = [col_ref[0, :, H + h:H + h + 1] for h in heads]
    gr = [row_ref[0, 0, h:h + 1, :] for h in heads]
    decay = [jnp.exp(jnp.where(incl, gc[h] - gr[h], -jnp.inf)) for h in heads]
    kcb = [k_ref[0, :, ks[h]].astype(bf16) for h in heads]
    kb = [k_ref[0, :, ks[h]] * bc[h] for h in heads]
    mpow = [-jnp.where(strict, lax.dot_general(kb[h].astype(bf16), kcb[h], nt, preferred_element_type=f32) * decay[h], 0.0)
            for h in heads]
    tinv = [eye + mpow[h] for h in heads]
    for _ in range(C.bit_length() - 2):
        mpow = [_dot_split3(mpow[h], mpow[h]) for h in heads]
        tinv = [tinv[h] + _dot_split3(tinv[h], mpow[h]) for h in heads]
    eg = [jnp.exp(gc[h]) for h in heads]
    uw = [dot(tinv[h].astype(bf16), jnp.concatenate([v_ref[0, :, vs[h]] * bc[h], kb[h] * eg[h]], axis=1).astype(bf16))
          for h in heads]
    qh = [q_ref[0, :, ks[h]] * dk ** -0.5 for h in heads]
    a_intra = [(lax.dot_general(qh[h].astype(bf16), kcb[h], nt, preferred_element_type=f32) * decay[h]).astype(bf16)
               for h in heads]
    s_b = [s_scr[h].astype(bf16) for h in heads]
    v_nb = [(uw[h][:, :dv] - dot(uw[h][:, dv:].astype(bf16), s_b[h])).astype(bf16) for h in heads]
    o = [dot((qh[h] * eg[h]).astype(bf16), s_b[h]) + dot(a_intra[h], v_nb[h]) for h in heads]
    for h in heads:
        g_last = gr[h][:, C - 1:C]
        kd = (k_ref[0, :, ks[h]] * jnp.exp(g_last - gc[h])).astype(bf16)
        s_scr[h] = s_scr[h] * jnp.exp(g_last) + lax.dot_general(kd, v_nb[h], _TN, preferred_element_type=f32)
    for h in heads:
        y = o[h] * lax.rsqrt(jnp.mean(o[h] * o[h], -1, keepdims=True) + EPS) * ng_ref[...]
        zz = z_ref[0, :, vs[h]]
        o_ref[0, :, vs[h]] = y * (zz * jax.nn.sigmoid(zz))

    @pl.when(n == pl.num_programs(1) - 1)
    def _():
        sout_ref[0] = s_scr[...]


def _gdn_delta_rule(q, k, v, z, g, beta, S0, norm_g):
    B, T, _ = q.shape
    H, dk, dv = GDN_HEADS, GDN_DK, GDN_DV
    C = GDN_CHUNK if T % GDN_CHUNK == 0 else T
    assert C & (C - 1) == 0 and C % 8 == 0
    nC = T // C
    f32 = jnp.float32
    gcum = jnp.cumsum(g.astype(f32).reshape(B, nC, C, H), axis=2)
    col = jnp.concatenate([gcum.reshape(B, T, H), beta.astype(f32)], axis=-1)
    row = jnp.swapaxes(gcum, 2, 3)
    tok = lambda w: pl.BlockSpec((1, C, w), lambda b, n: (b, n, 0))
    o, s_out = pl.pallas_call(
        _gdn_chunk_kernel,
        grid=(B, nC),
        in_specs=[tok(H * dk), tok(H * dk), tok(H * dv), tok(H * dv), tok(2 * H),
                  pl.BlockSpec((1, 1, H, C), lambda b, n: (b, n, 0, 0)),
                  pl.BlockSpec((1, H, dk, dv), lambda b, n: (b, 0, 0, 0)),
                  pl.BlockSpec((1, dv), lambda b, n: (0, 0))],
        out_specs=[tok(H * dv), pl.BlockSpec((1, H, dk, dv), lambda b, n: (b, 0, 0, 0))],
        out_shape=[jax.ShapeDtypeStruct((B, T, H * dv), f32), jax.ShapeDtypeStruct((B, H, dk, dv), f32)],
        scratch_shapes=[pltpu.VMEM((H, dk, dv), f32)],
        compiler_params=pltpu.CompilerParams(dimension_semantics=("parallel", "arbitrary"), vmem_limit_bytes=VMEM_LIMIT),
        name="gdn_delta_rule",
    )(q, k, v, z, col, row, S0.astype(f32), norm_g.reshape(1, dv))
    return o, s_out


def nsa_seq(q_s, g_s, full_s, win_s, pos0, wpos0, cmp_pos, cmp_w):
    dt = q_s.dtype
    T = q_s.shape[0]
    G, R, dh = NSA_KV_GROUPS, NSA_HEADS // NSA_KV_GROUPS, HEAD_DIM
    Tk = full_s.shape[0]
    Tkp = -(-Tk // SEL_BLOCK) * SEL_BLOCK
    full_p = jnp.pad(full_s, ((0, Tkp - Tk), (0, 0), (0, 0), (0, 0)))
    n_cmp = Tkp // CMP_BLOCK
    n_sel = Tkp // SEL_BLOCK
    blocks = full_p[:, :2].reshape(n_cmp, CMP_BLOCK, 2, G, dh) + jnp.transpose(cmp_pos, (1, 0, 2))[:, :, None, :]
    kv_cmp = jnp.einsum('nlsgd,slde->nsge', blocks, cmp_w)
    k_cmp, v_cmp = kv_cmp[:, 0], kv_cmp[:, 1]
    cmp_end = (jnp.arange(n_cmp) + 1) * CMP_BLOCK - 1
    sel = jnp.transpose(full_p[:, 2:].reshape(n_sel, SEL_BLOCK, 2, G, dh), (2, 3, 0, 1, 4))
    k_selb, v_selb = sel[0], sel[1]
    n_top = min(TOP_N, n_sel)
    win_p = jnp.pad(win_s, ((WINDOW, 0), (0, 0), (0, 0), (0, 0)))
    qb = Q_BLOCK if T % Q_BLOCK == 0 else T
    nqb = T // qb
    lw = WINDOW + qb - 1
    scale = HEAD_DIM ** -0.5
    g_idx = jnp.arange(G)[None, :, None]
    blk_ids = jnp.arange(n_sel)

    def block(args):
        qi, gi, bi = args
        qstart = pos0 + bi * qb
        qpos = qstart + jnp.arange(qb)
        qg = qi.reshape(qb, G, R, dh) * scale
        s_c = jnp.einsum('qgrd,ngd->qgrn', qg, k_cmp)
        p_c = masked_softmax(s_c, (cmp_end[None, :] <= qpos[:, None])[:, None, None, :])
        o_c = jnp.einsum('qgrn,ngd->qgrd', p_c.astype(dt), v_cmp)
        imp = p_c.sum(2).reshape(qb, G, n_sel, SEL_BLOCK // CMP_BLOCK).sum(-1)
        cur = (qpos // SEL_BLOCK)[:, None] == blk_ids[None, :]
        causal_blk = (blk_ids * SEL_BLOCK)[None, :] <= qpos[:, None]
        imp = jnp.where(cur[:, None], jnp.inf, jnp.where(causal_blk[:, None], imp, -jnp.inf))
        top_v, top_i = lax.top_k(imp, n_top)
        k_sel = k_selb[g_idx, top_i]
        v_sel = v_selb[g_idx, top_i]
        kpos = top_i[..., None] * SEL_BLOCK + jnp.arange(SEL_BLOCK)
        m_s = (kpos <= qpos[:, None, None, None]) & (top_v > -jnp.inf)[..., None]
        s_s = jnp.einsum('qgrd,qgksd->qgrks', qg, k_sel).reshape(qb, G, R, n_top * SEL_BLOCK)
        p_s = masked_softmax(s_s, m_s.reshape(qb, G, 1, n_top * SEL_BLOCK))
        o_s = jnp.einsum('qgrm,qgmd->qgrd', p_s.astype(dt), v_sel.reshape(qb, G, n_top * SEL_BLOCK, dh))
        wblk = lax.dynamic_slice_in_dim(win_p, qstart - wpos0 + 1, lw, axis=0)
        wpos = qstart - WINDOW + 1 + jnp.arange(lw)
        m_w = (wpos[None] <= qpos[:, None]) & (wpos[None] > qpos[:, None] - WINDOW) & (wpos[None] >= wpos0)
        s_w = jnp.einsum('qgrd,kgd->qgrk', qg, wblk[:, 0])
        p_w = masked_softmax(s_w, m_w[:, None, None, :])
        o_w = jnp.einsum('qgrk,kgd->qgrd', p_w.astype(dt), wblk[:, 1])
        gg = gi.reshape(qb, G, R, 3)
        o = gg[..., 0:1] * o_c + gg[..., 1:2] * o_s + gg[..., 2:3] * o_w
        return o.reshape(qb, NSA_HEADS, dh)

    out = lax.map(block, (q_s.reshape(nqb, qb, NSA_HEADS, dh), g_s.reshape(nqb, qb, NSA_HEADS, 3), jnp.arange(nqb)))
    return out.reshape(T, NSA_HEADS, dh)


NSA_R = NSA_HEADS // NSA_KV_GROUPS
NSA_TK = 512
NSA_WSPAN = WINDOW + Q_BLOCK


def _softmax_rows(s, mask):
    s = jnp.where(mask, s, -jnp.inf)
    m = jnp.max(s, axis=0, keepdims=True)
    m = jnp.where(m == -jnp.inf, 0.0, m)
    p = jnp.exp(s - m)
    return p * (1.0 / jnp.maximum(jnp.sum(p, axis=0, keepdims=True), 1e-30))


def _nsa_cmp_kernel(x_ref, pos_ref, w_ref, o_ref):
    xb = (x_ref[0, 0, 0, 0] + pos_ref[0]).astype(jnp.bfloat16)
    o_ref[0, 0, 0, 0] = jnp.dot(xb, w_ref[0], preferred_element_type=jnp.float32)


def _nsa_compress(rows_cmp, cmp_pos, cmp_w):
    B, T, _, G, dh = rows_cmp.shape
    n_sel = T // SEL_BLOCK
    ld = CMP_BLOCK * dh
    x = rows_cmp.reshape(B, n_sel, 2, CMP_BLOCK, 2, G, dh)
    x = jnp.transpose(x, (0, 4, 5, 2, 1, 3, 6)).reshape(B, 2, G, 2, n_sel, ld)
    return pl.pallas_call(
        _nsa_cmp_kernel,
        grid=(B, 2, G, 2),
        in_specs=[pl.BlockSpec((1, 1, 1, 1, n_sel, ld), lambda b, s, g, p: (b, s, g, p, 0, 0)),
                  pl.BlockSpec((1, 1, ld), lambda b, s, g, p: (s, 0, 0)),
                  pl.BlockSpec((1, ld, dh), lambda b, s, g, p: (s, 0, 0))],
        out_specs=pl.BlockSpec((1, 1, 1, 1, n_sel, dh), lambda b, s, g, p: (b, s, g, p, 0, 0)),
        out_shape=jax.ShapeDtypeStruct((B, 2, G, 2, n_sel, dh), jnp.float32),
        compiler_params=pltpu.CompilerParams(dimension_semantics=("parallel",) * 4, vmem_limit_bytes=VMEM_LIMIT),
        name="nsa_compress",
    )(x, cmp_pos.reshape(2, 1, ld), cmp_w.reshape(2, ld, dh).astype(jnp.bfloat16))


def _nsa_prompt_kernel(q_ref, gt_ref, kc_ref, vct_ref, ks_ref, vst_ref, kw_ref, vwt_ref, o_ref,
                       sel_scr, m_scr, l_scr, acc_scr):
    f32, bf16 = jnp.float32, jnp.bfloat16
    i = pl.program_id(2)
    R, QB = NSA_R, Q_BLOCK
    W = R * QB
    T = ks_ref.shape[2]
    n_sel = T // SEL_BLOCK
    nt = (((1,), (1,)), ((), ()))
    qb = q_ref[0]
    q = jnp.concatenate([qb[:, r * HEAD_DIM:(r + 1) * HEAD_DIM] for r in range(R)], axis=0)
    q = (q * HEAD_DIM ** -0.5).astype(bf16)
    lane = lax.broadcasted_iota(jnp.int32, (1, W), 1)
    qpos = i * QB + (lane & (QB - 1))

    s = lax.dot_general(kc_ref[0, 0], q, nt, preferred_element_type=f32)
    row = lax.broadcasted_iota(jnp.int32, (2 * n_sel, 1), 0)
    cidx = jnp.where(row < n_sel, 2 * row, 2 * (row - n_sel) + 1)
    p_c = _softmax_rows(s, (cidx + 1) * CMP_BLOCK - 1 <= qpos)
    o_c = jnp.dot(vct_ref[0, 0], p_c.astype(bf16), preferred_element_type=f32)

    p_r = p_c[:, 0:QB]
    for r in range(1, R):
        p_r = p_r + p_c[:, r * QB:(r + 1) * QB]
    imp = p_r[:n_sel] + p_r[n_sel:]
    blk = lax.broadcasted_iota(jnp.int32, (n_sel, QB), 0)
    qp = qpos[:, :QB]
    imp = jnp.where(blk == qp // SEL_BLOCK, jnp.inf, jnp.where(blk * SEL_BLOCK <= qp, imp, -jnp.inf))
    chosen = [jnp.zeros((n_sel, QB), f32)]

    def pick(r, m, sel, chosen=chosen):
        chosen[0] = jnp.where(sel & (m > -jnp.inf), 1.0, chosen[0])

    _extract_top(imp, min(TOP_N, n_sel), pick)
    sel_scr[...] = jnp.concatenate([chosen[0]] * R, axis=1)

    m_scr[...] = jnp.full(m_scr.shape, -jnp.inf, f32)
    l_scr[...] = jnp.zeros(l_scr.shape, f32)
    acc_scr[...] = jnp.zeros(acc_scr.shape, f32)
    blk_per_tile = NSA_TK // SEL_BLOCK

    def tile_body(kt, carry):
        k0 = pl.multiple_of(kt * NSA_TK, NSA_TK)
        s = lax.dot_general(ks_ref[0, 0, pl.ds(k0, NSA_TK), :], q, nt, preferred_element_type=f32)
        selrows = sel_scr[pl.ds(pl.multiple_of(kt * blk_per_tile, blk_per_tile), blk_per_tile), :]
        kpos = k0 + lax.broadcasted_iota(jnp.int32, (NSA_TK, 1), 0)
        parts = []
        for j in range(blk_per_tile):
            rows = slice(j * SEL_BLOCK, (j + 1) * SEL_BLOCK)
            ok = (selrows[j:j + 1, :] > 0.0) & (kpos[rows] <= qpos)
            parts.append(jnp.where(ok, s[rows], -jnp.inf))
        _online_softmax_tile(jnp.concatenate(parts, axis=0), vst_ref[0, 0, :, pl.ds(k0, NSA_TK)], m_scr, l_scr, acc_scr)
        return carry

    lax.fori_loop(0, (i * QB + QB + NSA_TK - 1) // NSA_TK, tile_body, 0)
    o_s = acc_scr[...] * (1.0 / jnp.maximum(l_scr[...], 1e-30))

    w0 = pl.multiple_of(jnp.clip(i * QB + QB - NSA_WSPAN, 0, T - NSA_WSPAN), QB)
    s = lax.dot_general(kw_ref[0, 0, pl.ds(w0, NSA_WSPAN), :], q, nt, preferred_element_type=f32)
    kpos = w0 + lax.broadcasted_iota(jnp.int32, (NSA_WSPAN, 1), 0)
    p_w = _softmax_rows(s, (kpos <= qpos) & (kpos > qpos - WINDOW))
    o_w = jnp.dot(vwt_ref[0, 0, :, pl.ds(w0, NSA_WSPAN)], p_w.astype(bf16), preferred_element_type=f32)

    g = gt_ref[0, 0, 0]
    o = g[0:1] * o_c + g[1:2] * o_s + g[2:3] * o_w
    for r in range(R):
        o_ref[0, :, r * HEAD_DIM:(r + 1) * HEAD_DIM] = o[:, r * QB:(r + 1) * QB].T


def _nsa_prompt_attention(q, gates, new_rows, win, cmp_pos, cmp_w):
    B, T, _ = q.shape
    G, R, dh, QB = NSA_KV_GROUPS, NSA_R, HEAD_DIM, Q_BLOCK
    assert T % NSA_TK == 0 and T >= NSA_WSPAN and (T // SEL_BLOCK) % 8 == 0
    nqb = T // QB
    n_sel = T // SEL_BLOCK
    W = R * QB
    bf16 = jnp.bfloat16
    cmp = _nsa_compress(new_rows[:, :, :2], cmp_pos, cmp_w).reshape(B, 2, G, 2 * n_sel, dh)
    kc = cmp[:, 0].astype(bf16)
    vct = jnp.swapaxes(cmp[:, 1], -1, -2).astype(bf16)
    ks = jnp.transpose(new_rows[:, :, 2], (0, 2, 1, 3)).astype(bf16)
    vst = jnp.transpose(new_rows[:, :, 3], (0, 2, 3, 1)).astype(bf16)
    kw = jnp.transpose(win[:, :, 0], (0, 2, 1, 3)).astype(bf16)
    vwt = jnp.transpose(win[:, :, 1], (0, 2, 3, 1)).astype(bf16)
    gt = jnp.transpose(gates.reshape(B, nqb, QB, G, R, 3), (0, 3, 1, 5, 4, 2)).reshape(B, G, nqb, 3, W)
    gt = jnp.pad(gt, ((0, 0), (0, 0), (0, 0), (0, 5), (0, 0)))
    per_bg = lambda *shape: pl.BlockSpec((1, 1) + shape, lambda b, g, i: (b, g, 0, 0))
    return pl.pallas_call(
        _nsa_prompt_kernel,
        grid=(B, G, nqb),
        in_specs=[pl.BlockSpec((1, QB, R * dh), lambda b, g, i: (b, i, g)),
                  pl.BlockSpec((1, 1, 1, 8, W), lambda b, g, i: (b, g, i, 0, 0)),
                  per_bg(2 * n_sel, dh), per_bg(dh, 2 * n_sel), per_bg(T, dh), per_bg(dh, T), per_bg(T, dh), per_bg(dh, T)],
        out_specs=pl.BlockSpec((1, QB, R * dh), lambda b, g, i: (b, i, g)),
        out_shape=jax.ShapeDtypeStruct(q.shape, jnp.float32),
        scratch_shapes=[pltpu.VMEM((n_sel, W), jnp.float32), pltpu.VMEM((1, W), jnp.float32),
                        pltpu.VMEM((1, W), jnp.float32), pltpu.VMEM((dh, W), jnp.float32)],
        compiler_params=pltpu.CompilerParams(dimension_semantics=("parallel", "parallel", "arbitrary"),
                                             vmem_limit_bytes=VMEM_LIMIT),
        name="nsa_prompt",
    )(q, gt, kc, vct, ks, vst, kw, vwt)


_TN = (((0,), (0,)), ((), ()))
_NT = (((1,), (1,)), ((), ()))


def _nsa_sample_kernel(pt_ref, xc_ref, *refs, n_pages, past_len):
    pages = refs[:n_pages]
    q_ref, rows_ref, wnew_ref, wbuf_ref, gt_ref, pos_ref, w_ref, o_ref, sel_scr = refs[n_pages:]
    f32, bf16 = jnp.float32, jnp.bfloat16
    G, R, dh = NSA_KV_GROUPS, NSA_R, HEAD_DIM
    T = q_ref.shape[1]
    n_blk = past_len // SEL_BLOCK
    n_w = wbuf_ref.shape[1]
    lane = lax.broadcasted_iota(jnp.int32, (1, LANE), 1)
    qidx = lane & (T - 1)
    qpos = past_len + qidx
    new_row = lax.broadcasted_iota(jnp.int32, (LANE, 1), 0)
    new_ok = (new_row < T) & (new_row <= qidx)

    def new_tile(ref, col):
        return jnp.concatenate([ref[0, :, col * dh:(col + 1) * dh], jnp.zeros((LANE - T, dh), f32)], axis=0).astype(bf16)

    cmp = []
    for s in range(2):
        xs = jnp.concatenate(
            [jnp.concatenate([xc_ref[0, :, l * 4 * dh + (s * G + g) * dh:l * 4 * dh + (s * G + g + 1) * dh]
                              for l in range(CMP_BLOCK)], axis=1) for g in range(G)], axis=0)
        cmp.append(jnp.dot((xs + pos_ref[s]).astype(bf16), w_ref[s], preferred_element_type=f32))
    for g in range(G):
        q = jnp.concatenate([q_ref[0, :, (g * R + r) * dh:(g * R + r + 1) * dh] for r in range(R)]
                            + [jnp.zeros((LANE - R * T, dh), f32)], axis=0)
        q = (q * dh ** -0.5).astype(bf16)
        kc = cmp[0][g * 2 * n_blk:(g + 1) * 2 * n_blk]
        vc = cmp[1][g * 2 * n_blk:(g + 1) * 2 * n_blk]
        s_c = lax.dot_general(kc.astype(bf16), q, _NT, preferred_element_type=f32)
        p_c = _softmax_rows(s_c, jnp.full(s_c.shape, True))
        o_c = lax.dot_general(vc.astype(bf16), p_c.astype(bf16), _TN, preferred_element_type=f32)
        p_r = p_c
        for r in range(1, R):
            p_r = p_r + pltpu.roll(p_c, LANE - r * T, axis=1)
        imp = p_r[:n_blk] + p_r[n_blk:]
        chosen = [jnp.zeros((n_blk, LANE), f32)]

        def pick(_, m, sel, chosen=chosen):
            chosen[0] = jnp.where(sel, 1.0, chosen[0])

        _extract_top(imp, min(TOP_N, n_blk + 1) - 1, pick)
        ch = jnp.where(lane < T, chosen[0], 0.0)
        ch4 = ch
        for r in range(1, R):
            ch4 = ch4 + pltpu.roll(ch, r * T, axis=1)
        sel_scr[...] = ch4
        page = pages[0].shape[1] // (4 * G)
        ks = jnp.concatenate([pg[0, pl.ds(2 * G + g, page, stride=4 * G), :] for pg in pages], axis=0).astype(bf16)
        vs = jnp.concatenate([pg[0, pl.ds(3 * G + g, page, stride=4 * G), :] for pg in pages], axis=0).astype(bf16)
        s_p = lax.dot_general(ks, q, _NT, preferred_element_type=f32)
        s_p = jnp.concatenate([jnp.where(sel_scr[j:j + 1, :] > 0.0, s_p[j * SEL_BLOCK:(j + 1) * SEL_BLOCK], -jnp.inf)
                               for j in range(n_blk)], axis=0)
        s_n = lax.dot_general(new_tile(rows_ref, 2 * G + g), q, _NT, preferred_element_type=f32)
        s_all = jnp.concatenate([s_p, jnp.where(new_ok, s_n, -jnp.inf)], axis=0)
        p_s = _softmax_rows(s_all, s_all > -jnp.inf).astype(bf16)
        o_s = (lax.dot_general(vs, p_s[:past_len], _TN, preferred_element_type=f32)
               + lax.dot_general(new_tile(rows_ref, 3 * G + g), p_s[past_len:], _TN, preferred_element_type=f32))
        kw = wbuf_ref[0, :, g * dh:(g + 1) * dh].astype(bf16)
        vw = wbuf_ref[0, :, (G + g) * dh:(G + g + 1) * dh].astype(bf16)
        s_w = lax.dot_general(kw, q, _NT, preferred_element_type=f32)
        wpos = past_len - n_w + lax.broadcasted_iota(jnp.int32, (n_w, 1), 0)
        s_w = jnp.where(wpos > qpos - WINDOW, s_w, -jnp.inf)
        s_n = lax.dot_general(new_tile(wnew_ref, g), q, _NT, preferred_element_type=f32)
        s_all = jnp.concatenate([s_w, jnp.where(new_ok, s_n, -jnp.inf)], axis=0)
        p_w = _softmax_rows(s_all, s_all > -jnp.inf).astype(bf16)
        o_w = (lax.dot_general(vw, p_w[:n_w], _TN, preferred_element_type=f32)
               + lax.dot_general(new_tile(wnew_ref, G + g), p_w[n_w:], _TN, preferred_element_type=f32))
        gt = gt_ref[0, g]
        o = (gt[0:1] * o_c + gt[1:2] * o_s + gt[2:3] * o_w).T
        for r in range(R):
            o_ref[0, :, (g * R + r) * dh:(g * R + r + 1) * dh] = o[r * T:(r + 1) * T]


def _nsa_sample_attention(q, gates, new_rows, win_new, cache, page_table, wbuf, cmp_pos, cmp_w):
    B, T, _ = q.shape
    G, R, dh = NSA_KV_GROUPS, NSA_R, HEAD_DIM
    n_pool, page = cache.shape[:2]
    n_pages = page_table.shape[1]
    past_len = n_pages * page
    n_w = wbuf.shape[1]
    assert T == 8 and R * T <= LANE and page % SEL_BLOCK == 0 and n_w + T > WINDOW
    assert past_len % SEL_BLOCK == 0 and T <= CMP_BLOCK
    row_w = 4 * G * dh
    per_page = page // SEL_BLOCK
    f32 = jnp.float32
    cache2 = cache.reshape(n_pool, page * 4 * G, dh)
    xc = cache2[page_table].reshape(B, n_pages, per_page, 2, CMP_BLOCK, 4 * G, dh)[..., :2 * G, :]
    xc = jnp.transpose(xc, (0, 3, 1, 2, 4, 5, 6)).reshape(B, 2 * n_pages * per_page, CMP_BLOCK * 2 * G * dh)
    n_blk = past_len // SEL_BLOCK
    gt = jnp.transpose(gates.reshape(B, T, G, R, 3), (0, 2, 4, 3, 1)).reshape(B, G, 3, R * T)
    gt = jnp.pad(gt, ((0, 0), (0, 0), (0, 5), (0, LANE - R * T)))
    page_spec = lambda p: pl.BlockSpec((1, page * 4 * G, dh), lambda b, pt: (pt[b, p], 0, 0))
    seq = lambda *shape: pl.BlockSpec((1,) + shape, lambda b, pt: (b,) + (0,) * len(shape))
    whole = lambda *shape: pl.BlockSpec(shape, lambda b, pt: (0,) * len(shape))
    grid_spec = pltpu.PrefetchScalarGridSpec(
        num_scalar_prefetch=1,
        grid=(B,),
        in_specs=[seq(2 * n_blk, CMP_BLOCK * 2 * G * dh)] + [page_spec(p) for p in range(n_pages)]
                 + [seq(T, R * G * dh), seq(T, row_w), seq(T, 2 * G * dh), seq(n_w, 2 * G * dh), seq(G, 8, LANE),
                    whole(2, 1, CMP_BLOCK * dh), whole(2, CMP_BLOCK * dh, dh)],
        out_specs=seq(T, R * G * dh),
        scratch_shapes=[pltpu.VMEM((n_blk, LANE), f32)],
    )
    return pl.pallas_call(
        functools.partial(_nsa_sample_kernel, n_pages=n_pages, past_len=past_len),
        grid_spec=grid_spec,
        out_shape=jax.ShapeDtypeStruct(q.shape, f32),
        compiler_params=pltpu.CompilerParams(dimension_semantics=("parallel",), vmem_limit_bytes=VMEM_LIMIT),
        name="nsa_sample",
    )(page_table, xc, *([cache2] * n_pages), q, new_rows.reshape(B, T, row_w), win_new.reshape(B, T, 2 * G * dh),
      wbuf.reshape(B, n_w, 2 * G * dh), gt, cmp_pos.reshape(2, 1, CMP_BLOCK * dh),
      cmp_w.reshape(2, CMP_BLOCK * dh, dh).astype(jnp.bfloat16))


def nsa_mixer(h, pos0, past, wbuf, w_in, cmp_pos, cmp_w, w_out):
    B, T, _ = h.shape
    G, dh = NSA_KV_GROUPS, HEAD_DIM
    qd = NSA_HEADS * dh
    kvd = 6 * G * dh
    pos = pos0 + jnp.arange(T, dtype=jnp.int32)
    proj = _mm(h, w_in)
    q = rope(proj[..., :qd].reshape(B, T, NSA_HEADS, dh), pos)
    kv = proj[..., qd:qd + kvd].reshape(B, T, 6, G, dh)
    gates = jax.nn.sigmoid(proj[..., qd + kvd:].reshape(B, T, NSA_HEADS, 3))
    keys = rope(kv[:, :, 0::2].reshape(B, T, 3 * G, dh), pos).reshape(B, T, 3, G, dh)
    kv = jnp.stack([keys[:, :, 0], kv[:, :, 1], keys[:, :, 1], kv[:, :, 3], keys[:, :, 2], kv[:, :, 5]], axis=2)
    new_rows = kv[:, :, :4]
    win = kv[:, :, 4:] if past is None else jnp.concatenate([wbuf, kv[:, :, 4:]], 1)
    if past is None:
        o = _nsa_prompt_attention(q.reshape(B, T, qd), gates, new_rows, win, cmp_pos, cmp_w)
    else:
        cache, page_table = past
        o = _nsa_sample_attention(q.reshape(B, T, qd), gates, new_rows, kv[:, :, 4:], cache, page_table, wbuf,
                                  cmp_pos, cmp_w)
    new_win = win[:, -min(WINDOW, win.shape[1]):]
    return _mm(o.reshape(B, T, D_MODEL), w_out), new_rows, new_win


DIFF_QB = 256
DIFF_TK = 512


def _online_softmax_tile(s, vt_tile, m_scr, l_scr, acc_scr):
    m_old = m_scr[...]
    m_new = jnp.maximum(m_old, jnp.max(s, axis=0, keepdims=True))
    m_safe = jnp.where(m_new == -jnp.inf, 0.0, m_new)
    alpha = jnp.exp(m_old - m_safe)
    p = jnp.exp(s - m_safe)
    l_scr[...] = alpha * l_scr[...] + jnp.sum(p, axis=0, keepdims=True)
    acc_scr[...] = alpha * acc_scr[...] + jnp.dot(vt_tile, p.astype(jnp.bfloat16), preferred_element_type=jnp.float32)
    m_scr[...] = m_new


def _diff_prompt_kernel(lam_ref, q_ref, k_ref, vt_ref, g_ref, o_ref, m_scr, l_scr, acc_scr, *, out_scale):
    f32, bf16 = jnp.float32, jnp.bfloat16
    i = pl.program_id(2)
    QB, TK, dd = DIFF_QB, DIFF_TK, DIFF_DH
    W = 2 * QB
    nt = (((1,), (1,)), ((), ()))
    qb = q_ref[0] * dd ** -0.5
    col = lax.broadcasted_iota(jnp.int32, (1, 2 * dd), 1)
    q = jnp.concatenate([jnp.where(col < dd, qb, 0.0), jnp.where(col >= dd, qb, 0.0)], axis=0).astype(bf16)
    lane = lax.broadcasted_iota(jnp.int32, (1, W), 1)
    qpos = i * QB + (lane & (QB - 1))
    m_scr[...] = jnp.full(m_scr.shape, -jnp.inf, f32)
    l_scr[...] = jnp.zeros(l_scr.shape, f32)
    acc_scr[...] = jnp.zeros(acc_scr.shape, f32)
    n_full = (i * QB) // TK

    def full_tile(kt, carry):
        k0 = pl.multiple_of(kt * TK, TK)
        s = lax.dot_general(k_ref[0, pl.ds(k0, TK), :], q, nt, preferred_element_type=f32)
        _online_softmax_tile(s, vt_ref[0, 0, :, pl.ds(k0, TK)], m_scr, l_scr, acc_scr)
        return carry

    lax.fori_loop(0, n_full, full_tile, 0)
    k0 = pl.multiple_of(n_full * TK, TK)
    s = lax.dot_general(k_ref[0, pl.ds(k0, TK), :], q, nt, preferred_element_type=f32)
    kpos = k0 + lax.broadcasted_iota(jnp.int32, (TK, 1), 0)
    _online_softmax_tile(jnp.where(kpos <= qpos, s, -jnp.inf), vt_ref[0, 0, :, pl.ds(k0, TK)], m_scr, l_scr, acc_scr)
    o = acc_scr[...] * (1.0 / jnp.maximum(l_scr[...], 1e-30))
    o = o[:, :QB] - lam_ref[0, 0] * o[:, QB:]
    o = o * lax.rsqrt(jnp.mean(o * o, axis=0, keepdims=True) + EPS) * (g_ref[...] * out_scale)
    o_ref[0] = o.T


def _diff_prompt_attention(q, k, v, lam, subln_g, out_scale):
    B, T, _ = q.shape
    H, dv = DIFF_HEADS, 2 * DIFF_DH
    assert T % DIFF_TK == 0 and DIFF_TK % DIFF_QB == 0
    bf16 = jnp.bfloat16
    vt = jnp.transpose(v.reshape(B, T, H, dv), (0, 2, 3, 1)).astype(bf16)
    W = 2 * DIFF_QB
    return pl.pallas_call(
        functools.partial(_diff_prompt_kernel, out_scale=out_scale),
        grid=(B, H, T // DIFF_QB),
        in_specs=[pl.BlockSpec(memory_space=pltpu.SMEM),
                  pl.BlockSpec((1, DIFF_QB, dv), lambda b, h, i: (b, i, h)),
                  pl.BlockSpec((1, T, dv), lambda b, h, i: (b, 0, h)),
                  pl.BlockSpec((1, 1, dv, T), lambda b, h, i: (b, h, 0, 0)),
                  pl.BlockSpec((dv, 1), lambda b, h, i: (0, 0))],
        out_specs=pl.BlockSpec((1, DIFF_QB, dv), lambda b, h, i: (b, i, h)),
        out_shape=jax.ShapeDtypeStruct(q.shape, jnp.float32),
        scratch_shapes=[pltpu.VMEM((1, W), jnp.float32), pltpu.VMEM((1, W), jnp.float32), pltpu.VMEM((dv, W), jnp.float32)],
        compiler_params=pltpu.CompilerParams(dimension_semantics=("parallel", "parallel", "arbitrary"),
                                             vmem_limit_bytes=VMEM_LIMIT),
        name="diff_prompt",
    )(lam.reshape(1, 1), q, k.astype(bf16), vt, subln_g.reshape(dv, 1))


def _diff_sample_kernel(pt_ref, lam_ref, *refs, n_pages, out_scale):
    pages = refs[:n_pages]
    q_ref, new_ref, g_ref, o_ref = refs[n_pages:]
    f32, bf16 = jnp.float32, jnp.bfloat16
    H, dd = DIFF_HEADS, DIFF_DH
    T = q_ref.shape[1]
    D = H * 2 * dd
    qt = jnp.concatenate([q_ref[0] * dd ** -0.5] * (LANE // T), axis=0)
    rowi = lax.broadcasted_iota(jnp.int32, (LANE, D), 0)
    coli = lax.broadcasted_iota(jnp.int32, (LANE, D), 1)
    qbd = jnp.where(coli // dd == rowi // T, qt, 0.0).astype(bf16)
    lane = lax.broadcasted_iota(jnp.int32, (1, LANE), 1)
    qidx = lane & (T - 1)
    s = [lax.dot_general(pg[0, :, 0:D].astype(bf16), qbd, _NT, preferred_element_type=f32) for pg in pages]
    new_row = lax.broadcasted_iota(jnp.int32, (LANE, 1), 0)
    pad = jnp.zeros((LANE - T, D), f32)
    k_new = jnp.concatenate([new_ref[0, :, 0:D], pad], axis=0).astype(bf16)
    v_new = jnp.concatenate([new_ref[0, :, D:2 * D], pad], axis=0).astype(bf16)
    s_n = lax.dot_general(k_new, qbd, _NT, preferred_element_type=f32)
    s_n = jnp.where((new_row < T) & (new_row <= qidx), s_n, -jnp.inf)
    m = jnp.max(s_n, axis=0, keepdims=True)
    for sp in s:
        m = jnp.maximum(m, jnp.max(sp, axis=0, keepdims=True))
    p_n = jnp.exp(s_n - m)
    l = jnp.sum(p_n, axis=0, keepdims=True)
    acc = lax.dot_general(v_new, p_n.astype(bf16), _TN, preferred_element_type=f32)
    for pg, sp in zip(pages, s):
        p = jnp.exp(sp - m)
        l = l + jnp.sum(p, axis=0, keepdims=True)
        acc = acc + lax.dot_general(pg[0, :, D:2 * D].astype(bf16), p.astype(bf16), _TN, preferred_element_type=f32)
    inv_l = 1.0 / jnp.maximum(l, 1e-30)
    lam = lam_ref[0, 0]
    for h in range(H):
        a = acc[h * 2 * dd:(h + 1) * 2 * dd, :] * inv_l
        d = a - lam * pltpu.roll(a, LANE - T, axis=1)
        if h:
            d = pltpu.roll(d, LANE - h * 2 * T, axis=1)
        d = d * lax.rsqrt(jnp.mean(d * d, axis=0, keepdims=True) + EPS) * (g_ref[...] * out_scale)
        o_ref[0, :, h * 2 * dd:(h + 1) * 2 * dd] = d.T[0:T]


def _diff_sample_attention(q, new_rows, cache, page_table, lam, subln_g, out_scale):
    B, T, D = q.shape
    n_pool, page = cache.shape[:2]
    n_pages = page_table.shape[1]
    assert T == 8 and 2 * DIFF_HEADS * T == LANE
    f32 = jnp.float32
    seq = lambda *shape: pl.BlockSpec((1,) + shape, lambda b, pt: (b,) + (0,) * len(shape))
    page_spec = lambda p: pl.BlockSpec((1, page, 2 * D), lambda b, pt: (pt[b, p], 0, 0))
    grid_spec = pltpu.PrefetchScalarGridSpec(
        num_scalar_prefetch=1,
        grid=(B,),
        in_specs=[pl.BlockSpec(memory_space=pltpu.SMEM)] + [page_spec(p) for p in range(n_pages)]
                 + [seq(T, D), seq(T, 2 * D), pl.BlockSpec((2 * DIFF_DH, 1), lambda b, pt: (0, 0))],
        out_specs=seq(T, D),
    )
    return pl.pallas_call(
        functools.partial(_diff_sample_kernel, n_pages=n_pages, out_scale=out_scale),
        grid_spec=grid_spec,
        out_shape=jax.ShapeDtypeStruct(q.shape, f32),
        compiler_params=pltpu.CompilerParams(dimension_semantics=("parallel",), vmem_limit_bytes=56 * 1024 * 1024),
        name="diff_sample",
    )(page_table, lam.reshape(1, 1), *([cache.astype(jnp.bfloat16).reshape(n_pool, page, 2 * D)] * n_pages), q,
      new_rows.reshape(B, T, 2 * D), subln_g.reshape(2 * DIFF_DH, 1))


def diff_mixer(h, pos0, past, layer_idx, w_in, lq1, lk1, lq2, lk2, subln_g, w_out):
    B, T, _ = h.shape
    H, dd = DIFF_HEADS, DIFF_DH
    f32 = jnp.float32
    dt = h.dtype
    pos = pos0 + jnp.arange(T, dtype=jnp.int32)
    proj = _mm(h, w_in)
    q = rope(proj[..., :D_MODEL].reshape(B, T, 2 * H, dd), pos)
    k = rope(proj[..., D_MODEL:2 * D_MODEL].reshape(B, T, 2 * H, dd), pos)
    v = proj[..., 2 * D_MODEL:].reshape(B, T, 2 * H, dd)
    new_rows = jnp.stack([k, v], 2)
    lam_init = 0.8 - 0.6 * math.exp(-0.3 * layer_idx)
    lam = (jnp.exp(jnp.sum(lq1.astype(f32) * lk1.astype(f32))) - jnp.exp(jnp.sum(lq2.astype(f32) * lk2.astype(f32))) + lam_init)
    if past is None:
        o = _diff_prompt_attention(q.reshape(B, T, D_MODEL), k.reshape(B, T, D_MODEL), v.reshape(B, T, D_MODEL),
                                   lam, subln_g, 1.0 - lam_init)
        return _mm(o, w_out), new_rows
    cache, page_table = past
    o = _diff_sample_attention(q.reshape(B, T, D_MODEL), new_rows, cache, page_table, lam, subln_g, 1.0 - lam_init)
    return _mm(o, w_out), new_rows


PEER_TB = 512
PEER_ET = 1024
_PEER_CAND = [(a, b) for a in range(PEER_TOPK) for b in range(PEER_TOPK) if (a + 1) * (b + 1) <= PEER_TOPK]


def _extract_top(s, n_iter, on_pick, break_ties=True):
    rows = s.shape[0]
    iota = lax.broadcasted_iota(jnp.int32, s.shape, 0)
    for r in range(n_iter):
        m = jnp.max(s, axis=0, keepdims=True)
        if break_ties:
            idx = jnp.min(jnp.where(s == m, iota, rows), axis=0, keepdims=True)
            sel = iota == idx
        else:
            sel = s == m
        on_pick(r, m, sel)
        s = jnp.where(sel, -jnp.inf, s)


def _peer_route_kernel(x_ref, shift_ref, scale_ref, g_ref, wq_ref, k1_ref, k2_ref, aof_ref,
                       h_ref, c1_ref, cnt1_ref, rank2_ref, e2_ref, q_scr, v1_scr, v2_scr, cand_scr):
    f32 = jnp.float32
    x = x_ref[...]
    tb = x.shape[0] * x.shape[1]
    y = x * lax.rsqrt(jnp.mean(x * x, -1, keepdims=True) + EPS) * g_ref[...]
    h = (y * (1.0 + scale_ref[...]) + shift_ref[...]).reshape(tb, D_MODEL)
    hb = h.astype(jnp.bfloat16)
    h_ref[...] = hb
    q = jnp.dot(hb, wq_ref[...], preferred_element_type=f32).astype(jnp.bfloat16)
    for hh in range(PEER_HEADS):
        q_scr[hh] = q[:, hh * PEER_QDIM:(hh + 1) * PEER_QDIM]
    nt = (((1,), (1,)), ((), ()))
    n_chunk = tb // LANE
    cand_scr[...] = jnp.full(cand_scr.shape, -jnp.inf, f32)

    def process(it, exact):
        hh = it // n_chunk
        c0 = pl.multiple_of((it % n_chunk) * LANE, LANE)
        qc = q_scr[hh, pl.ds(c0, LANE), :]
        s1 = lax.dot_general(k1_ref[hh], qc, nt, preferred_element_type=f32)
        s2 = lax.dot_general(k2_ref[hh], qc, nt, preferred_element_type=f32)
        ranks = []
        for s, v_scr in ((s1, v1_scr), (s2, v2_scr)):
            rank = [jnp.full(s.shape, float(N_KEYS), f32)]

            def pick(r, m, sel, v_scr=v_scr, rank=rank):
                v_scr[r:r + 1, :] = m
                rank[0] = jnp.where(sel, float(r), rank[0])

            _extract_top(s, PEER_TOPK, pick, break_ties=exact)
            ranks.append(rank[0])
        for k, (a, b) in enumerate(_PEER_CAND):
            cand_scr[k:k + 1, :] = v1_scr[a:a + 1, :] + v2_scr[b:b + 1, :]
        top1 = v1_scr[0:1, :]
        top2 = v2_scr[0:1, :]
        top_val = top1 + top2
        a_of = aof_ref[...]
        iota16 = lax.broadcasted_iota(jnp.int32, (PEER_TOPK, LANE), 0)
        st = {"z": jnp.zeros((1, LANE), f32), "cnt": jnp.zeros((PEER_TOPK, LANE), f32), "taken": jnp.zeros((1, LANE), f32)}

        def pick_c(r, m, sel, st=st):
            st["z"] = st["z"] + jnp.exp(m - top_val)
            a_sel = jnp.max(jnp.where(sel, a_of, 0), axis=0, keepdims=True)
            st["cnt"] = st["cnt"] + (iota16 == a_sel).astype(f32)
            if not exact:
                st["taken"] = st["taken"] + jnp.sum(sel.astype(f32), axis=0, keepdims=True)

        _extract_top(cand_scr[...], PEER_TOPK, pick_c, break_ties=exact)
        inv_z = 1.0 / st["z"]
        cnt1 = jnp.zeros(s1.shape, f32)
        for a in range(PEER_TOPK):
            cnt1 = jnp.where(ranks[0] == float(a), st["cnt"][a:a + 1], cnt1)
        c1_ref[hh, :, pl.ds(c0, LANE)] = jnp.exp(s1 - top1) * inv_z
        cnt1_ref[hh, :, pl.ds(c0, LANE)] = cnt1
        rank2_ref[hh, :, pl.ds(c0, LANE)] = ranks[1]
        e2_ref[hh, :, pl.ds(c0, LANE)] = jnp.exp(s2 - top2)
        if exact:
            return None
        want = float(PEER_TOPK)
        tied = st["taken"] != want
        for rank in ranks:
            tied = tied | (jnp.sum((rank < want).astype(f32), axis=0, keepdims=True) != want)
        return tied

    def body(it, carry):
        tied = process(it, exact=False)

        @pl.when(jnp.max(tied.astype(f32)) > 0.0)
        def _():
            process(it, exact=True)

        return carry

    lax.fori_loop(0, PEER_HEADS * n_chunk, body, 0)


def _peer_dense_kernel(hb_ref, u_ref, vt_ref, c1_ref, cnt1_ref, rank2_ref, e2_ref, x_ref, gm_ref,
                       o_ref, ht0_ref, ht1_ref, gh0_ref, gh1_ref, acc_ref):
    f32, bf16 = jnp.float32, jnp.bfloat16
    s = pl.program_id(1)
    tb = hb_ref.shape[0]
    n_rows = PEER_ET // N_KEYS
    sub = 16
    nt = (((1,), (1,)), ((), ()))

    @pl.when(s == 0)
    def _():
        acc_ref[...] = jnp.zeros_like(acc_ref)
        ht1_ref[...] = jnp.zeros_like(ht1_ref)
        gh0_ref[...] = jnp.zeros_like(gh0_ref)
        gh1_ref[...] = jnp.zeros_like(gh1_ref)

    def stages(ht_w, ht_r, gh_w, gh_r):
        ht_w[...] = lax.dot_general(u_ref[...], hb_ref[...], nt, preferred_element_type=f32)
        for c0 in range(0, tb, LANE):
            lanes = slice(c0, c0 + LANE)
            for s0 in range(0, N_KEYS, sub):
                g = [jnp.zeros((sub, LANE), f32) for _ in range(n_rows)]
                for hh in range(PEER_HEADS):
                    rk = rank2_ref[hh, s0:s0 + sub, lanes]
                    ev = e2_ref[hh, s0:s0 + sub, lanes]
                    for r in range(n_rows):
                        g[r] = g[r] + jnp.where(rk < cnt1_ref[hh, r:r + 1, lanes], ev * c1_ref[hh, r:r + 1, lanes], 0.0)
                for r in range(n_rows):
                    rows = slice(r * N_KEYS + s0, r * N_KEYS + s0 + sub)
                    pre = ht_r[rows, lanes]
                    act = 0.5 * pre * (1.0 + lax.erf(pre * (2.0 ** -0.5)))
                    gh_w[rows, lanes] = (g[r] * act).astype(bf16)
        acc_ref[...] += jnp.dot(vt_ref[...], gh_r[...], preferred_element_type=f32)

    @pl.when(s % 2 == 0)
    def _():
        stages(ht0_ref, ht1_ref, gh1_ref, gh0_ref)

    @pl.when(s % 2 == 1)
    def _():
        stages(ht1_ref, ht0_ref, gh0_ref, gh1_ref)

    @pl.when(s == pl.num_programs(1) - 1)
    def _():
        upd = acc_ref[...].T.reshape(x_ref.shape)
        o_ref[...] = x_ref[...] + gm_ref[...] * upd


def _peer_sublayer(x, shift, scale, gate, norm_g, wq_b, k1p, k2p, u_b, vt_b):
    B, T, D = x.shape
    n = B * T
    tb = PEER_TB
    assert n % tb == 0
    if T % tb == 0:
        nbs, tper, per = 1, tb, T // tb
        xmap = lambda i, *_: (i // per, i % per, 0)
        mmap = lambda i, *_: (i // per, 0, 0)
    else:
        assert tb % T == 0 and T % 8 == 0
        nbs, tper = tb // T, T
        xmap = lambda i, *_: (i, 0, 0)
        mmap = lambda i, *_: (i, 0, 0)
    nblk = n // tb
    f32 = jnp.float32
    x_spec = pl.BlockSpec((nbs, tper, D), xmap)
    m_spec = pl.BlockSpec((nbs, 1, D), mmap)
    n_cand_pad = -(-len(_PEER_CAND) // 8) * 8
    a_of = jnp.asarray(np.broadcast_to(np.array([a for a, _ in _PEER_CAND] + [0] * (n_cand_pad - len(_PEER_CAND)),
                                                np.int32)[:, None], (n_cand_pad, LANE)))
    route_shape = lambda dt: jax.ShapeDtypeStruct((PEER_HEADS, N_KEYS, n), dt)
    route_spec = pl.BlockSpec((PEER_HEADS, N_KEYS, tb), lambda i: (0, 0, i))
    hb, c1, cnt1, rank2, e2 = pl.pallas_call(
        _peer_route_kernel,
        grid=(nblk,),
        in_specs=[x_spec, m_spec, m_spec,
                  pl.BlockSpec((1, D), lambda i: (0, 0)),
                  pl.BlockSpec((D, PEER_HEADS * PEER_QDIM), lambda i: (0, 0)),
                  pl.BlockSpec((PEER_HEADS, N_KEYS, PEER_QDIM), lambda i: (0, 0, 0)),
                  pl.BlockSpec((PEER_HEADS, N_KEYS, PEER_QDIM), lambda i: (0, 0, 0)),
                  pl.BlockSpec((n_cand_pad, LANE), lambda i: (0, 0))],
        out_specs=[pl.BlockSpec((tb, D), lambda i: (i, 0)), route_spec, route_spec, route_spec, route_spec],
        out_shape=[jax.ShapeDtypeStruct((n, D), jnp.bfloat16)] + [route_shape(f32)] * 4,
        scratch_shapes=[pltpu.VMEM((PEER_HEADS, tb, PEER_QDIM), jnp.bfloat16), pltpu.VMEM((PEER_TOPK, LANE), f32),
                        pltpu.VMEM((PEER_TOPK, LANE), f32), pltpu.VMEM((n_cand_pad, LANE), f32)],
        compiler_params=pltpu.CompilerParams(dimension_semantics=("parallel",), vmem_limit_bytes=VMEM_LIMIT),
        name="peer_route",
    )(x, shift, scale, norm_g.reshape(1, D), wq_b, k1p, k2p, a_of)

    rows = PEER_ET // N_KEYS
    n_tiles = N_EXPERTS // PEER_ET
    tile = lambda s, lag: jnp.clip(s - lag, 0, n_tiles - 1)
    sub_spec = pl.BlockSpec((PEER_HEADS, rows, tb), lambda i, s: (0, tile(s, 1), i))
    full_spec = pl.BlockSpec((PEER_HEADS, N_KEYS, tb), lambda i, s: (0, 0, i))
    return pl.pallas_call(
        _peer_dense_kernel,
        grid=(nblk, n_tiles + 2),
        in_specs=[pl.BlockSpec((tb, D), lambda i, s: (i, 0)),
                  pl.BlockSpec((PEER_ET, D), lambda i, s: (tile(s, 0), 0)),
                  pl.BlockSpec((D, PEER_ET), lambda i, s: (0, tile(s, 2))),
                  sub_spec, sub_spec, full_spec, full_spec,
                  pl.BlockSpec((nbs, tper, D), lambda i, s: xmap(i)),
                  pl.BlockSpec((nbs, 1, D), lambda i, s: mmap(i))],
        out_specs=pl.BlockSpec((nbs, tper, D), lambda i, s: xmap(i)),
        out_shape=jax.ShapeDtypeStruct(x.shape, x.dtype),
        scratch_shapes=[pltpu.VMEM((PEER_ET, tb), f32), pltpu.VMEM((PEER_ET, tb), f32),
                        pltpu.VMEM((PEER_ET, tb), jnp.bfloat16), pltpu.VMEM((PEER_ET, tb), jnp.bfloat16),
                        pltpu.VMEM((D, tb), f32)],
        compiler_params=pltpu.CompilerParams(dimension_semantics=("parallel", "arbitrary"), vmem_limit_bytes=VMEM_LIMIT),
        name="peer_dense",
    )(hb, u_b, vt_b, c1, cnt1, rank2, e2, x, gate)


def _peer_weights(w_q, k1, k2, u_tab, v_tab):
    bf16 = jnp.bfloat16
    half = PEER_QDIM // 2
    k1p = jnp.pad(k1, ((0, 0), (0, 0), (0, half))).astype(bf16)
    k2p = jnp.pad(k2, ((0, 0), (0, 0), (half, 0))).astype(bf16)
    return w_q.astype(bf16), k1p, k2p, u_tab.astype(bf16), v_tab.astype(bf16).T


def _layer_pool(cache, j, page_table):
    return cache.reshape((-1,) + cache.shape[2:]), page_table + j * cache.shape[1]


def kernel(x_prompt, x_sample, cache_nsa_kv, cache_diff_kv, state_nsa_window, state_gdn_S, state_gdn_conv,
           page_table, c_prompt, c_sample, ada_w, ada_b, norm_mix_g, norm_ffn_g, final_norm_g,
           gdn_w_in, gdn_conv_w, gdn_a_log, gdn_dt_bias, gdn_norm_g, gdn_w_out,
           nsa_w_in, nsa_cmp_pos, nsa_cmp_w, nsa_w_out,
           diff_w_in, diff_lq1, diff_lk1, diff_lq2, diff_lk2, diff_subln_g, diff_w_out,
           peer_w_q, peer_k1, peer_k2, peer_u, peer_v):
    past_len = page_table.shape[1] * cache_nsa_kv.shape[2]
    peer_w = [_peer_weights(peer_w_q[i], peer_k1[i], peer_k2[i], peer_u[i], peer_v[i]) for i in range(DEPTH)]

    def trunk(x, c, sample):
        B, T, _ = x.shape
        pos0 = past_len if sample else 0
        cs = jax.nn.silu(c)
        new_S, new_conv, new_nsa_kv, new_nsa_win, new_diff_kv = [], [], [], [], []
        for i in range(DEPTH):
            mod = (cs @ ada_w[i] + ada_b[i]).reshape(B, 6, 1, D_MODEL)
            h = rmsnorm(x, norm_mix_g[i]) * (1.0 + mod[:, 1]) + mod[:, 0]
            j = i // N_MIXERS
            kind = i % N_MIXERS
            if kind == 0:
                if sample:
                    S0 = state_gdn_S[j].astype(jnp.float32)
                    buf = state_gdn_conv[j]
                else:
                    S0 = jnp.zeros((B, GDN_HEADS, GDN_DK, GDN_DV), jnp.float32)
                    buf = jnp.zeros((B, CONV_W - 1, GDN_HEADS * (2 * GDN_DK + GDN_DV)), x.dtype)
                m, buf_n, S_n = gdn_mixer(h, buf, S0, gdn_w_in[j], gdn_conv_w[j], gdn_a_log[j], gdn_dt_bias[j], gdn_norm_g[j], gdn_w_out[j])
                new_S.append(S_n.astype(x.dtype))
                new_conv.append(buf_n)
            elif kind == 1:
                past = _layer_pool(cache_nsa_kv, j, page_table) if sample else None
                wbuf = state_nsa_window[j] if sample else None
                m, kv_n, win_n = nsa_mixer(h, pos0, past, wbuf, nsa_w_in[j], nsa_cmp_pos[j], nsa_cmp_w[j], nsa_w_out[j])
                new_nsa_kv.append(kv_n)
                new_nsa_win.append(win_n)
            else:
                past = _layer_pool(cache_diff_kv, j, page_table) if sample else None
                m, kv_n = diff_mixer(h, pos0, past, i, diff_w_in[j], diff_lq1[j], diff_lk1[j], diff_lq2[j], diff_lk2[j], diff_subln_g[j], diff_w_out[j])
                new_diff_kv.append(kv_n)
            x = x + mod[:, 2] * m
            x = _peer_sublayer(x, mod[:, 3], mod[:, 4], mod[:, 5], norm_ffn_g[i], *peer_w[i])
        y = rmsnorm(x, final_norm_g)
        return y, jnp.stack(new_S), jnp.stack(new_conv), jnp.stack(new_nsa_kv), jnp.stack(new_nsa_win), jnp.stack(new_diff_kv)

    y_prompt, p_gdn_S, p_gdn_conv, p_nsa_kv, p_nsa_win, p_diff_kv = trunk(x_prompt, c_prompt, False)
    y_sample, s_gdn_S, s_gdn_conv, s_nsa_kv, s_nsa_win, s_diff_kv = trunk(x_sample, c_sample, True)
    return (y_prompt, y_sample, p_gdn_S, p_gdn_conv, p_nsa_kv, p_nsa_win, p_diff_kv, s_gdn_S, s_gdn_conv, s_nsa_kv, s_nsa_win, s_diff_kv)
```

```python
import functools
import math

import jax
import jax.numpy as jnp
import numpy as np
from jax import lax
from jax.experimental import pallas as pl
from jax.experimental.pallas import tpu as pltpu

D_MODEL = 1024
DEPTH = 4
N_MIXERS = 3
HEAD_DIM = 128
GDN_HEADS = D_MODEL // HEAD_DIM
GDN_DK = HEAD_DIM
GDN_DV = HEAD_DIM
CONV_W = 4
GDN_CHUNK = 64
NSA_HEADS = D_MODEL // HEAD_DIM
NSA_KV_GROUPS = 2
CMP_BLOCK = 32
SEL_BLOCK = 64
TOP_N = 16
WINDOW = 512
Q_BLOCK = 128
DIFF_HEADS = D_MODEL // HEAD_DIM
DIFF_DH = D_MODEL // DIFF_HEADS // 2
PEER_HEADS = 8
N_KEYS = 128
N_EXPERTS = N_KEYS * N_KEYS
PEER_TOPK = 16
PEER_QDIM = 128
PEER_TOKEN_BLOCK = 256
ROPE_THETA = 10000.0
EPS = 1e-6

LANE = 128
VMEM_LIMIT = 48 * 1024 * 1024


def _mm_kernel(x_ref, w_ref, o_ref):
    o_ref[...] = jnp.dot(x_ref[...].astype(jnp.bfloat16), w_ref[...], preferred_element_type=jnp.float32)


def _mm(x, w, tm=256):
    lead = x.shape[:-1]
    K = x.shape[-1]
    N = w.shape[1]
    x2 = x.reshape(-1, K)
    M = x2.shape[0]
    n_pad = -(-N // 256) * 256
    wb = jnp.pad(w, ((0, 0), (0, n_pad - N))).astype(jnp.bfloat16)
    tm = min(tm, M)
    assert M % tm == 0
    out = pl.pallas_call(
        _mm_kernel,
        grid=(M // tm,),
        in_specs=[pl.BlockSpec((tm, K), lambda i: (i, 0)), pl.BlockSpec((K, n_pad), lambda i: (0, 0))],
        out_specs=pl.BlockSpec((tm, n_pad), lambda i: (i, 0)),
        out_shape=jax.ShapeDtypeStruct((M, n_pad), jnp.float32),
        compiler_params=pltpu.CompilerParams(dimension_semantics=("parallel",), vmem_limit_bytes=VMEM_LIMIT),
        name="proj_matmul",
    )(x2, wb)
    return out[:, :N].reshape(*lead, N)


def rmsnorm(x, g):
    xf = x.astype(jnp.float32)
    y = xf * lax.rsqrt(jnp.mean(xf * xf, -1, keepdims=True) + EPS)
    return (y * g.astype(jnp.float32)).astype(x.dtype)


def l2norm(x):
    xf = x.astype(jnp.float32)
    return (xf * lax.rsqrt(jnp.sum(xf * xf, -1, keepdims=True) + EPS)).astype(x.dtype)


def rope(x, pos):
    half = x.shape[-1] // 2
    inv = ROPE_THETA ** (-jnp.arange(half, dtype=jnp.float32) / half)
    ang = pos.astype(jnp.float32)[:, None] * inv[None, :]
    cos = jnp.cos(ang)[:, None, :]
    sin = jnp.sin(ang)[:, None, :]
    xf = x.astype(jnp.float32)
    x1, x2 = xf[..., :half], xf[..., half:]
    return jnp.concatenate([x1 * cos - x2 * sin, x2 * cos + x1 * sin], -1).astype(x.dtype)


def masked_softmax(s, mask):
    s = jnp.where(mask, s.astype(jnp.float32), -jnp.inf)
    m = jnp.max(s, -1, keepdims=True)
    m = jnp.where(jnp.isfinite(m), m, 0.0)
    p = jnp.exp(s - m)
    return p / jnp.maximum(jnp.sum(p, -1, keepdims=True), 1e-30)


def gather_pages(pool, page_table):
    g = pool[page_table]
    return g.reshape(g.shape[0], g.shape[1] * g.shape[2], *pool.shape[2:])


def causal_conv(x, buf, w):
    T = x.shape[1]
    xx = jnp.concatenate([buf, x], 1)
    y = xx[:, 0:T] * w[0]
    for j in range(1, CONV_W):
        y = y + xx[:, j:j + T] * w[j]
    return jax.nn.silu(y), xx[:, -(CONV_W - 1):]


def gated_delta_rule(q, k, v, g, beta, S0):
    f32 = jnp.float32
    B, T, H, dk = q.shape
    dv = v.shape[-1]
    C = GDN_CHUNK
    Tp = -(-T // C) * C

    def prep(a):
        a = a.astype(f32)
        a = jnp.pad(a, [(0, 0), (0, Tp - T)] + [(0, 0)] * (a.ndim - 2))
        a = a.reshape(B, Tp // C, C, *a.shape[2:])
        return jnp.moveaxis(a, (1, 3), (0, 2))

    qc = prep(q) * dk ** -0.5
    kc, vc, gc, bc = prep(k), prep(v), prep(g), prep(beta)
    gcum = jnp.cumsum(gc, -1)
    idx = jnp.arange(C)
    incl = idx[:, None] >= idx[None, :]
    strict = idx[:, None] > idx[None, :]
    decay = jnp.exp(jnp.where(incl, gcum[..., :, None] - gcum[..., None, :], -jnp.inf))
    kb = kc * bc[..., None]
    L = jnp.where(strict, jnp.einsum('nbhid,nbhjd->nbhij', kb, kc) * decay, 0.0)
    eye = jnp.eye(C, dtype=f32)
    A = eye + L
    Tinv = lax.linalg.triangular_solve(A, jnp.broadcast_to(eye, A.shape), left_side=True, lower=True, unit_diagonal=True)
    u = Tinv @ (vc * bc[..., None])
    w = Tinv @ (kb * jnp.exp(gcum)[..., None])
    a_intra = jnp.einsum('nbhid,nbhjd->nbhij', qc, kc) * decay

    def step(S, xs):
        q_i, k_i, u_i, w_i, a_i, g_i = xs
        v_new = u_i - w_i @ S
        o = (q_i * jnp.exp(g_i)[..., None]) @ S + a_i @ v_new
        g_last = g_i[..., -1:]
        S = S * jnp.exp(g_last)[..., None] + jnp.einsum('bhck,bhcv->bhkv', k_i * jnp.exp(g_last - g_i)[..., None], v_new)
        return S, o

    S, o = lax.scan(step, S0.astype(f32), (qc, kc, u, w, a_intra, gcum))
    o = jnp.moveaxis(o, (0, 2), (1, 3)).reshape(B, Tp, H, dv)[:, :T]
    return o, S


GDN_TM = 256
_CARRY = 8


def _gdn_in_kernel(x_ref, shift_ref, scale_ref, ng_ref, w_ref, cw_ref, buf_ref, dyn_ref,
                   q_ref, k_ref, v_ref, z_ref, gb_ref, conv_ref, carry):
    f32 = jnp.float32
    i = pl.program_id(1)
    H, dk, dv = GDN_HEADS, GDN_DK, GDN_DV
    n_qkv = H * (2 * dk + dv)
    tm = x_ref.shape[1]

    @pl.when(i == 0)
    def _():
        carry[...] = buf_ref[0]

    x = x_ref[0]
    y = x * lax.rsqrt(jnp.mean(x * x, -1, keepdims=True) + EPS) * ng_ref[...]
    h = y * (1.0 + scale_ref[0]) + shift_ref[0]
    proj = jnp.dot(h.astype(jnp.bfloat16), w_ref[...], preferred_element_type=f32)
    xx = jnp.concatenate([carry[...], proj[:, :n_qkv]], axis=0)
    first = _CARRY - (CONV_W - 1)
    conv = xx[first:first + tm] * cw_ref[0:1, :]
    for j in range(1, CONV_W):
        conv = conv + xx[first + j:first + j + tm] * cw_ref[j:j + 1, :]
    act = conv * jax.nn.sigmoid(conv)
    carry[...] = xx[tm:tm + _CARRY]
    for hh in range(H):
        for ref, base in ((q_ref, 0), (k_ref, H * dk)):
            a = act[:, base + hh * dk:base + (hh + 1) * dk]
            ref[0, :, hh * dk:(hh + 1) * dk] = a * lax.rsqrt(jnp.sum(a * a, -1, keepdims=True) + EPS)
    v_ref[0] = act[:, 2 * H * dk:]
    z_ref[0] = proj[:, n_qkv:n_qkv + H * dv]
    tail = proj[:, n_qkv + H * dv:n_qkv + H * dv + LANE]
    t2 = tail + dyn_ref[1:2, :]
    softplus = jnp.maximum(t2, 0.0) + jnp.log(1.0 + jnp.exp(-jnp.abs(t2)))
    lane = lax.broadcasted_iota(jnp.int32, tail.shape, 1)
    gb = jnp.where(lane < H, jax.nn.sigmoid(tail), dyn_ref[0:1, :] * softplus)
    gb_ref[0] = gb[:, :2 * H]

    @pl.when(i == pl.num_programs(1) - 1)
    def _():
        conv_ref[0] = xx[tm:tm + _CARRY]


def gdn_mixer(x, shift, scale, mix_g, conv_buf, S0, w_in, conv_w, a_log, dt_bias, norm_g, w_out):
    B, T, D = x.shape
    H, dk, dv = GDN_HEADS, GDN_DK, GDN_DV
    n_qkv = H * (2 * dk + dv)
    f32 = jnp.float32
    tm = GDN_TM if T % GDN_TM == 0 else T
    assert T % tm == 0 and tm % 8 == 0
    n_cols = w_in.shape[1]
    n_pad = n_qkv + H * dv + LANE
    assert n_cols == n_qkv + H * dv + 2 * H
    wb = jnp.pad(w_in, ((0, 0), (0, n_pad - n_cols))).astype(jnp.bfloat16)
    buf8 = jnp.pad(conv_buf, ((0, 0), (_CARRY - (CONV_W - 1), 0), (0, 0)))
    dyn = jnp.zeros((2, LANE), f32).at[0, H:2 * H].set(-jnp.exp(a_log.astype(f32))).at[1, H:2 * H].set(dt_bias.astype(f32))
    tok = lambda w: pl.BlockSpec((1, tm, w), lambda b, i: (b, i, 0))
    per_seq = lambda r, w: pl.BlockSpec((1, r, w), lambda b, i: (b, 0, 0))
    whole = lambda r, w: pl.BlockSpec((r, w), lambda b, i: (0, 0))
    q, k, v, z, gb, conv8 = pl.pallas_call(
        _gdn_in_kernel,
        grid=(B, T // tm),
        in_specs=[tok(D), per_seq(1, D), per_seq(1, D), whole(1, D), whole(D, n_pad), whole(CONV_W, n_qkv),
                  per_seq(_CARRY, n_qkv), whole(2, LANE)],
        out_specs=[tok(H * dk), tok(H * dk), tok(H * dv), tok(H * dv), tok(2 * H), per_seq(_CARRY, n_qkv)],
        out_shape=[jax.ShapeDtypeStruct((B, T, H * dk), f32), jax.ShapeDtypeStruct((B, T, H * dk), f32),
                   jax.ShapeDtypeStruct((B, T, H * dv), f32), jax.ShapeDtypeStruct((B, T, H * dv), f32),
                   jax.ShapeDtypeStruct((B, T, 2 * H), f32), jax.ShapeDtypeStruct((B, _CARRY, n_qkv), f32)],
        scratch_shapes=[pltpu.VMEM((_CARRY, n_qkv), f32)],
        compiler_params=pltpu.CompilerParams(dimension_semantics=("parallel", "arbitrary"), vmem_limit_bytes=VMEM_LIMIT),
        name="gdn_in",
    )(x, shift, scale, mix_g.reshape(1, D), wb, conv_w, buf8, dyn)
    o, S = _gdn_delta_rule(q, k, v, z, gb[..., H:], gb[..., :H], S0, norm_g)
    return _mm(o, w_out), conv8[:, _CARRY - (CONV_W - 1):], S


def _dot_split3(a, b):
    f32, bf16 = jnp.float32, jnp.bfloat16
    ah = a.astype(bf16)
    al = (a - ah.astype(f32)).astype(bf16)
    bh = b.astype(bf16)
    bl = (b - bh.astype(f32)).astype(bf16)
    dot = functools.partial(jnp.dot, preferred_element_type=f32)
    return dot(ah, bh) + dot(ah, bl) + dot(al, bh)


def _gdn_chunk_kernel(q_ref, k_ref, v_ref, z_ref, col_ref, row_ref, s0_ref, ng_ref, o_ref, sout_ref, s_scr):
    f32, bf16 = jnp.float32, jnp.bfloat16
    n = pl.program_id(1)
    C = q_ref.shape[1]
    H, dk, dv = GDN_HEADS, GDN_DK, GDN_DV
    dot = functools.partial(jnp.dot, preferred_element_type=f32)
    nt = (((1,), (1,)), ((), ()))

    @pl.when(n == 0)
    def _():
        s_scr[...] = s0_ref[0]

    ii = lax.broadcasted_iota(jnp.int32, (C, C), 0)
    jj = lax.broadcasted_iota(jnp.int32, (C, C), 1)
    incl = ii >= jj
    strict = ii > jj
    eye = (ii == jj).astype(f32)
    heads = range(H)
    ks = [slice(h * dk, (h + 1) * dk) for h in heads]
    vs = [slice(h * dv, (h + 1) * dv) for h in heads]
    gc = [col_ref[0, :, h:h + 1] for h in heads]
    bc = [col_ref[0, :, H + h:H + h + 1] for h in heads]
    gr = [row_ref[0, 0, h:h + 1, :] for h in heads]
    decay = [jnp.exp(jnp.where(incl, gc[h] - gr[h], -jnp.inf)) for h in heads]
    kcb = [k_ref[0, :, ks[h]].astype(bf16) for h in heads]
    kb = [k_ref[0, :, ks[h]] * bc[h] for h in heads]
    mpow = [-jnp.where(strict, lax.dot_general(kb[h].astype(bf16), kcb[h], nt, preferred_element_type=f32) * decay[h], 0.0)
            for h in heads]
    tinv = [eye + mpow[h] for h in heads]
    for _ in range(C.bit_length() - 2):
        mpow = [_dot_split3(mpow[h], mpow[h]) for h in heads]
        tinv = [tinv[h] + _dot_split3(tinv[h], mpow[h]) for h in heads]
    eg = [jnp.exp(gc[h]) for h in heads]
    uw = [dot(tinv[h].astype(bf16), jnp.concatenate([v_ref[0, :, vs[h]] * bc[h], kb[h] * eg[h]], axis=1).astype(bf16))
          for h in heads]
    qh = [q_ref[0, :, ks[h]] * dk ** -0.5 for h in heads]
    a_intra = [(lax.dot_general(qh[h].astype(bf16), kcb[h], nt, preferred_element_type=f32) * decay[h]).astype(bf16)
               for h in heads]
    s_b = [s_scr[h].astype(bf16) for h in heads]
    v_nb = [(uw[h][:, :dv] - dot(uw[h][:, dv:].astype(bf16), s_b[h])).astype(bf16) for h in heads]
    o = [dot((qh[h] * eg[h]).astype(bf16), s_b[h]) + dot(a_intra[h], v_nb[h]) for h in heads]
    for h in heads:
        g_last = gr[h][:, C - 1:C]
        kd = (k_ref[0, :, ks[h]] * jnp.exp(g_last - gc[h])).astype(bf16)
        s_scr[h] = s_scr[h] * jnp.exp(g_last) + lax.dot_general(kd, v_nb[h], _TN, preferred_element_type=f32)
    for h in heads:
        y = o[h] * lax.rsqrt(jnp.mean(o[h] * o[h], -1, keepdims=True) + EPS) * ng_ref[...]
        zz = z_ref[0, :, vs[h]]
        o_ref[0, :, vs[h]] = y * (zz * jax.nn.sigmoid(zz))

    @pl.when(n == pl.num_programs(1) - 1)
    def _():
        sout_ref[0] = s_scr[...]


def _gdn_delta_rule(q, k, v, z, g, beta, S0, norm_g):
    B, T, _ = q.shape
    H, dk, dv = GDN_HEADS, GDN_DK, GDN_DV
    C = GDN_CHUNK if T % GDN_CHUNK == 0 else T
    assert C & (C - 1) == 0 and C % 8 == 0
    nC = T // C
    f32 = jnp.float32
    gcum = jnp.cumsum(g.astype(f32).reshape(B, nC, C, H), axis=2)
    col = jnp.concatenate([gcum.reshape(B, T, H), beta.astype(f32)], axis=-1)
    row = jnp.swapaxes(gcum, 2, 3)
    tok = lambda w: pl.BlockSpec((1, C, w), lambda b, n: (b, n, 0))
    o, s_out = pl.pallas_call(
        _gdn_chunk_kernel,
        grid=(B, nC),
        in_specs=[tok(H * dk), tok(H * dk), tok(H * dv), tok(H * dv), tok(2 * H),
                  pl.BlockSpec((1, 1, H, C), lambda b, n: (b, n, 0, 0)),
                  pl.BlockSpec((1, H, dk, dv), lambda b, n: (b, 0, 0, 0)),
                  pl.BlockSpec((1, dv), lambda b, n: (0, 0))],
        out_specs=[tok(H * dv), pl.BlockSpec((1, H, dk, dv), lambda b, n: (b, 0, 0, 0))],
        out_shape=[jax.ShapeDtypeStruct((B, T, H * dv), f32), jax.ShapeDtypeStruct((B, H, dk, dv), f32)],
        scratch_shapes=[pltpu.VMEM((H, dk, dv), f32)],
        compiler_params=pltpu.CompilerParams(dimension_semantics=("parallel", "arbitrary"), vmem_limit_bytes=VMEM_LIMIT),
        name="gdn_delta_rule",
    )(q, k, v, z, col, row, S0.astype(f32), norm_g.reshape(1, dv))
    return o, s_out


def nsa_seq(q_s, g_s, full_s, win_s, pos0, wpos0, cmp_pos, cmp_w):
    dt = q_s.dtype
    T = q_s.shape[0]
    G, R, dh = NSA_KV_GROUPS, NSA_HEADS // NSA_KV_GROUPS, HEAD_DIM
    Tk = full_s.shape[0]
    Tkp = -(-Tk // SEL_BLOCK) * SEL_BLOCK
    full_p = jnp.pad(full_s, ((0, Tkp - Tk), (0, 0), (0, 0), (0, 0)))
    n_cmp = Tkp // CMP_BLOCK
    n_sel = Tkp // SEL_BLOCK
    blocks = full_p[:, :2].reshape(n_cmp, CMP_BLOCK, 2, G, dh) + jnp.transpose(cmp_pos, (1, 0, 2))[:, :, None, :]
    kv_cmp = jnp.einsum('nlsgd,slde->nsge', blocks, cmp_w)
    k_cmp, v_cmp = kv_cmp[:, 0], kv_cmp[:, 1]
    cmp_end = (jnp.arange(n_cmp) + 1) * CMP_BLOCK - 1
    sel = jnp.transpose(full_p[:, 2:].reshape(n_sel, SEL_BLOCK, 2, G, dh), (2, 3, 0, 1, 4))
    k_selb, v_selb = sel[0], sel[1]
    n_top = min(TOP_N, n_sel)
    win_p = jnp.pad(win_s, ((WINDOW, 0), (0, 0), (0, 0), (0, 0)))
    qb = Q_BLOCK if T % Q_BLOCK == 0 else T
    nqb = T // qb
    lw = WINDOW + qb - 1
    scale = HEAD_DIM ** -0.5
    g_idx = jnp.arange(G)[None, :, None]
    blk_ids = jnp.arange(n_sel)

    def block(args):
        qi, gi, bi = args
        qstart = pos0 + bi * qb
        qpos = qstart + jnp.arange(qb)
        qg = qi.reshape(qb, G, R, dh) * scale
        s_c = jnp.einsum('qgrd,ngd->qgrn', qg, k_cmp)
        p_c = masked_softmax(s_c, (cmp_end[None, :] <= qpos[:, None])[:, None, None, :])
        o_c = jnp.einsum('qgrn,ngd->qgrd', p_c.astype(dt), v_cmp)
        imp = p_c.sum(2).reshape(qb, G, n_sel, SEL_BLOCK // CMP_BLOCK).sum(-1)
        cur = (qpos // SEL_BLOCK)[:, None] == blk_ids[None, :]
        causal_blk = (blk_ids * SEL_BLOCK)[None, :] <= qpos[:, None]
        imp = jnp.where(cur[:, None], jnp.inf, jnp.where(causal_blk[:, None], imp, -jnp.inf))
        top_v, top_i = lax.top_k(imp, n_top)
        k_sel = k_selb[g_idx, top_i]
        v_sel = v_selb[g_idx, top_i]
        kpos = top_i[..., None] * SEL_BLOCK + jnp.arange(SEL_BLOCK)
        m_s = (kpos <= qpos[:, None, None, None]) & (top_v > -jnp.inf)[..., None]
        s_s = jnp.einsum('qgrd,qgksd->qgrks', qg, k_sel).reshape(qb, G, R, n_top * SEL_BLOCK)
        p_s = masked_softmax(s_s, m_s.reshape(qb, G, 1, n_top * SEL_BLOCK))
        o_s = jnp.einsum('qgrm,qgmd->qgrd', p_s.astype(dt), v_sel.reshape(qb, G, n_top * SEL_BLOCK, dh))
        wblk = lax.dynamic_slice_in_dim(win_p, qstart - wpos0 + 1, lw, axis=0)
        wpos = qstart - WINDOW + 1 + jnp.arange(lw)
        m_w = (wpos[None] <= qpos[:, None]) & (wpos[None] > qpos[:, None] - WINDOW) & (wpos[None] >= wpos0)
        s_w = jnp.einsum('qgrd,kgd->qgrk', qg, wblk[:, 0])
        p_w = masked_softmax(s_w, m_w[:, None, None, :])
        o_w = jnp.einsum('qgrk,kgd->qgrd', p_w.astype(dt), wblk[:, 1])
        gg = gi.reshape(qb, G, R, 3)
        o = gg[..., 0:1] * o_c + gg[..., 1:2] * o_s + gg[..., 2:3] * o_w
        return o.reshape(qb, NSA_HEADS, dh)

    out = lax.map(block, (q_s.reshape(nqb, qb, NSA_HEADS, dh), g_s.reshape(nqb, qb, NSA_HEADS, 3), jnp.arange(nqb)))
    return out.reshape(T, NSA_HEADS, dh)


NSA_R = NSA_HEADS // NSA_KV_GROUPS
NSA_TK = 512
NSA_WSPAN = WINDOW + Q_BLOCK


def _softmax_rows(s, mask):
    s = jnp.where(mask, s, -jnp.inf)
    m = jnp.max(s, axis=0, keepdims=True)
    m = jnp.where(m == -jnp.inf, 0.0, m)
    p = jnp.exp(s - m)
    return p * (1.0 / jnp.maximum(jnp.sum(p, axis=0, keepdims=True), 1e-30))


def _nsa_cmp_kernel(x_ref, pos_ref, w_ref, o_ref):
    xb = (x_ref[0, 0, 0, 0] + pos_ref[0]).astype(jnp.bfloat16)
    o_ref[0, 0, 0, 0] = jnp.dot(xb, w_ref[0], preferred_element_type=jnp.float32)


def _nsa_compress(rows_cmp, cmp_pos, cmp_w):
    B, T, _, G, dh = rows_cmp.shape
    n_sel = T // SEL_BLOCK
    ld = CMP_BLOCK * dh
    x = rows_cmp.reshape(B, n_sel, 2, CMP_BLOCK, 2, G, dh)
    x = jnp.transpose(x, (0, 4, 5, 2, 1, 3, 6)).reshape(B, 2, G, 2, n_sel, ld)
    return pl.pallas_call(
        _nsa_cmp_kernel,
        grid=(B, 2, G, 2),
        in_specs=[pl.BlockSpec((1, 1, 1, 1, n_sel, ld), lambda b, s, g, p: (b, s, g, p, 0, 0)),
                  pl.BlockSpec((1, 1, ld), lambda b, s, g, p: (s, 0, 0)),
                  pl.BlockSpec((1, ld, dh), lambda b, s, g, p: (s, 0, 0))],
        out_specs=pl.BlockSpec((1, 1, 1, 1, n_sel, dh), lambda b, s, g, p: (b, s, g, p, 0, 0)),
        out_shape=jax.ShapeDtypeStruct((B, 2, G, 2, n_sel, dh), jnp.float32),
        compiler_params=pltpu.CompilerParams(dimension_semantics=("parallel",) * 4, vmem_limit_bytes=VMEM_LIMIT),
        name="nsa_compress",
    )(x, cmp_pos.reshape(2, 1, ld), cmp_w.reshape(2, ld, dh).astype(jnp.bfloat16))


def _nsa_prompt_kernel(q_ref, gt_ref, kc_ref, vct_ref, ks_ref, vst_ref, kw_ref, vwt_ref, o_ref,
                       sel_scr, m_scr, l_scr, acc_scr):
    f32, bf16 = jnp.float32, jnp.bfloat16
    i = pl.program_id(2)
    R, QB = NSA_R, Q_BLOCK
    W = R * QB
    T = ks_ref.shape[2]
    n_sel = T // SEL_BLOCK
    nt = (((1,), (1,)), ((), ()))
    qb = q_ref[0]
    q = jnp.concatenate([qb[:, r * HEAD_DIM:(r + 1) * HEAD_DIM] for r in range(R)], axis=0)
    q = (q * HEAD_DIM ** -0.5).astype(bf16)
    lane = lax.broadcasted_iota(jnp.int32, (1, W), 1)
    qpos = i * QB + (lane & (QB - 1))

    s = lax.dot_general(kc_ref[0, 0], q, nt, preferred_element_type=f32)
    row = lax.broadcasted_iota(jnp.int32, (2 * n_sel, 1), 0)
    cidx = jnp.where(row < n_sel, 2 * row, 2 * (row - n_sel) + 1)
    p_c = _softmax_rows(s, (cidx + 1) * CMP_BLOCK - 1 <= qpos)
    o_c = jnp.dot(vct_ref[0, 0], p_c.astype(bf16), preferred_element_type=f32)

    p_r = p_c[:, 0:QB]
    for r in range(1, R):
        p_r = p_r + p_c[:, r * QB:(r + 1) * QB]
    imp = p_r[:n_sel] + p_r[n_sel:]
    blk = lax.broadcasted_iota(jnp.int32, (n_sel, QB), 0)
    qp = qpos[:, :QB]
    imp = jnp.where(blk == qp // SEL_BLOCK, jnp.inf, jnp.where(blk * SEL_BLOCK <= qp, imp, -jnp.inf))
    chosen = [jnp.zeros((n_sel, QB), f32)]

    def pick(r, m, sel, chosen=chosen):
        chosen[0] = jnp.where(sel & (m > -jnp.inf), 1.0, chosen[0])

    _extract_top(imp, min(TOP_N, n_sel), pick)
    sel_scr[...] = jnp.concatenate([chosen[0]] * R, axis=1)

    m_scr[...] = jnp.full(m_scr.shape, -jnp.inf, f32)
    l_scr[...] = jnp.zeros(l_scr.shape, f32)
    acc_scr[...] = jnp.zeros(acc_scr.shape, f32)
    blk_per_tile = NSA_TK // SEL_BLOCK

    def tile_body(kt, carry):
        k0 = pl.multiple_of(kt * NSA_TK, NSA_TK)
        s = lax.dot_general(ks_ref[0, 0, pl.ds(k0, NSA_TK), :], q, nt, preferred_element_type=f32)
        selrows = sel_scr[pl.ds(pl.multiple_of(kt * blk_per_tile, blk_per_tile), blk_per_tile), :]
        kpos = k0 + lax.broadcasted_iota(jnp.int32, (NSA_TK, 1), 0)
        parts = []
        for j in range(blk_per_tile):
            rows = slice(j * SEL_BLOCK, (j + 1) * SEL_BLOCK)
            ok = (selrows[j:j + 1, :] > 0.0) & (kpos[rows] <= qpos)
            parts.append(jnp.where(ok, s[rows], -jnp.inf))
        _online_softmax_tile(jnp.concatenate(parts, axis=0), vst_ref[0, 0, :, pl.ds(k0, NSA_TK)], m_scr, l_scr, acc_scr)
        return carry

    lax.fori_loop(0, (i * QB + QB + NSA_TK - 1) // NSA_TK, tile_body, 0)
    o_s = acc_scr[...] * (1.0 / jnp.maximum(l_scr[...], 1e-30))

    w0 = pl.multiple_of(jnp.clip(i * QB + QB - NSA_WSPAN, 0, T - NSA_WSPAN), QB)
    s = lax.dot_general(kw_ref[0, 0, pl.ds(w0, NSA_WSPAN), :], q, nt, preferred_element_type=f32)
    kpos = w0 + lax.broadcasted_iota(jnp.int32, (NSA_WSPAN, 1), 0)
    p_w = _softmax_rows(s, (kpos <= qpos) & (kpos > qpos - WINDOW))
    o_w = jnp.dot(vwt_ref[0, 0, :, pl.ds(w0, NSA_WSPAN)], p_w.astype(bf16), preferred_element_type=f32)

    g = gt_ref[0, 0, 0]
    o = g[0:1] * o_c + g[1:2] * o_s + g[2:3] * o_w
    for r in range(R):
        o_ref[0, :, r * HEAD_DIM:(r + 1) * HEAD_DIM] = o[:, r * QB:(r + 1) * QB].T


def _nsa_prompt_attention(q, gates, new_rows, win, cmp_pos, cmp_w):
    B, T, _ = q.shape
    G, R, dh, QB = NSA_KV_GROUPS, NSA_R, HEAD_DIM, Q_BLOCK
    assert T % NSA_TK == 0 and T >= NSA_WSPAN and (T // SEL_BLOCK) % 8 == 0
    nqb = T // QB
    n_sel = T // SEL_BLOCK
    W = R * QB
    bf16 = jnp.bfloat16
    cmp = _nsa_compress(new_rows[:, :, :2], cmp_pos, cmp_w).reshape(B, 2, G, 2 * n_sel, dh)
    kc = cmp[:, 0].astype(bf16)
    vct = jnp.swapaxes(cmp[:, 1], -1, -2).astype(bf16)
    ks = jnp.transpose(new_rows[:, :, 2], (0, 2, 1, 3)).astype(bf16)
    vst = jnp.transpose(new_rows[:, :, 3], (0, 2, 3, 1)).astype(bf16)
    kw = jnp.transpose(win[:, :, 0], (0, 2, 1, 3)).astype(bf16)
    vwt = jnp.transpose(win[:, :, 1], (0, 2, 3, 1)).astype(bf16)
    gt = jnp.transpose(gates.reshape(B, nqb, QB, G, R, 3), (0, 3, 1, 5, 4, 2)).reshape(B, G, nqb, 3, W)
    gt = jnp.pad(gt, ((0, 0), (0, 0), (0, 0), (0, 5), (0, 0)))
    per_bg = lambda *shape: pl.BlockSpec((1, 1) + shape, lambda b, g, i: (b, g, 0, 0))
    return pl.pallas_call(
        _nsa_prompt_kernel,
        grid=(B, G, nqb),
        in_specs=[pl.BlockSpec((1, QB, R * dh), lambda b, g, i: (b, i, g)),
                  pl.BlockSpec((1, 1, 1, 8, W), lambda b, g, i: (b, g, i, 0, 0)),
                  per_bg(2 * n_sel, dh), per_bg(dh, 2 * n_sel), per_bg(T, dh), per_bg(dh, T), per_bg(T, dh), per_bg(dh, T)],
        out_specs=pl.BlockSpec((1, QB, R * dh), lambda b, g, i: (b, i, g)),
        out_shape=jax.ShapeDtypeStruct(q.shape, jnp.float32),
        scratch_shapes=[pltpu.VMEM((n_sel, W), jnp.float32), pltpu.VMEM((1, W), jnp.float32),
                        pltpu.VMEM((1, W), jnp.float32), pltpu.VMEM((dh, W), jnp.float32)],
        compiler_params=pltpu.CompilerParams(dimension_semantics=("parallel", "parallel", "arbitrary"),
                                             vmem_limit_bytes=VMEM_LIMIT),
        name="nsa_prompt",
    )(q, gt, kc, vct, ks, vst, kw, vwt)


_TN = (((0,), (0,)), ((), ()))
_NT = (((1,), (1,)), ((), ()))


def _nsa_sample_kernel(pt_ref, *refs, n_pages, past_len):
    pages = refs[:n_pages]
    q_ref, rows_ref, wnew_ref, wbuf_ref, gt_ref, pos_ref, w_ref, o_ref, sel_scr, xs_scr, pr_scr = refs[n_pages:]
    f32, bf16 = jnp.float32, jnp.bfloat16
    G, R, dh = NSA_KV_GROUPS, NSA_R, HEAD_DIM
    T = q_ref.shape[1]
    n_blk = past_len // SEL_BLOCK
    n_w = wbuf_ref.shape[1]
    lane = lax.broadcasted_iota(jnp.int32, (1, LANE), 1)
    qidx = lane & (T - 1)
    qpos = past_len + qidx
    new_row = lax.broadcasted_iota(jnp.int32, (LANE, 1), 0)
    new_ok = (new_row < T) & (new_row <= qidx)

    def new_tile(ref, col):
        return jnp.concatenate([ref[0, :, col * dh:(col + 1) * dh], jnp.zeros((LANE - T, dh), f32)], axis=0).astype(bf16)

    page = pages[0].shape[1] // (4 * G)
    per_page = page // CMP_BLOCK
    n_cmp = n_pages * per_page
    cmp = []
    for s in range(2):
        for g in range(G):
            for p, pg in enumerate(pages):
                for l in range(CMP_BLOCK):
                    xs_scr[g * n_cmp + p * per_page:g * n_cmp + (p + 1) * per_page, l * dh:(l + 1) * dh] = (
                        pg[0, pl.ds(l * 4 * G + s * G + g, per_page, stride=CMP_BLOCK * 4 * G), :])
        cmp.append(jnp.dot((xs_scr[...] + pos_ref[s]).astype(bf16), w_ref[s], preferred_element_type=f32))
    for g in range(G):
        q = jnp.concatenate([q_ref[0, :, (g * R + r) * dh:(g * R + r + 1) * dh] for r in range(R)]
                            + [jnp.zeros((LANE - R * T, dh), f32)], axis=0)
        q = (q * dh ** -0.5).astype(bf16)
        kc = cmp[0][g * 2 * n_blk:(g + 1) * 2 * n_blk]
        vc = cmp[1][g * 2 * n_blk:(g + 1) * 2 * n_blk]
        s_c = lax.dot_general(kc.astype(bf16), q, _NT, preferred_element_type=f32)
        p_c = _softmax_rows(s_c, jnp.full(s_c.shape, True))
        o_c = lax.dot_general(vc.astype(bf16), p_c.astype(bf16), _TN, preferred_element_type=f32)
        p_r = p_c
        for r in range(1, R):
            p_r = p_r + pltpu.roll(p_c, LANE - r * T, axis=1)
        pr_scr[...] = p_r
        imp = pr_scr[pl.ds(0, n_blk, stride=2), :] + pr_scr[pl.ds(1, n_blk, stride=2), :]
        chosen = [jnp.zeros((n_blk, LANE), f32)]

        def pick(_, m, sel, chosen=chosen):
            chosen[0] = jnp.where(sel, 1.0, chosen[0])

        _extract_top(imp, min(TOP_N, n_blk + 1) - 1, pick)
        ch = jnp.where(lane < T, chosen[0], 0.0)
        ch4 = ch
        for r in range(1, R):
            ch4 = ch4 + pltpu.roll(ch, r * T, axis=1)
        sel_scr[...] = ch4
        ks = jnp.concatenate([pg[0, pl.ds(2 * G + g, page, stride=4 * G), :] for pg in pages], axis=0).astype(bf16)
        vs = jnp.concatenate([pg[0, pl.ds(3 * G + g, page, stride=4 * G), :] for pg in pages], axis=0).astype(bf16)
        s_p = lax.dot_general(ks, q, _NT, preferred_element_type=f32)
        s_p = jnp.concatenate([jnp.where(sel_scr[j:j + 1, :] > 0.0, s_p[j * SEL_BLOCK:(j + 1) * SEL_BLOCK], -jnp.inf)
                               for j in range(n_blk)], axis=0)
        s_n = lax.dot_general(new_tile(rows_ref, 2 * G + g), q, _NT, preferred_element_type=f32)
        s_all = jnp.concatenate([s_p, jnp.where(new_ok, s_n, -jnp.inf)], axis=0)
        p_s = _softmax_rows(s_all, s_all > -jnp.inf).astype(bf16)
        o_s = (lax.dot_general(vs, p_s[:past_len], _TN, preferred_element_type=f32)
               + lax.dot_general(new_tile(rows_ref, 3 * G + g), p_s[past_len:], _TN, preferred_element_type=f32))
        kw = wbuf_ref[0, :, g * dh:(g + 1) * dh].astype(bf16)
        vw = wbuf_ref[0, :, (G + g) * dh:(G + g + 1) * dh].astype(bf16)
        s_w = lax.dot_general(kw, q, _NT, preferred_element_type=f32)
        wpos = past_len - n_w + lax.broadcasted_iota(jnp.int32, (n_w, 1), 0)
        s_w = jnp.where(wpos > qpos - WINDOW, s_w, -jnp.inf)
        s_n = lax.dot_general(new_tile(wnew_ref, g), q, _NT, preferred_element_type=f32)
        s_all = jnp.concatenate([s_w, jnp.where(new_ok, s_n, -jnp.inf)], axis=0)
        p_w = _softmax_rows(s_all, s_all > -jnp.inf).astype(bf16)
        o_w = (lax.dot_general(vw, p_w[:n_w], _TN, preferred_element_type=f32)
               + lax.dot_general(new_tile(wnew_ref, G + g), p_w[n_w:], _TN, preferred_element_type=f32))
        gt = gt_ref[0, g]
        o = (gt[0:1] * o_c + gt[1:2] * o_s + gt[2:3] * o_w).T
        for r in range(R):
            o_ref[0, :, (g * R + r) * dh:(g * R + r + 1) * dh] = o[r * T:(r + 1) * T]


def _nsa_sample_attention(q, gates, new_rows, win_new, cache, page_table, wbuf, cmp_pos, cmp_w):
    B, T, _ = q.shape
    G, R, dh = NSA_KV_GROUPS, NSA_R, HEAD_DIM
    n_pool, page = cache.shape[:2]
    n_pages = page_table.shape[1]
    past_len = n_pages * page
    n_w = wbuf.shape[1]
    assert T == 8 and R * T <= LANE and page % SEL_BLOCK == 0 and n_w + T > WINDOW
    assert past_len % SEL_BLOCK == 0 and T <= CMP_BLOCK
    row_w = 4 * G * dh
    per_page = page // SEL_BLOCK
    f32 = jnp.float32
    cache2 = cache.reshape(n_pool, page * 4 * G, dh)
    n_blk = past_len // SEL_BLOCK
    gt = jnp.transpose(gates.reshape(B, T, G, R, 3), (0, 2, 4, 3, 1)).reshape(B, G, 3, R * T)
    gt = jnp.pad(gt, ((0, 0), (0, 0), (0, 5), (0, LANE - R * T)))
    page_spec = lambda p: pl.BlockSpec((1, page * 4 * G, dh), lambda b, pt: (pt[b, p], 0, 0))
    seq = lambda *shape: pl.BlockSpec((1,) + shape, lambda b, pt: (b,) + (0,) * len(shape))
    whole = lambda *shape: pl.BlockSpec(shape, lambda b, pt: (0,) * len(shape))
    grid_spec = pltpu.PrefetchScalarGridSpec(
        num_scalar_prefetch=1,
        grid=(B,),
        in_specs=[page_spec(p) for p in range(n_pages)]
                 + [seq(T, R * G * dh), seq(T, row_w), seq(T, 2 * G * dh), seq(n_w, 2 * G * dh), seq(G, 8, LANE),
                    whole(2, 1, CMP_BLOCK * dh), whole(2, CMP_BLOCK * dh, dh)],
        out_specs=seq(T, R * G * dh),
        scratch_shapes=[pltpu.VMEM((n_blk, LANE), f32), pltpu.VMEM((G * 2 * n_blk, CMP_BLOCK * dh), f32),
                        pltpu.VMEM((2 * n_blk, LANE), f32)],
    )
    return pl.pallas_call(
        functools.partial(_nsa_sample_kernel, n_pages=n_pages, past_len=past_len),
        grid_spec=grid_spec,
        out_shape=jax.ShapeDtypeStruct(q.shape, f32),
        compiler_params=pltpu.CompilerParams(dimension_semantics=("parallel",), vmem_limit_bytes=VMEM_LIMIT),
        name="nsa_sample",
    )(page_table, *([cache2] * n_pages), q, new_rows.reshape(B, T, row_w), win_new.reshape(B, T, 2 * G * dh),
      wbuf.reshape(B, n_w, 2 * G * dh), gt, cmp_pos.reshape(2, 1, CMP_BLOCK * dh),
      cmp_w.reshape(2, CMP_BLOCK * dh, dh).astype(jnp.bfloat16))


def nsa_mixer(h, pos0, past, wbuf, w_in, cmp_pos, cmp_w, w_out):
    B, T, _ = h.shape
    G, dh = NSA_KV_GROUPS, HEAD_DIM
    qd = NSA_HEADS * dh
    kvd = 6 * G * dh
    pos = pos0 + jnp.arange(T, dtype=jnp.int32)
    proj = _mm(h, w_in)
    q = rope(proj[..., :qd].reshape(B, T, NSA_HEADS, dh), pos)
    kv = proj[..., qd:qd + kvd].reshape(B, T, 6, G, dh)
    gates = jax.nn.sigmoid(proj[..., qd + kvd:].reshape(B, T, NSA_HEADS, 3))
    keys = rope(kv[:, :, 0::2].reshape(B, T, 3 * G, dh), pos).reshape(B, T, 3, G, dh)
    kv = jnp.stack([keys[:, :, 0], kv[:, :, 1], keys[:, :, 1], kv[:, :, 3], keys[:, :, 2], kv[:, :, 5]], axis=2)
    new_rows = kv[:, :, :4]
    win = kv[:, :, 4:] if past is None else jnp.concatenate([wbuf, kv[:, :, 4:]], 1)
    if past is None:
        o = _nsa_prompt_attention(q.reshape(B, T, qd), gates, new_rows, win, cmp_pos, cmp_w)
    else:
        cache, page_table = past
        o = _nsa_sample_attention(q.reshape(B, T, qd), gates, new_rows, kv[:, :, 4:], cache, page_table, wbuf,
                                  cmp_pos, cmp_w)
    new_win = win[:, -min(WINDOW, win.shape[1]):]
    return _mm(o.reshape(B, T, D_MODEL), w_out), new_rows, new_win


DIFF_QB = 256
DIFF_TK = 512


def _online_softmax_tile(s, vt_tile, m_scr, l_scr, acc_scr):
    m_old = m_scr[...]
    m_new = jnp.maximum(m_old, jnp.max(s, axis=0, keepdims=True))
    m_safe = jnp.where(m_new == -jnp.inf, 0.0, m_new)
    alpha = jnp.exp(m_old - m_safe)
    p = jnp.exp(s - m_safe)
    l_scr[...] = alpha * l_scr[...] + jnp.sum(p, axis=0, keepdims=True)
    acc_scr[...] = alpha * acc_scr[...] + jnp.dot(vt_tile, p.astype(jnp.bfloat16), preferred_element_type=jnp.float32)
    m_scr[...] = m_new


def _diff_prompt_kernel(lam_ref, q_ref, k_ref, vt_ref, g_ref, o_ref, m_scr, l_scr, acc_scr, *, out_scale):
    f32, bf16 = jnp.float32, jnp.bfloat16
    i = pl.program_id(2)
    QB, TK, dd = DIFF_QB, DIFF_TK, DIFF_DH
    W = 2 * QB
    nt = (((1,), (1,)), ((), ()))
    qb = q_ref[0] * dd ** -0.5
    col = lax.broadcasted_iota(jnp.int32, (1, 2 * dd), 1)
    q = jnp.concatenate([jnp.where(col < dd, qb, 0.0), jnp.where(col >= dd, qb, 0.0)], axis=0).astype(bf16)
    lane = lax.broadcasted_iota(jnp.int32, (1, W), 1)
    qpos = i * QB + (lane & (QB - 1))
    m_scr[...] = jnp.full(m_scr.shape, -jnp.inf, f32)
    l_scr[...] = jnp.zeros(l_scr.shape, f32)
    acc_scr[...] = jnp.zeros(acc_scr.shape, f32)
    n_full = (i * QB) // TK

    def full_tile(kt, carry):
        k0 = pl.multiple_of(kt * TK, TK)
        s = lax.dot_general(k_ref[0, pl.ds(k0, TK), :], q, nt, preferred_element_type=f32)
        _online_softmax_tile(s, vt_ref[0, 0, :, pl.ds(k0, TK)], m_scr, l_scr, acc_scr)
        return carry

    lax.fori_loop(0, n_full, full_tile, 0)
    k0 = pl.multiple_of(n_full * TK, TK)
    s = lax.dot_general(k_ref[0, pl.ds(k0, TK), :], q, nt, preferred_element_type=f32)
    kpos = k0 + lax.broadcasted_iota(jnp.int32, (TK, 1), 0)
    _online_softmax_tile(jnp.where(kpos <= qpos, s, -jnp.inf), vt_ref[0, 0, :, pl.ds(k0, TK)], m_scr, l_scr, acc_scr)
    o = acc_scr[...] * (1.0 / jnp.maximum(l_scr[...], 1e-30))
    o = o[:, :QB] - lam_ref[0, 0] * o[:, QB:]
    o = o * lax.rsqrt(jnp.mean(o * o, axis=0, keepdims=True) + EPS) * (g_ref[...] * out_scale)
    o_ref[0] = o.T


def _diff_prompt_attention(q, k, v, lam, subln_g, out_scale):
    B, T, _ = q.shape
    H, dv = DIFF_HEADS, 2 * DIFF_DH
    assert T % DIFF_TK == 0 and DIFF_TK % DIFF_QB == 0
    bf16 = jnp.bfloat16
    vt = jnp.transpose(v.reshape(B, T, H, dv), (0, 2, 3, 1)).astype(bf16)
    W = 2 * DIFF_QB
    return pl.pallas_call(
        functools.partial(_diff_prompt_kernel, out_scale=out_scale),
        grid=(B, H, T // DIFF_QB),
        in_specs=[pl.BlockSpec(memory_space=pltpu.SMEM),
                  pl.BlockSpec((1, DIFF_QB, dv), lambda b, h, i: (b, i, h)),
                  pl.BlockSpec((1, T, dv), lambda b, h, i: (b, 0, h)),
                  pl.BlockSpec((1, 1, dv, T), lambda b, h, i: (b, h, 0, 0)),
                  pl.BlockSpec((dv, 1), lambda b, h, i: (0, 0))],
        out_specs=pl.BlockSpec((1, DIFF_QB, dv), lambda b, h, i: (b, i, h)),
        out_shape=jax.ShapeDtypeStruct(q.shape, jnp.float32),
        scratch_shapes=[pltpu.VMEM((1, W), jnp.float32), pltpu.VMEM((1, W), jnp.float32), pltpu.VMEM((dv, W), jnp.float32)],
        compiler_params=pltpu.CompilerParams(dimension_semantics=("parallel", "parallel", "arbitrary"),
                                             vmem_limit_bytes=VMEM_LIMIT),
        name="diff_prompt",
    )(lam.reshape(1, 1), q, k.astype(bf16), vt, subln_g.reshape(dv, 1))


def _diff_sample_kernel(pt_ref, lam_ref, *refs, n_pages, out_scale):
    pages = refs[:n_pages]
    q_ref, new_ref, g_ref, o_ref = refs[n_pages:]
    f32, bf16 = jnp.float32, jnp.bfloat16
    H, dd = DIFF_HEADS, DIFF_DH
    T = q_ref.shape[1]
    D = H * 2 * dd
    qt = jnp.concatenate([q_ref[0] * dd ** -0.5] * (LANE // T), axis=0)
    rowi = lax.broadcasted_iota(jnp.int32, (LANE, D), 0)
    coli = lax.broadcasted_iota(jnp.int32, (LANE, D), 1)
    qbd = jnp.where(coli // dd == rowi // T, qt, 0.0).astype(bf16)
    lane = lax.broadcasted_iota(jnp.int32, (1, LANE), 1)
    qidx = lane & (T - 1)
    s = [lax.dot_general(pg[0, :, 0:D].astype(bf16), qbd, _NT, preferred_element_type=f32) for pg in pages]
    new_row = lax.broadcasted_iota(jnp.int32, (LANE, 1), 0)
    pad = jnp.zeros((LANE - T, D), f32)
    k_new = jnp.concatenate([new_ref[0, :, 0:D], pad], axis=0).astype(bf16)
    v_new = jnp.concatenate([new_ref[0, :, D:2 * D], pad], axis=0).astype(bf16)
    s_n = lax.dot_general(k_new, qbd, _NT, preferred_element_type=f32)
    s_n = jnp.where((new_row < T) & (new_row <= qidx), s_n, -jnp.inf)
    m = jnp.max(s_n, axis=0, keepdims=True)
    for sp in s:
        m = jnp.maximum(m, jnp.max(sp, axis=0, keepdims=True))
    p_n = jnp.exp(s_n - m)
    l = jnp.sum(p_n, axis=0, keepdims=True)
    acc = lax.dot_general(v_new, p_n.astype(bf16), _TN, preferred_element_type=f32)
    for pg, sp in zip(pages, s):
        p = jnp.exp(sp - m)
        l = l + jnp.sum(p, axis=0, keepdims=True)
        acc = acc + lax.dot_general(pg[0, :, D:2 * D].astype(bf16), p.astype(bf16), _TN, preferred_element_type=f32)
    inv_l = 1.0 / jnp.maximum(l, 1e-30)
    lam = lam_ref[0, 0]
    for h in range(H):
        a = acc[h * 2 * dd:(h + 1) * 2 * dd, :] * inv_l
        d = a - lam * pltpu.roll(a, LANE - T, axis=1)
        if h:
            d = pltpu.roll(d, LANE - h * 2 * T, axis=1)
        d = d * lax.rsqrt(jnp.mean(d * d, axis=0, keepdims=True) + EPS) * (g_ref[...] * out_scale)
        o_ref[0, :, h * 2 * dd:(h + 1) * 2 * dd] = d.T[0:T]


def _diff_sample_attention(q, new_rows, cache, page_table, lam, subln_g, out_scale):
    B, T, D = q.shape
    n_pool, page = cache.shape[:2]
    n_pages = page_table.shape[1]
    assert T == 8 and 2 * DIFF_HEADS * T == LANE
    f32 = jnp.float32
    seq = lambda *shape: pl.BlockSpec((1,) + shape, lambda b, pt: (b,) + (0,) * len(shape))
    page_spec = lambda p: pl.BlockSpec((1, page, 2 * D), lambda b, pt: (pt[b, p], 0, 0))
    grid_spec = pltpu.PrefetchScalarGridSpec(
        num_scalar_prefetch=1,
        grid=(B,),
        in_specs=[pl.BlockSpec(memory_space=pltpu.SMEM)] + [page_spec(p) for p in range(n_pages)]
                 + [seq(T, D), seq(T, 2 * D), pl.BlockSpec((2 * DIFF_DH, 1), lambda b, pt: (0, 0))],
        out_specs=seq(T, D),
    )
    return pl.pallas_call(
        functools.partial(_diff_sample_kernel, n_pages=n_pages, out_scale=out_scale),
        grid_spec=grid_spec,
        out_shape=jax.ShapeDtypeStruct(q.shape, f32),
        compiler_params=pltpu.CompilerParams(dimension_semantics=("parallel",), vmem_limit_bytes=56 * 1024 * 1024),
        name="diff_sample",
    )(page_table, lam.reshape(1, 1), *([cache.reshape(n_pool, page, 2 * D)] * n_pages), q,
      new_rows.reshape(B, T, 2 * D), subln_g.reshape(2 * DIFF_DH, 1))


def diff_mixer(h, pos0, past, layer_idx, w_in, lq1, lk1, lq2, lk2, subln_g, w_out):
    B, T, _ = h.shape
    H, dd = DIFF_HEADS, DIFF_DH
    f32 = jnp.float32
    dt = h.dtype
    pos = pos0 + jnp.arange(T, dtype=jnp.int32)
    proj = _mm(h, w_in)
    q = rope(proj[..., :D_MODEL].reshape(B, T, 2 * H, dd), pos)
    k = rope(proj[..., D_MODEL:2 * D_MODEL].reshape(B, T, 2 * H, dd), pos)
    v = proj[..., 2 * D_MODEL:].reshape(B, T, 2 * H, dd)
    new_rows = jnp.stack([k, v], 2)
    lam_init = 0.8 - 0.6 * math.exp(-0.3 * layer_idx)
    lam = (jnp.exp(jnp.sum(lq1.astype(f32) * lk1.astype(f32))) - jnp.exp(jnp.sum(lq2.astype(f32) * lk2.astype(f32))) + lam_init)
    if past is None:
        o = _diff_prompt_attention(q.reshape(B, T, D_MODEL), k.reshape(B, T, D_MODEL), v.reshape(B, T, D_MODEL),
                                   lam, subln_g, 1.0 - lam_init)
        return _mm(o, w_out), new_rows
    cache, page_table = past
    o = _diff_sample_attention(q.reshape(B, T, D_MODEL), new_rows, cache, page_table, lam, subln_g, 1.0 - lam_init)
    return _mm(o, w_out), new_rows


PEER_TB = 512
PEER_ET = 1024
_PEER_CAND = [(a, b) for a in range(PEER_TOPK) for b in range(PEER_TOPK) if (a + 1) * (b + 1) <= PEER_TOPK]


def _extract_top(s, n_iter, on_pick, break_ties=True):
    rows = s.shape[0]
    iota = lax.broadcasted_iota(jnp.int32, s.shape, 0)
    for r in range(n_iter):
        m = jnp.max(s, axis=0, keepdims=True)
        if break_ties:
            idx = jnp.min(jnp.where(s == m, iota, rows), axis=0, keepdims=True)
            sel = iota == idx
        else:
            sel = s == m
        on_pick(r, m, sel)
        s = jnp.where(sel, -jnp.inf, s)


def _peer_route_kernel(x_ref, shift_ref, scale_ref, g_ref, wq_ref, k1_ref, k2_ref, aof_ref,
                       h_ref, c1_ref, cnt1_ref, rank2_ref, e2_ref, q_scr, v1_scr, v2_scr, cand_scr):
    f32 = jnp.float32
    x = x_ref[...]
    tb = x.shape[0] * x.shape[1]
    y = x * lax.rsqrt(jnp.mean(x * x, -1, keepdims=True) + EPS) * g_ref[...]
    h = (y * (1.0 + scale_ref[...]) + shift_ref[...]).reshape(tb, D_MODEL)
    hb = h.astype(jnp.bfloat16)
    h_ref[...] = hb
    q = jnp.dot(hb, wq_ref[...], preferred_element_type=f32).astype(jnp.bfloat16)
    for hh in range(PEER_HEADS):
        q_scr[hh] = q[:, hh * PEER_QDIM:(hh + 1) * PEER_QDIM]
    nt = (((1,), (1,)), ((), ()))
    n_chunk = tb // LANE
    cand_scr[...] = jnp.full(cand_scr.shape, -jnp.inf, f32)

    def process(it, exact):
        hh = it // n_chunk
        c0 = pl.multiple_of((it % n_chunk) * LANE, LANE)
        qc = q_scr[hh, pl.ds(c0, LANE), :]
        s1 = lax.dot_general(k1_ref[hh], qc, nt, preferred_element_type=f32)
        s2 = lax.dot_general(k2_ref[hh], qc, nt, preferred_element_type=f32)
        ranks = []
        for s, v_scr in ((s1, v1_scr), (s2, v2_scr)):
            rank = [jnp.full(s.shape, float(N_KEYS), f32)]

            def pick(r, m, sel, v_scr=v_scr, rank=rank):
                v_scr[r:r + 1, :] = m
                rank[0] = jnp.where(sel, float(r), rank[0])

            _extract_top(s, PEER_TOPK, pick, break_ties=exact)
            ranks.append(rank[0])
        for k, (a, b) in enumerate(_PEER_CAND):
            cand_scr[k:k + 1, :] = v1_scr[a:a + 1, :] + v2_scr[b:b + 1, :]
        top1 = v1_scr[0:1, :]
        top2 = v2_scr[0:1, :]
        top_val = top1 + top2
        a_of = aof_ref[...]
        iota16 = lax.broadcasted_iota(jnp.int32, (PEER_TOPK, LANE), 0)
        st = {"z": jnp.zeros((1, LANE), f32), "cnt": jnp.zeros((PEER_TOPK, LANE), f32), "taken": jnp.zeros((1, LANE), f32)}

        def pick_c(r, m, sel, st=st):
            st["z"] = st["z"] + jnp.exp(m - top_val)
            a_sel = jnp.max(jnp.where(sel, a_of, 0), axis=0, keepdims=True)
            st["cnt"] = st["cnt"] + (iota16 == a_sel).astype(f32)
            if not exact:
                st["taken"] = st["taken"] + jnp.sum(sel.astype(f32), axis=0, keepdims=True)

        _extract_top(cand_scr[...], PEER_TOPK, pick_c, break_ties=exact)
        inv_z = 1.0 / st["z"]
        cnt1 = jnp.zeros(s1.shape, f32)
        for a in range(PEER_TOPK):
            cnt1 = jnp.where(ranks[0] == float(a), st["cnt"][a:a + 1], cnt1)
        c1_ref[hh, :, pl.ds(c0, LANE)] = jnp.exp(s1 - top1) * inv_z
        cnt1_ref[hh, :, pl.ds(c0, LANE)] = cnt1
        rank2_ref[hh, :, pl.ds(c0, LANE)] = ranks[1]
        e2_ref[hh, :, pl.ds(c0, LANE)] = jnp.exp(s2 - top2)
        if exact:
            return None
        want = float(PEER_TOPK)
        tied = st["taken"] != want
        for rank in ranks:
            tied = tied | (jnp.sum((rank < want).astype(f32), axis=0, keepdims=True) != want)
        return tied

    def body(it, carry):
        tied = process(it, exact=False)

        @pl.when(jnp.max(tied.astype(f32)) > 0.0)
        def _():
            process(it, exact=True)

        return carry

    lax.fori_loop(0, PEER_HEADS * n_chunk, body, 0)


def _peer_dense_kernel(hb_ref, u_ref, vt_ref, c1_ref, cnt1_ref, rank2_ref, e2_ref, x_ref, gm_ref,
                       o_ref, ht0_ref, ht1_ref, gh0_ref, gh1_ref, acc_ref):
    f32, bf16 = jnp.float32, jnp.bfloat16
    s = pl.program_id(1)
    tb = hb_ref.shape[0]
    n_rows = PEER_ET // N_KEYS
    sub = 16
    nt = (((1,), (1,)), ((), ()))

    @pl.when(s == 0)
    def _():
        acc_ref[...] = jnp.zeros_like(acc_ref)
        ht1_ref[...] = jnp.zeros_like(ht1_ref)
        gh0_ref[...] = jnp.zeros_like(gh0_ref)
        gh1_ref[...] = jnp.zeros_like(gh1_ref)

    def stages(ht_w, ht_r, gh_w, gh_r):
        ht_w[...] = lax.dot_general(u_ref[...], hb_ref[...], nt, preferred_element_type=f32)
        for c0 in range(0, tb, LANE):
            lanes = slice(c0, c0 + LANE)
            for s0 in range(0, N_KEYS, sub):
                g = [jnp.zeros((sub, LANE), f32) for _ in range(n_rows)]
                for hh in range(PEER_HEADS):
                    rk = rank2_ref[hh, s0:s0 + sub, lanes]
                    ev = e2_ref[hh, s0:s0 + sub, lanes]
                    for r in range(n_rows):
                        g[r] = g[r] + jnp.where(rk < cnt1_ref[hh, r:r + 1, lanes], ev * c1_ref[hh, r:r + 1, lanes], 0.0)
                for r in range(n_rows):
                    rows = slice(r * N_KEYS + s0, r * N_KEYS + s0 + sub)
                    pre = ht_r[rows, lanes]
                    act = 0.5 * pre * (1.0 + lax.erf(pre * (2.0 ** -0.5)))
                    gh_w[rows, lanes] = (g[r] * act).astype(bf16)
        acc_ref[...] += jnp.dot(vt_ref[...], gh_r[...], preferred_element_type=f32)

    @pl.when(s % 2 == 0)
    def _():
        stages(ht0_ref, ht1_ref, gh1_ref, gh0_ref)

    @pl.when(s % 2 == 1)
    def _():
        stages(ht1_ref, ht0_ref, gh0_ref, gh1_ref)

    @pl.when(s == pl.num_programs(1) - 1)
    def _():
        upd = acc_ref[...].T.reshape(x_ref.shape)
        o_ref[...] = x_ref[...] + gm_ref[...] * upd


def _peer_sublayer(x, shift, scale, gate, norm_g, wq_b, k1p, k2p, u_b, vt_b):
    B, T, D = x.shape
    n = B * T
    tb = PEER_TB
    assert n % tb == 0
    if T % tb == 0:
        nbs, tper, per = 1, tb, T // tb
        xmap = lambda i, *_: (i // per, i % per, 0)
        mmap = lambda i, *_: (i // per, 0, 0)
    else:
        assert tb % T == 0 and T % 8 == 0
        nbs, tper = tb // T, T
        xmap = lambda i, *_: (i, 0, 0)
        mmap = lambda i, *_: (i, 0, 0)
    nblk = n // tb
    f32 = jnp.float32
    x_spec = pl.BlockSpec((nbs, tper, D), xmap)
    m_spec = pl.BlockSpec((nbs, 1, D), mmap)
    n_cand_pad = -(-len(_PEER_CAND) // 8) * 8
    a_of = jnp.asarray(np.broadcast_to(np.array([a for a, _ in _PEER_CAND] + [0] * (n_cand_pad - len(_PEER_CAND)),
                                                np.int32)[:, None], (n_cand_pad, LANE)))
    route_shape = lambda dt: jax.ShapeDtypeStruct((PEER_HEADS, N_KEYS, n), dt)
    route_spec = pl.BlockSpec((PEER_HEADS, N_KEYS, tb), lambda i: (0, 0, i))
    hb, c1, cnt1, rank2, e2 = pl.pallas_call(
        _peer_route_kernel,
        grid=(nblk,),
        in_specs=[x_spec, m_spec, m_spec,
                  pl.BlockSpec((1, D), lambda i: (0, 0)),
                  pl.BlockSpec((D, PEER_HEADS * PEER_QDIM), lambda i: (0, 0)),
                  pl.BlockSpec((PEER_HEADS, N_KEYS, PEER_QDIM), lambda i: (0, 0, 0)),
                  pl.BlockSpec((PEER_HEADS, N_KEYS, PEER_QDIM), lambda i: (0, 0, 0)),
                  pl.BlockSpec((n_cand_pad, LANE), lambda i: (0, 0))],
        out_specs=[pl.BlockSpec((tb, D), lambda i: (i, 0)), route_spec, route_spec, route_spec, route_spec],
        out_shape=[jax.ShapeDtypeStruct((n, D), jnp.bfloat16)] + [route_shape(f32)] * 4,
        scratch_shapes=[pltpu.VMEM((PEER_HEADS, tb, PEER_QDIM), jnp.bfloat16), pltpu.VMEM((PEER_TOPK, LANE), f32),
                        pltpu.VMEM((PEER_TOPK, LANE), f32), pltpu.VMEM((n_cand_pad, LANE), f32)],
        compiler_params=pltpu.CompilerParams(dimension_semantics=("parallel",), vmem_limit_bytes=VMEM_LIMIT),
        name="peer_route",
    )(x, shift, scale, norm_g.reshape(1, D), wq_b, k1p, k2p, a_of)

    rows = PEER_ET // N_KEYS
    n_tiles = N_EXPERTS // PEER_ET
    tile = lambda s, lag: jnp.clip(s - lag, 0, n_tiles - 1)
    sub_spec = pl.BlockSpec((PEER_HEADS, rows, tb), lambda i, s: (0, tile(s, 1), i))
    full_spec = pl.BlockSpec((PEER_HEADS, N_KEYS, tb), lambda i, s: (0, 0, i))
    return pl.pallas_call(
        _peer_dense_kernel,
        grid=(nblk, n_tiles + 2),
        in_specs=[pl.BlockSpec((tb, D), lambda i, s: (i, 0)),
                  pl.BlockSpec((PEER_ET, D), lambda i, s: (tile(s, 0), 0)),
                  pl.BlockSpec((D, PEER_ET), lambda i, s: (0, tile(s, 2))),
                  sub_spec, sub_spec, full_spec, full_spec,
                  pl.BlockSpec((nbs, tper, D), lambda i, s: xmap(i)),
                  pl.BlockSpec((nbs, 1, D), lambda i, s: mmap(i))],
        out_specs=pl.BlockSpec((nbs, tper, D), lambda i, s: xmap(i)),
        out_shape=jax.ShapeDtypeStruct(x.shape, x.dtype),
        scratch_shapes=[pltpu.VMEM((PEER_ET, tb), f32), pltpu.VMEM((PEER_ET, tb), f32),
                        pltpu.VMEM((PEER_ET, tb), jnp.bfloat16), pltpu.VMEM((PEER_ET, tb), jnp.bfloat16),
                        pltpu.VMEM((D, tb), f32)],
        compiler_params=pltpu.CompilerParams(dimension_semantics=("parallel", "arbitrary"), vmem_limit_bytes=VMEM_LIMIT),
        name="peer_dense",
    )(hb, u_b, vt_b, c1, cnt1, rank2, e2, x, gate)


def _peer_weights(w_q, k1, k2, u_tab, v_tab):
    bf16 = jnp.bfloat16
    half = PEER_QDIM // 2
    k1p = jnp.pad(k1, ((0, 0), (0, 0), (0, half))).astype(bf16)
    k2p = jnp.pad(k2, ((0, 0), (0, 0), (half, 0))).astype(bf16)
    return w_q.astype(bf16), k1p, k2p, u_tab.astype(bf16), v_tab.astype(bf16).T


def _layer_pool(cache, j, page_table):
    return cache.reshape((-1,) + cache.shape[2:]), page_table + j * cache.shape[1]


def kernel(x_prompt, x_sample, cache_nsa_kv, cache_diff_kv, state_nsa_window, state_gdn_S, state_gdn_conv,
           page_table, c_prompt, c_sample, ada_w, ada_b, norm_mix_g, norm_ffn_g, final_norm_g,
           gdn_w_in, gdn_conv_w, gdn_a_log, gdn_dt_bias, gdn_norm_g, gdn_w_out,
           nsa_w_in, nsa_cmp_pos, nsa_cmp_w, nsa_w_out,
           diff_w_in, diff_lq1, diff_lk1, diff_lq2, diff_lk2, diff_subln_g, diff_w_out,
           peer_w_q, peer_k1, peer_k2, peer_u, peer_v):
    past_len = page_table.shape[1] * cache_nsa_kv.shape[2]
    peer_w = [_peer_weights(peer_w_q[i], peer_k1[i], peer_k2[i], peer_u[i], peer_v[i]) for i in range(DEPTH)]

    def trunk(x, c, sample):
        B, T, _ = x.shape
        pos0 = past_len if sample else 0
        cs = jax.nn.silu(c)
        new_S, new_conv, new_nsa_kv, new_nsa_win, new_diff_kv = [], [], [], [], []
        for i in range(DEPTH):
            mod = (cs @ ada_w[i] + ada_b[i]).reshape(B, 6, 1, D_MODEL)
            h = rmsnorm(x, norm_mix_g[i]) * (1.0 + mod[:, 1]) + mod[:, 0]
            j = i // N_MIXERS
            kind = i % N_MIXERS
            if kind == 0:
                if sample:
                    S0 = state_gdn_S[j].astype(jnp.float32)
                    buf = state_gdn_conv[j]
                else:
                    S0 = jnp.zeros((B, GDN_HEADS, GDN_DK, GDN_DV), jnp.float32)
                    buf = jnp.zeros((B, CONV_W - 1, GDN_HEADS * (2 * GDN_DK + GDN_DV)), x.dtype)
                m, buf_n, S_n = gdn_mixer(x, mod[:, 0], mod[:, 1], norm_mix_g[i], buf, S0, gdn_w_in[j], gdn_conv_w[j],
                                          gdn_a_log[j], gdn_dt_bias[j], gdn_norm_g[j], gdn_w_out[j])
                new_S.append(S_n.astype(x.dtype))
                new_conv.append(buf_n)
            elif kind == 1:
                past = _layer_pool(cache_nsa_kv, j, page_table) if sample else None
                wbuf = state_nsa_window[j] if sample else None
                m, kv_n, win_n = nsa_mixer(h, pos0, past, wbuf, nsa_w_in[j], nsa_cmp_pos[j], nsa_cmp_w[j], nsa_w_out[j])
                new_nsa_kv.append(kv_n)
                new_nsa_win.append(win_n)
            else:
                past = _layer_pool(cache_diff_kv, j, page_table) if sample else None
                m, kv_n = diff_mixer(h, pos0, past, i, diff_w_in[j], diff_lq1[j], diff_lk1[j], diff_lq2[j], diff_lk2[j], diff_subln_g[j], diff_w_out[j])
                new_diff_kv.append(kv_n)
            x = x + mod[:, 2] * m
            x = _peer_sublayer(x, mod[:, 3], mod[:, 4], mod[:, 5], norm_ffn_g[i], *peer_w[i])
        y = rmsnorm(x, final_norm_g)
        return y, jnp.stack(new_S), jnp.stack(new_conv), jnp.stack(new_nsa_kv), jnp.stack(new_nsa_win), jnp.stack(new_diff_kv)

    y_prompt, p_gdn_S, p_gdn_conv, p_nsa_kv, p_nsa_win, p_diff_kv = trunk(x_prompt, c_prompt, False)
    y_sample, s_gdn_S, s_gdn_conv, s_nsa_kv, s_nsa_win, s_diff_kv = trunk(x_sample, c_sample, True)
    return (y_prompt, y_sample, p_gdn_S, p_gdn_conv, p_nsa_kv, p_nsa_win, p_diff_kv, s_gdn_S, s_gdn_conv, s_nsa_kv, s_nsa_win, s_diff_kv)
```

```python
import functools
import math

import jax
import jax.numpy as jnp
import numpy as np
from jax import lax
from jax.experimental import pallas as pl
from jax.experimental.pallas import tpu as pltpu

D_MODEL = 1024
DEPTH = 4
N_MIXERS = 3
HEAD_DIM = 128
GDN_HEADS = D_MODEL // HEAD_DIM
GDN_DK = HEAD_DIM
GDN_DV = HEAD_DIM
CONV_W = 4
GDN_CHUNK = 64
NSA_HEADS = D_MODEL // HEAD_DIM
NSA_KV_GROUPS = 2
CMP_BLOCK = 32
SEL_BLOCK = 64
TOP_N = 16
WINDOW = 512
Q_BLOCK = 128
DIFF_HEADS = D_MODEL // HEAD_DIM
DIFF_DH = D_MODEL // DIFF_HEADS // 2
PEER_HEADS = 8
N_KEYS = 128
N_EXPERTS = N_KEYS * N_KEYS
PEER_TOPK = 16
PEER_QDIM = 128
PEER_TOKEN_BLOCK = 256
ROPE_THETA = 10000.0
EPS = 1e-6

LANE = 128
VMEM_LIMIT = 48 * 1024 * 1024


def _mm_kernel(x_ref, w_ref, o_ref):
    o_ref[...] = jnp.dot(x_ref[...].astype(jnp.bfloat16), w_ref[...], preferred_element_type=jnp.float32)


def _mm(x, w, tm=256):
    lead = x.shape[:-1]
    K = x.shape[-1]
    N = w.shape[1]
    x2 = x.reshape(-1, K)
    M = x2.shape[0]
    n_pad = -(-N // 256) * 256
    wb = jnp.pad(w, ((0, 0), (0, n_pad - N))).astype(jnp.bfloat16)
    tm = min(tm, M)
    assert M % tm == 0
    out = pl.pallas_call(
        _mm_kernel,
        grid=(M // tm,),
        in_specs=[pl.BlockSpec((tm, K), lambda i: (i, 0)), pl.BlockSpec((K, n_pad), lambda i: (0, 0))],
        out_specs=pl.BlockSpec((tm, n_pad), lambda i: (i, 0)),
        out_shape=jax.ShapeDtypeStruct((M, n_pad), jnp.float32),
        compiler_params=pltpu.CompilerParams(dimension_semantics=("parallel",), vmem_limit_bytes=VMEM_LIMIT),
        name="proj_matmul",
    )(x2, wb)
    return out[:, :N].reshape(*lead, N)


def rmsnorm(x, g):
    xf = x.astype(jnp.float32)
    y = xf * lax.rsqrt(jnp.mean(xf * xf, -1, keepdims=True) + EPS)
    return (y * g.astype(jnp.float32)).astype(x.dtype)


def l2norm(x):
    xf = x.astype(jnp.float32)
    return (xf * lax.rsqrt(jnp.sum(xf * xf, -1, keepdims=True) + EPS)).astype(x.dtype)


def rope(x, pos):
    half = x.shape[-1] // 2
    inv = ROPE_THETA ** (-jnp.arange(half, dtype=jnp.float32) / half)
    ang = pos.astype(jnp.float32)[:, None] * inv[None, :]
    cos = jnp.cos(ang)[:, None, :]
    sin = jnp.sin(ang)[:, None, :]
    xf = x.astype(jnp.float32)
    x1, x2 = xf[..., :half], xf[..., half:]
    return jnp.concatenate([x1 * cos - x2 * sin, x2 * cos + x1 * sin], -1).astype(x.dtype)


def masked_softmax(s, mask):
    s = jnp.where(mask, s.astype(jnp.float32), -jnp.inf)
    m = jnp.max(s, -1, keepdims=True)
    m = jnp.where(jnp.isfinite(m), m, 0.0)
    p = jnp.exp(s - m)
    return p / jnp.maximum(jnp.sum(p, -1, keepdims=True), 1e-30)


def gather_pages(pool, page_table):
    g = pool[page_table]
    return g.reshape(g.shape[0], g.shape[1] * g.shape[2], *pool.shape[2:])


def causal_conv(x, buf, w):
    T = x.shape[1]
    xx = jnp.concatenate([buf, x], 1)
    y = xx[:, 0:T] * w[0]
    for j in range(1, CONV_W):
        y = y + xx[:, j:j + T] * w[j]
    return jax.nn.silu(y), xx[:, -(CONV_W - 1):]


def gated_delta_rule(q, k, v, g, beta, S0):
    f32 = jnp.float32
    B, T, H, dk = q.shape
    dv = v.shape[-1]
    C = GDN_CHUNK
    Tp = -(-T // C) * C

    def prep(a):
        a = a.astype(f32)
        a = jnp.pad(a, [(0, 0), (0, Tp - T)] + [(0, 0)] * (a.ndim - 2))
        a = a.reshape(B, Tp // C, C, *a.shape[2:])
        return jnp.moveaxis(a, (1, 3), (0, 2))

    qc = prep(q) * dk ** -0.5
    kc, vc, gc, bc = prep(k), prep(v), prep(g), prep(beta)
    gcum = jnp.cumsum(gc, -1)
    idx = jnp.arange(C)
    incl = idx[:, None] >= idx[None, :]
    strict = idx[:, None] > idx[None, :]
    decay = jnp.exp(jnp.where(incl, gcum[..., :, None] - gcum[..., None, :], -jnp.inf))
    kb = kc * bc[..., None]
    L = jnp.where(strict, jnp.einsum('nbhid,nbhjd->nbhij', kb, kc) * decay, 0.0)
    eye = jnp.eye(C, dtype=f32)
    A = eye + L
    Tinv = lax.linalg.triangular_solve(A, jnp.broadcast_to(eye, A.shape), left_side=True, lower=True, unit_diagonal=True)
    u = Tinv @ (vc * bc[..., None])
    w = Tinv @ (kb * jnp.exp(gcum)[..., None])
    a_intra = jnp.einsum('nbhid,nbhjd->nbhij', qc, kc) * decay

    def step(S, xs):
        q_i, k_i, u_i, w_i, a_i, g_i = xs
        v_new = u_i - w_i @ S
        o = (q_i * jnp.exp(g_i)[..., None]) @ S + a_i @ v_new
        g_last = g_i[..., -1:]
        S = S * jnp.exp(g_last)[..., None] + jnp.einsum('bhck,bhcv->bhkv', k_i * jnp.exp(g_last - g_i)[..., None], v_new)
        return S, o

    S, o = lax.scan(step, S0.astype(f32), (qc, kc, u, w, a_intra, gcum))
    o = jnp.moveaxis(o, (0, 2), (1, 3)).reshape(B, Tp, H, dv)[:, :T]
    return o, S


GDN_TM = 256
_CARRY = 8


def _gdn_in_kernel(x_ref, shift_ref, scale_ref, ng_ref, w_ref, cw_ref, buf_ref, dyn_ref,
                   q_ref, k_ref, v_ref, z_ref, gb_ref, conv_ref, carry):
    f32 = jnp.float32
    i = pl.program_id(1)
    H, dk, dv = GDN_HEADS, GDN_DK, GDN_DV
    n_qkv = H * (2 * dk + dv)
    tm = x_ref.shape[1]

    @pl.when(i == 0)
    def _():
        carry[...] = buf_ref[0]

    x = x_ref[0]
    y = x * lax.rsqrt(jnp.mean(x * x, -1, keepdims=True) + EPS) * ng_ref[...]
    h = y * (1.0 + scale_ref[0]) + shift_ref[0]
    proj = jnp.dot(h.astype(jnp.bfloat16), w_ref[...], preferred_element_type=f32)
    xx = jnp.concatenate([carry[...], proj[:, :n_qkv]], axis=0)
    first = _CARRY - (CONV_W - 1)
    conv = xx[first:first + tm] * cw_ref[0:1, :]
    for j in range(1, CONV_W):
        conv = conv + xx[first + j:first + j + tm] * cw_ref[j:j + 1, :]
    act = conv * jax.nn.sigmoid(conv)
    carry[...] = xx[tm:tm + _CARRY]
    for hh in range(H):
        for ref, base in ((q_ref, 0), (k_ref, H * dk)):
            a = act[:, base + hh * dk:base + (hh + 1) * dk]
            ref[0, :, hh * dk:(hh + 1) * dk] = a * lax.rsqrt(jnp.sum(a * a, -1, keepdims=True) + EPS)
    v_ref[0] = act[:, 2 * H * dk:]
    z_ref[0] = proj[:, n_qkv:n_qkv + H * dv]
    tail = proj[:, n_qkv + H * dv:n_qkv + H * dv + LANE]
    t2 = tail + dyn_ref[1:2, :]
    softplus = jnp.maximum(t2, 0.0) + jnp.log(1.0 + jnp.exp(-jnp.abs(t2)))
    lane = lax.broadcasted_iota(jnp.int32, tail.shape, 1)
    gb = jnp.where(lane < H, jax.nn.sigmoid(tail), dyn_ref[0:1, :] * softplus)
    gb_ref[0] = gb[:, :2 * H]

    @pl.when(i == pl.num_programs(1) - 1)
    def _():
        conv_ref[0] = xx[tm:tm + _CARRY]


def gdn_mixer(x, shift, scale, mix_g, conv_buf, S0, w_in, conv_w, a_log, dt_bias, norm_g, w_out):
    B, T, D = x.shape
    H, dk, dv = GDN_HEADS, GDN_DK, GDN_DV
    n_qkv = H * (2 * dk + dv)
    f32 = jnp.float32
    tm = GDN_TM if T % GDN_TM == 0 else T
    assert T % tm == 0 and tm % 8 == 0
    n_cols = w_in.shape[1]
    n_pad = n_qkv + H * dv + LANE
    assert n_cols == n_qkv + H * dv + 2 * H
    wb = jnp.pad(w_in, ((0, 0), (0, n_pad - n_cols))).astype(jnp.bfloat16)
    buf8 = jnp.pad(conv_buf, ((0, 0), (_CARRY - (CONV_W - 1), 0), (0, 0)))
    dyn = jnp.zeros((2, LANE), f32).at[0, H:2 * H].set(-jnp.exp(a_log.astype(f32))).at[1, H:2 * H].set(dt_bias.astype(f32))
    tok = lambda w: pl.BlockSpec((1, tm, w), lambda b, i: (b, i, 0))
    per_seq = lambda r, w: pl.BlockSpec((1, r, w), lambda b, i: (b, 0, 0))
    whole = lambda r, w: pl.BlockSpec((r, w), lambda b, i: (0, 0))
    q, k, v, z, gb, conv8 = pl.pallas_call(
        _gdn_in_kernel,
        grid=(B, T // tm),
        in_specs=[tok(D), per_seq(1, D), per_seq(1, D), whole(1, D), whole(D, n_pad), whole(CONV_W, n_qkv),
                  per_seq(_CARRY, n_qkv), whole(2, LANE)],
        out_specs=[tok(H * dk), tok(H * dk), tok(H * dv), tok(H * dv), tok(2 * H), per_seq(_CARRY, n_qkv)],
        out_shape=[jax.ShapeDtypeStruct((B, T, H * dk), f32), jax.ShapeDtypeStruct((B, T, H * dk), f32),
                   jax.ShapeDtypeStruct((B, T, H * dv), f32), jax.ShapeDtypeStruct((B, T, H * dv), f32),
                   jax.ShapeDtypeStruct((B, T, 2 * H), f32), jax.ShapeDtypeStruct((B, _CARRY, n_qkv), f32)],
        scratch_shapes=[pltpu.VMEM((_CARRY, n_qkv), f32)],
        compiler_params=pltpu.CompilerParams(dimension_semantics=("parallel", "arbitrary"), vmem_limit_bytes=VMEM_LIMIT),
        name="gdn_in",
    )(x, shift, scale, mix_g.reshape(1, D), wb, conv_w, buf8, dyn)
    o, S = _gdn_delta_rule(q, k, v, z, gb[..., H:], gb[..., :H], S0, norm_g)
    return _mm(o, w_out), conv8[:, _CARRY - (CONV_W - 1):], S


def _dot_split3(a, b):
    f32, bf16 = jnp.float32, jnp.bfloat16
    ah = a.astype(bf16)
    al = (a - ah.astype(f32)).astype(bf16)
    bh = b.astype(bf16)
    bl = (b - bh.astype(f32)).astype(bf16)
    dot = functools.partial(jnp.dot, preferred_element_type=f32)
    return dot(ah, bh) + dot(ah, bl) + dot(al, bh)


def _gdn_chunk_kernel(q_ref, k_ref, v_ref, z_ref, col_ref, row_ref, s0_ref, ng_ref, o_ref, sout_ref, s_scr):
    f32, bf16 = jnp.float32, jnp.bfloat16
    n = pl.program_id(1)
    C = q_ref.shape[1]
    H, dk, dv = GDN_HEADS, GDN_DK, GDN_DV
    dot = functools.partial(jnp.dot, preferred_element_type=f32)
    nt = (((1,), (1,)), ((), ()))

    @pl.when(n == 0)
    def _():
        s_scr[...] = s0_ref[0]

    ii = lax.broadcasted_iota(jnp.int32, (C, C), 0)
    jj = lax.broadcasted_iota(jnp.int32, (C, C), 1)
    incl = ii >= jj
    strict = ii > jj
    eye = (ii == jj).astype(f32)
    heads = range(H)
    ks = [slice(h * dk, (h + 1) * dk) for h in heads]
    vs = [slice(h * dv, (h + 1) * dv) for h in heads]
    gc = [col_ref[0, :, h:h + 1] for h in heads]
    bc = [col_ref[0, :, H + h:H + h + 1] for h in heads]
    gr = [row_ref[0, 0, h:h + 1, :] for h in heads]
    decay = [jnp.exp(jnp.where(incl, gc[h] - gr[h], -jnp.inf)) for h in heads]
    kcb = [k_ref[0, :, ks[h]].astype(bf16) for h in heads]
    kb = [k_ref[0, :, ks[h]] * bc[h] for h in heads]
    mpow = [-jnp.where(strict, lax.dot_general(kb[h].astype(bf16), kcb[h], nt, preferred_element_type=f32) * decay[h], 0.0)
            for h in heads]
    tinv = [eye + mpow[h] for h in heads]
    for _ in range(C.bit_length() - 2):
        mpow = [_dot_split3(mpow[h], mpow[h]) for h in heads]
        tinv = [tinv[h] + _dot_split3(tinv[h], mpow[h]) for h in heads]
    eg = [jnp.exp(gc[h]) for h in heads]
    uw = [dot(tinv[h].astype(bf16), jnp.concatenate([v_ref[0, :, vs[h]] * bc[h], kb[h] * eg[h]], axis=1).astype(bf16))
          for h in heads]
    qh = [q_ref[0, :, ks[h]] * dk ** -0.5 for h in heads]
    a_intra = [(lax.dot_general(qh[h].astype(bf16), kcb[h], nt, preferred_element_type=f32) * decay[h]).astype(bf16)
               for h in heads]
    s_b = [s_scr[h].astype(bf16) for h in heads]
    v_nb = [(uw[h][:, :dv] - dot(uw[h][:, dv:].astype(bf16), s_b[h])).astype(bf16) for h in heads]
    o = [dot((qh[h] * eg[h]).astype(bf16), s_b[h]) + dot(a_intra[h], v_nb[h]) for h in heads]
    for h in heads:
        g_last = gr[h][:, C - 1:C]
        kd = (k_ref[0, :, ks[h]] * jnp.exp(g_last - gc[h])).astype(bf16)
        s_scr[h] = s_scr[h] * jnp.exp(g_last) + lax.dot_general(kd, v_nb[h], _TN, preferred_element_type=f32)
    for h in heads:
        y = o[h] * lax.rsqrt(jnp.mean(o[h] * o[h], -1, keepdims=True) + EPS) * ng_ref[...]
        zz = z_ref[0, :, vs[h]]
        o_ref[0, :, vs[h]] = y * (zz * jax.nn.sigmoid(zz))

    @pl.when(n == pl.num_programs(1) - 1)
    def _():
        sout_ref[0] = s_scr[...]


def _gdn_delta_rule(q, k, v, z, g, beta, S0, norm_g):
    B, T, _ = q.shape
    H, dk, dv = GDN_HEADS, GDN_DK, GDN_DV
    C = GDN_CHUNK if T % GDN_CHUNK == 0 else T
    assert C & (C - 1) == 0 and C % 8 == 0
    nC = T // C
    f32 = jnp.float32
    gcum = jnp.cumsum(g.astype(f32).reshape(B, nC, C, H), axis=2)
    col = jnp.concatenate([gcum.reshape(B, T, H), beta.astype(f32)], axis=-1)
    row = jnp.swapaxes(gcum, 2, 3)
    tok = lambda w: pl.BlockSpec((1, C, w), lambda b, n: (b, n, 0))
    o, s_out = pl.pallas_call(
        _gdn_chunk_kernel,
        grid=(B, nC),
        in_specs=[tok(H * dk), tok(H * dk), tok(H * dv), tok(H * dv), tok(2 * H),
                  pl.BlockSpec((1, 1, H, C), lambda b, n: (b, n, 0, 0)),
                  pl.BlockSpec((1, H, dk, dv), lambda b, n: (b, 0, 0, 0)),
                  pl.BlockSpec((1, dv), lambda b, n: (0, 0))],
        out_specs=[tok(H * dv), pl.BlockSpec((1, H, dk, dv), lambda b, n: (b, 0, 0, 0))],
        out_shape=[jax.ShapeDtypeStruct((B, T, H * dv), f32), jax.ShapeDtypeStruct((B, H, dk, dv), f32)],
        scratch_shapes=[pltpu.VMEM((H, dk, dv), f32)],
        compiler_params=pltpu.CompilerParams(dimension_semantics=("parallel", "arbitrary"), vmem_limit_bytes=VMEM_LIMIT),
        name="gdn_delta_rule",
    )(q, k, v, z, col, row, S0.astype(f32), norm_g.reshape(1, dv))
    return o, s_out


def nsa_seq(q_s, g_s, full_s, win_s, pos0, wpos0, cmp_pos, cmp_w):
    dt = q_s.dtype
    T = q_s.shape[0]
    G, R, dh = NSA_KV_GROUPS, NSA_HEADS // NSA_KV_GROUPS, HEAD_DIM
    Tk = full_s.shape[0]
    Tkp = -(-Tk // SEL_BLOCK) * SEL_BLOCK
    full_p = jnp.pad(full_s, ((0, Tkp - Tk), (0, 0), (0, 0), (0, 0)))
    n_cmp = Tkp // CMP_BLOCK
    n_sel = Tkp // SEL_BLOCK
    blocks = full_p[:, :2].reshape(n_cmp, CMP_BLOCK, 2, G, dh) + jnp.transpose(cmp_pos, (1, 0, 2))[:, :, None, :]
    kv_cmp = jnp.einsum('nlsgd,slde->nsge', blocks, cmp_w)
    k_cmp, v_cmp = kv_cmp[:, 0], kv_cmp[:, 1]
    cmp_end = (jnp.arange(n_cmp) + 1) * CMP_BLOCK - 1
    sel = jnp.transpose(full_p[:, 2:].reshape(n_sel, SEL_BLOCK, 2, G, dh), (2, 3, 0, 1, 4))
    k_selb, v_selb = sel[0], sel[1]
    n_top = min(TOP_N, n_sel)
    win_p = jnp.pad(win_s, ((WINDOW, 0), (0, 0), (0, 0), (0, 0)))
    qb = Q_BLOCK if T % Q_BLOCK == 0 else T
    nqb = T // qb
    lw = WINDOW + qb - 1
    scale = HEAD_DIM ** -0.5
    g_idx = jnp.arange(G)[None, :, None]
    blk_ids = jnp.arange(n_sel)

    def block(args):
        qi, gi, bi = args
        qstart = pos0 + bi * qb
        qpos = qstart + jnp.arange(qb)
        qg = qi.reshape(qb, G, R, dh) * scale
        s_c = jnp.einsum('qgrd,ngd->qgrn', qg, k_cmp)
        p_c = masked_softmax(s_c, (cmp_end[None, :] <= qpos[:, None])[:, None, None, :])
        o_c = jnp.einsum('qgrn,ngd->qgrd', p_c.astype(dt), v_cmp)
        imp = p_c.sum(2).reshape(qb, G, n_sel, SEL_BLOCK // CMP_BLOCK).sum(-1)
        cur = (qpos // SEL_BLOCK)[:, None] == blk_ids[None, :]
        causal_blk = (blk_ids * SEL_BLOCK)[None, :] <= qpos[:, None]
        imp = jnp.where(cur[:, None], jnp.inf, jnp.where(causal_blk[:, None], imp, -jnp.inf))
        top_v, top_i = lax.top_k(imp, n_top)
        k_sel = k_selb[g_idx, top_i]
        v_sel = v_selb[g_idx, top_i]
        kpos = top_i[..., None] * SEL_BLOCK + jnp.arange(SEL_BLOCK)
        m_s = (kpos <= qpos[:, None, None, None]) & (top_v > -jnp.inf)[..., None]
        s_s = jnp.einsum('qgrd,qgksd->qgrks', qg, k_sel).reshape(qb, G, R, n_top * SEL_BLOCK)
        p_s = masked_softmax(s_s, m_s.reshape(qb, G, 1, n_top * SEL_BLOCK))
        o_s = jnp.einsum('qgrm,qgmd->qgrd', p_s.astype(dt), v_sel.reshape(qb, G, n_top * SEL_BLOCK, dh))
        wblk = lax.dynamic_slice_in_dim(win_p, qstart - wpos0 + 1, lw, axis=0)
        wpos = qstart - WINDOW + 1 + jnp.arange(lw)
        m_w = (wpos[None] <= qpos[:, None]) & (wpos[None] > qpos[:, None] - WINDOW) & (wpos[None] >= wpos0)
        s_w = jnp.einsum('qgrd,kgd->qgrk', qg, wblk[:, 0])
        p_w = masked_softmax(s_w, m_w[:, None, None, :])
        o_w = jnp.einsum('qgrk,kgd->qgrd', p_w.astype(dt), wblk[:, 1])
        gg = gi.reshape(qb, G, R, 3)
        o = gg[..., 0:1] * o_c + gg[..., 1:2] * o_s + gg[..., 2:3] * o_w
        return o.reshape(qb, NSA_HEADS, dh)

    out = lax.map(block, (q_s.reshape(nqb, qb, NSA_HEADS, dh), g_s.reshape(nqb, qb, NSA_HEADS, 3), jnp.arange(nqb)))
    return out.reshape(T, NSA_HEADS, dh)


NSA_R = NSA_HEADS // NSA_KV_GROUPS
NSA_TK = 1024
NSA_WSPAN = WINDOW + Q_BLOCK


def _softmax_rows(s, mask):
    s = jnp.where(mask, s, -jnp.inf)
    m = jnp.max(s, axis=0, keepdims=True)
    m = jnp.where(m == -jnp.inf, 0.0, m)
    p = jnp.exp(s - m)
    return p * (1.0 / jnp.maximum(jnp.sum(p, axis=0, keepdims=True), 1e-30))


def _nsa_cmp_kernel(x_ref, pos_ref, w_ref, o_ref):
    xb = (x_ref[0, 0, 0, 0] + pos_ref[0]).astype(jnp.bfloat16)
    o_ref[0, 0, 0, 0] = jnp.dot(xb, w_ref[0], preferred_element_type=jnp.float32)


def _nsa_compress(rows_cmp, cmp_pos, cmp_w):
    B, T, _, G, dh = rows_cmp.shape
    n_sel = T // SEL_BLOCK
    ld = CMP_BLOCK * dh
    x = rows_cmp.reshape(B, n_sel, 2, CMP_BLOCK, 2, G, dh)
    x = jnp.transpose(x, (0, 4, 5, 2, 1, 3, 6)).reshape(B, 2, G, 2, n_sel, ld)
    return pl.pallas_call(
        _nsa_cmp_kernel,
        grid=(B, 2, G, 2),
        in_specs=[pl.BlockSpec((1, 1, 1, 1, n_sel, ld), lambda b, s, g, p: (b, s, g, p, 0, 0)),
                  pl.BlockSpec((1, 1, ld), lambda b, s, g, p: (s, 0, 0)),
                  pl.BlockSpec((1, ld, dh), lambda b, s, g, p: (s, 0, 0))],
        out_specs=pl.BlockSpec((1, 1, 1, 1, n_sel, dh), lambda b, s, g, p: (b, s, g, p, 0, 0)),
        out_shape=jax.ShapeDtypeStruct((B, 2, G, 2, n_sel, dh), jnp.float32),
        compiler_params=pltpu.CompilerParams(dimension_semantics=("parallel",) * 4, vmem_limit_bytes=VMEM_LIMIT),
        name="nsa_compress",
    )(x, cmp_pos.reshape(2, 1, ld), cmp_w.reshape(2, ld, dh).astype(jnp.bfloat16))


def _nsa_prompt_kernel(q_ref, gt_ref, kc_ref, vct_ref, ks_ref, vst_ref, kw_ref, vwt_ref, o_ref,
                       sel_scr, m_scr, l_scr, acc_scr):
    f32, bf16 = jnp.float32, jnp.bfloat16
    i = pl.program_id(2)
    R, QB = NSA_R, Q_BLOCK
    W = R * QB
    T = ks_ref.shape[2]
    n_sel = T // SEL_BLOCK
    nt = (((1,), (1,)), ((), ()))
    qb = q_ref[0]
    q = jnp.concatenate([qb[:, r * HEAD_DIM:(r + 1) * HEAD_DIM] for r in range(R)], axis=0)
    q = (q * HEAD_DIM ** -0.5).astype(bf16)
    lane = lax.broadcasted_iota(jnp.int32, (1, W), 1)
    qpos = i * QB + (lane & (QB - 1))

    s = lax.dot_general(kc_ref[0, 0], q, nt, preferred_element_type=f32)
    row = lax.broadcasted_iota(jnp.int32, (2 * n_sel, 1), 0)
    cidx = jnp.where(row < n_sel, 2 * row, 2 * (row - n_sel) + 1)
    p_c = _softmax_rows(s, (cidx + 1) * CMP_BLOCK - 1 <= qpos)
    o_c = jnp.dot(vct_ref[0, 0], p_c.astype(bf16), preferred_element_type=f32)

    p_r = p_c[:, 0:QB]
    for r in range(1, R):
        p_r = p_r + p_c[:, r * QB:(r + 1) * QB]
    imp = p_r[:n_sel] + p_r[n_sel:]
    blk = lax.broadcasted_iota(jnp.int32, (n_sel, QB), 0)
    qp = qpos[:, :QB]
    imp = jnp.where(blk == qp // SEL_BLOCK, jnp.inf, jnp.where(blk * SEL_BLOCK <= qp, imp, -jnp.inf))
    chosen = [jnp.zeros((n_sel, QB), f32)]

    def pick(r, m, sel, chosen=chosen):
        chosen[0] = jnp.where(sel & (m > -jnp.inf), 1.0, chosen[0])

    _extract_top(imp, min(TOP_N, n_sel), pick)
    sel_scr[...] = jnp.concatenate([chosen[0]] * R, axis=1)

    m_scr[...] = jnp.full(m_scr.shape, -jnp.inf, f32)
    l_scr[...] = jnp.zeros(l_scr.shape, f32)
    acc_scr[...] = jnp.zeros(acc_scr.shape, f32)
    blk_per_tile = NSA_TK // SEL_BLOCK

    def tile_body(kt, carry):
        k0 = pl.multiple_of(kt * NSA_TK, NSA_TK)
        s = lax.dot_general(ks_ref[0, 0, pl.ds(k0, NSA_TK), :], q, nt, preferred_element_type=f32)
        selrows = sel_scr[pl.ds(pl.multiple_of(kt * blk_per_tile, blk_per_tile), blk_per_tile), :]
        kpos = k0 + lax.broadcasted_iota(jnp.int32, (NSA_TK, 1), 0)
        parts = []
        for j in range(blk_per_tile):
            rows = slice(j * SEL_BLOCK, (j + 1) * SEL_BLOCK)
            ok = (selrows[j:j + 1, :] > 0.0) & (kpos[rows] <= qpos)
            parts.append(jnp.where(ok, s[rows], -jnp.inf))
        _online_softmax_tile(jnp.concatenate(parts, axis=0), vst_ref[0, 0, :, pl.ds(k0, NSA_TK)], m_scr, l_scr, acc_scr)
        return carry

    lax.fori_loop(0, (i * QB + QB + NSA_TK - 1) // NSA_TK, tile_body, 0)
    o_s = acc_scr[...] * (1.0 / jnp.maximum(l_scr[...], 1e-30))

    w0 = pl.multiple_of(jnp.clip(i * QB + QB - NSA_WSPAN, 0, T - NSA_WSPAN), QB)
    s = lax.dot_general(kw_ref[0, 0, pl.ds(w0, NSA_WSPAN), :], q, nt, preferred_element_type=f32)
    kpos = w0 + lax.broadcasted_iota(jnp.int32, (NSA_WSPAN, 1), 0)
    p_w = _softmax_rows(s, (kpos <= qpos) & (kpos > qpos - WINDOW))
    o_w = jnp.dot(vwt_ref[0, 0, :, pl.ds(w0, NSA_WSPAN)], p_w.astype(bf16), preferred_element_type=f32)

    g = gt_ref[0, 0, 0]
    o = g[0:1] * o_c + g[1:2] * o_s + g[2:3] * o_w
    for r in range(R):
        o_ref[0, :, r * HEAD_DIM:(r + 1) * HEAD_DIM] = o[:, r * QB:(r + 1) * QB].T


def _nsa_prompt_attention(q, gates, new_rows, win, cmp_pos, cmp_w):
    B, T, _ = q.shape
    G, R, dh, QB = NSA_KV_GROUPS, NSA_R, HEAD_DIM, Q_BLOCK
    assert T % NSA_TK == 0 and T >= NSA_WSPAN and (T // SEL_BLOCK) % 8 == 0
    nqb = T // QB
    n_sel = T // SEL_BLOCK
    W = R * QB
    bf16 = jnp.bfloat16
    cmp = _nsa_compress(new_rows[:, :, :2], cmp_pos, cmp_w).reshape(B, 2, G, 2 * n_sel, dh)
    kc = cmp[:, 0].astype(bf16)
    vct = jnp.swapaxes(cmp[:, 1], -1, -2).astype(bf16)
    ks = jnp.transpose(new_rows[:, :, 2], (0, 2, 1, 3)).astype(bf16)
    vst = jnp.transpose(new_rows[:, :, 3], (0, 2, 3, 1)).astype(bf16)
    kw = jnp.transpose(win[:, :, 0], (0, 2, 1, 3)).astype(bf16)
    vwt = jnp.transpose(win[:, :, 1], (0, 2, 3, 1)).astype(bf16)
    gt = jnp.transpose(gates.reshape(B, nqb, QB, G, R, 3), (0, 3, 1, 5, 4, 2)).reshape(B, G, nqb, 3, W)
    gt = jnp.pad(gt, ((0, 0), (0, 0), (0, 0), (0, 5), (0, 0)))
    per_bg = lambda *shape: pl.BlockSpec((1, 1) + shape, lambda b, g, i: (b, g, 0, 0))
    return pl.pallas_call(
        _nsa_prompt_kernel,
        grid=(B, G, nqb),
        in_specs=[pl.BlockSpec((1, QB, R * dh), lambda b, g, i: (b, i, g)),
                  pl.BlockSpec((1, 1, 1, 8, W), lambda b, g, i: (b, g, i, 0, 0)),
                  per_bg(2 * n_sel, dh), per_bg(dh, 2 * n_sel), per_bg(T, dh), per_bg(dh, T), per_bg(T, dh), per_bg(dh, T)],
        out_specs=pl.BlockSpec((1, QB, R * dh), lambda b, g, i: (b, i, g)),
        out_shape=jax.ShapeDtypeStruct(q.shape, jnp.float32),
        scratch_shapes=[pltpu.VMEM((n_sel, W), jnp.float32), pltpu.VMEM((1, W), jnp.float32),
                        pltpu.VMEM((1, W), jnp.float32), pltpu.VMEM((dh, W), jnp.float32)],
        compiler_params=pltpu.CompilerParams(dimension_semantics=("parallel", "parallel", "arbitrary"),
                                             vmem_limit_bytes=VMEM_LIMIT),
        name="nsa_prompt",
    )(q, gt, kc, vct, ks, vst, kw, vwt)


_TN = (((0,), (0,)), ((), ()))
_NT = (((1,), (1,)), ((), ()))


def _nsa_sample_kernel(pt_ref, *refs, n_pages, past_len):
    pages = refs[:n_pages]
    q_ref, rows_ref, wnew_ref, wbuf_ref, gt_ref, pos_ref, w_ref, o_ref, sel_scr, xs_scr, pr_scr = refs[n_pages:]
    f32, bf16 = jnp.float32, jnp.bfloat16
    G, R, dh = NSA_KV_GROUPS, NSA_R, HEAD_DIM
    T = q_ref.shape[1]
    n_blk = past_len // SEL_BLOCK
    n_w = wbuf_ref.shape[1]
    lane = lax.broadcasted_iota(jnp.int32, (1, LANE), 1)
    qidx = lane & (T - 1)
    qpos = past_len + qidx
    new_row = lax.broadcasted_iota(jnp.int32, (LANE, 1), 0)
    new_ok = (new_row < T) & (new_row <= qidx)

    def new_tile(ref, col):
        return jnp.concatenate([ref[0, :, col * dh:(col + 1) * dh], jnp.zeros((LANE - T, dh), f32)], axis=0).astype(bf16)

    page = pages[0].shape[1] // (4 * G)
    per_page = page // CMP_BLOCK
    n_cmp = n_pages * per_page
    cmp = []
    for s in range(2):
        for g in range(G):
            for p, pg in enumerate(pages):
                for l in range(CMP_BLOCK):
                    xs_scr[g * n_cmp + p * per_page:g * n_cmp + (p + 1) * per_page, l * dh:(l + 1) * dh] = (
                        pg[0, pl.ds(l * 4 * G + s * G + g, per_page, stride=CMP_BLOCK * 4 * G), :])
        cmp.append(jnp.dot((xs_scr[...] + pos_ref[s]).astype(bf16), w_ref[s], preferred_element_type=f32))
    for g in range(G):
        q = jnp.concatenate([q_ref[0, :, (g * R + r) * dh:(g * R + r + 1) * dh] for r in range(R)]
                            + [jnp.zeros((LANE - R * T, dh), f32)], axis=0)
        q = (q * dh ** -0.5).astype(bf16)
        kc = cmp[0][g * 2 * n_blk:(g + 1) * 2 * n_blk]
        vc = cmp[1][g * 2 * n_blk:(g + 1) * 2 * n_blk]
        s_c = lax.dot_general(kc.astype(bf16), q, _NT, preferred_element_type=f32)
        p_c = _softmax_rows(s_c, jnp.full(s_c.shape, True))
        o_c = lax.dot_general(vc.astype(bf16), p_c.astype(bf16), _TN, preferred_element_type=f32)
        p_r = p_c
        for r in range(1, R):
            p_r = p_r + pltpu.roll(p_c, LANE - r * T, axis=1)
        pr_scr[...] = p_r
        imp = pr_scr[pl.ds(0, n_blk, stride=2), :] + pr_scr[pl.ds(1, n_blk, stride=2), :]
        chosen = [jnp.zeros((n_blk, LANE), f32)]

        def pick(_, m, sel, chosen=chosen):
            chosen[0] = jnp.where(sel, 1.0, chosen[0])

        _extract_top(imp, min(TOP_N, n_blk + 1) - 1, pick)
        ch = jnp.where(lane < T, chosen[0], 0.0)
        ch4 = ch
        for r in range(1, R):
            ch4 = ch4 + pltpu.roll(ch, r * T, axis=1)
        sel_scr[...] = ch4
        ks = jnp.concatenate([pg[0, pl.ds(2 * G + g, page, stride=4 * G), :] for pg in pages], axis=0).astype(bf16)
        vs = jnp.concatenate([pg[0, pl.ds(3 * G + g, page, stride=4 * G), :] for pg in pages], axis=0).astype(bf16)
        s_p = lax.dot_general(ks, q, _NT, preferred_element_type=f32)
        s_p = jnp.concatenate([jnp.where(sel_scr[j:j + 1, :] > 0.0, s_p[j * SEL_BLOCK:(j + 1) * SEL_BLOCK], -jnp.inf)
                               for j in range(n_blk)], axis=0)
        s_n = lax.dot_general(new_tile(rows_ref, 2 * G + g), q, _NT, preferred_element_type=f32)
        s_all = jnp.concatenate([s_p, jnp.where(new_ok, s_n, -jnp.inf)], axis=0)
        p_s = _softmax_rows(s_all, s_all > -jnp.inf).astype(bf16)
        o_s = (lax.dot_general(vs, p_s[:past_len], _TN, preferred_element_type=f32)
               + lax.dot_general(new_tile(rows_ref, 3 * G + g), p_s[past_len:], _TN, preferred_element_type=f32))
        kw = wbuf_ref[0, :, g * dh:(g + 1) * dh].astype(bf16)
        vw = wbuf_ref[0, :, (G + g) * dh:(G + g + 1) * dh].astype(bf16)
        s_w = lax.dot_general(kw, q, _NT, preferred_element_type=f32)
        wpos = past_len - n_w + lax.broadcasted_iota(jnp.int32, (n_w, 1), 0)
        s_w = jnp.where(wpos > qpos - WINDOW, s_w, -jnp.inf)
        s_n = lax.dot_general(new_tile(wnew_ref, g), q, _NT, preferred_element_type=f32)
        s_all = jnp.concatenate([s_w, jnp.where(new_ok, s_n, -jnp.inf)], axis=0)
        p_w = _softmax_rows(s_all, s_all > -jnp.inf).astype(bf16)
        o_w = (lax.dot_general(vw, p_w[:n_w], _TN, preferred_element_type=f32)
               + lax.dot_general(new_tile(wnew_ref, G + g), p_w[n_w:], _TN, preferred_element_type=f32))
        gt = gt_ref[0, g]
        o = (gt[0:1] * o_c + gt[1:2] * o_s + gt[2:3] * o_w).T
        for r in range(R):
            o_ref[0, :, (g * R + r) * dh:(g * R + r + 1) * dh] = o[r * T:(r + 1) * T]


def _nsa_sample_attention(q, gates, new_rows, win_new, cache, page_table, wbuf, cmp_pos, cmp_w):
    B, T, _ = q.shape
    G, R, dh = NSA_KV_GROUPS, NSA_R, HEAD_DIM
    n_pool, page = cache.shape[:2]
    n_pages = page_table.shape[1]
    past_len = n_pages * page
    n_w = wbuf.shape[1]
    assert T == 8 and R * T <= LANE and page % SEL_BLOCK == 0 and n_w + T > WINDOW
    assert past_len % SEL_BLOCK == 0 and T <= CMP_BLOCK
    row_w = 4 * G * dh
    per_page = page // SEL_BLOCK
    f32 = jnp.float32
    cache2 = cache.reshape(n_pool, page * 4 * G, dh)
    n_blk = past_len // SEL_BLOCK
    gt = jnp.transpose(gates.reshape(B, T, G, R, 3), (0, 2, 4, 3, 1)).reshape(B, G, 3, R * T)
    gt = jnp.pad(gt, ((0, 0), (0, 0), (0, 5), (0, LANE - R * T)))
    page_spec = lambda p: pl.BlockSpec((1, page * 4 * G, dh), lambda b, pt: (pt[b, p], 0, 0))
    seq = lambda *shape: pl.BlockSpec((1,) + shape, lambda b, pt: (b,) + (0,) * len(shape))
    whole = lambda *shape: pl.BlockSpec(shape, lambda b, pt: (0,) * len(shape))
    grid_spec = pltpu.PrefetchScalarGridSpec(
        num_scalar_prefetch=1,
        grid=(B,),
        in_specs=[page_spec(p) for p in range(n_pages)]
                 + [seq(T, R * G * dh), seq(T, row_w), seq(T, 2 * G * dh), seq(n_w, 2 * G * dh), seq(G, 8, LANE),
                    whole(2, 1, CMP_BLOCK * dh), whole(2, CMP_BLOCK * dh, dh)],
        out_specs=seq(T, R * G * dh),
        scratch_shapes=[pltpu.VMEM((n_blk, LANE), f32), pltpu.VMEM((G * 2 * n_blk, CMP_BLOCK * dh), f32),
                        pltpu.VMEM((2 * n_blk, LANE), f32)],
    )
    return pl.pallas_call(
        functools.partial(_nsa_sample_kernel, n_pages=n_pages, past_len=past_len),
        grid_spec=grid_spec,
        out_shape=jax.ShapeDtypeStruct(q.shape, f32),
        compiler_params=pltpu.CompilerParams(dimension_semantics=("parallel",), vmem_limit_bytes=VMEM_LIMIT),
        name="nsa_sample",
    )(page_table, *([cache2] * n_pages), q, new_rows.reshape(B, T, row_w), win_new.reshape(B, T, 2 * G * dh),
      wbuf.reshape(B, n_w, 2 * G * dh), gt, cmp_pos.reshape(2, 1, CMP_BLOCK * dh),
      cmp_w.reshape(2, CMP_BLOCK * dh, dh).astype(jnp.bfloat16))


def nsa_mixer(h, pos0, past, wbuf, w_in, cmp_pos, cmp_w, w_out):
    B, T, _ = h.shape
    G, dh = NSA_KV_GROUPS, HEAD_DIM
    qd = NSA_HEADS * dh
    kvd = 6 * G * dh
    pos = pos0 + jnp.arange(T, dtype=jnp.int32)
    proj = _mm(h, w_in)
    q = rope(proj[..., :qd].reshape(B, T, NSA_HEADS, dh), pos)
    kv = proj[..., qd:qd + kvd].reshape(B, T, 6, G, dh)
    gates = jax.nn.sigmoid(proj[..., qd + kvd:].reshape(B, T, NSA_HEADS, 3))
    keys = rope(kv[:, :, 0::2].reshape(B, T, 3 * G, dh), pos).reshape(B, T, 3, G, dh)
    kv = jnp.stack([keys[:, :, 0], kv[:, :, 1], keys[:, :, 1], kv[:, :, 3], keys[:, :, 2], kv[:, :, 5]], axis=2)
    new_rows = kv[:, :, :4]
    win = kv[:, :, 4:] if past is None else jnp.concatenate([wbuf, kv[:, :, 4:]], 1)
    if past is None:
        o = _nsa_prompt_attention(q.reshape(B, T, qd), gates, new_rows, win, cmp_pos, cmp_w)
    else:
        cache, page_table = past
        o = _nsa_sample_attention(q.reshape(B, T, qd), gates, new_rows, kv[:, :, 4:], cache, page_table, wbuf,
                                  cmp_pos, cmp_w)
    new_win = win[:, -min(WINDOW, win.shape[1]):]
    return _mm(o.reshape(B, T, D_MODEL), w_out), new_rows, new_win


DIFF_QB = 512
DIFF_TK = 1024


def _online_softmax_tile(s, vt_tile, m_scr, l_scr, acc_scr):
    m_old = m_scr[...]
    m_new = jnp.maximum(m_old, jnp.max(s, axis=0, keepdims=True))
    m_safe = jnp.where(m_new == -jnp.inf, 0.0, m_new)
    alpha = jnp.exp(m_old - m_safe)
    p = jnp.exp(s - m_safe)
    l_scr[...] = alpha * l_scr[...] + jnp.sum(p, axis=0, keepdims=True)
    acc_scr[...] = alpha * acc_scr[...] + jnp.dot(vt_tile, p.astype(jnp.bfloat16), preferred_element_type=jnp.float32)
    m_scr[...] = m_new


def _diff_prompt_kernel(lam_ref, q_ref, k_ref, vt_ref, g_ref, o_ref, m_scr, l_scr, acc_scr, *, out_scale):
    f32, bf16 = jnp.float32, jnp.bfloat16
    i = pl.program_id(2)
    QB, TK, dd = DIFF_QB, DIFF_TK, DIFF_DH
    W = 2 * QB
    nt = (((1,), (1,)), ((), ()))
    qb = q_ref[0] * dd ** -0.5
    col = lax.broadcasted_iota(jnp.int32, (1, 2 * dd), 1)
    q = jnp.concatenate([jnp.where(col < dd, qb, 0.0), jnp.where(col >= dd, qb, 0.0)], axis=0).astype(bf16)
    lane = lax.broadcasted_iota(jnp.int32, (1, W), 1)
    qpos = i * QB + (lane & (QB - 1))
    m_scr[...] = jnp.full(m_scr.shape, -jnp.inf, f32)
    l_scr[...] = jnp.zeros(l_scr.shape, f32)
    acc_scr[...] = jnp.zeros(acc_scr.shape, f32)
    n_full = (i * QB) // TK

    def full_tile(kt, carry):
        k0 = pl.multiple_of(kt * TK, TK)
        s = lax.dot_general(k_ref[0, pl.ds(k0, TK), :], q, nt, preferred_element_type=f32)
        _online_softmax_tile(s, vt_ref[0, 0, :, pl.ds(k0, TK)], m_scr, l_scr, acc_scr)
        return carry

    lax.fori_loop(0, n_full, full_tile, 0)
    k0 = pl.multiple_of(n_full * TK, TK)
    s = lax.dot_general(k_ref[0, pl.ds(k0, TK), :], q, nt, preferred_element_type=f32)
    kpos = k0 + lax.broadcasted_iota(jnp.int32, (TK, 1), 0)
    _online_softmax_tile(jnp.where(kpos <= qpos, s, -jnp.inf), vt_ref[0, 0, :, pl.ds(k0, TK)], m_scr, l_scr, acc_scr)
    o = acc_scr[...] * (1.0 / jnp.maximum(l_scr[...], 1e-30))
    o = o[:, :QB] - lam_ref[0, 0] * o[:, QB:]
    o = o * lax.rsqrt(jnp.mean(o * o, axis=0, keepdims=True) + EPS) * (g_ref[...] * out_scale)
    o_ref[0] = o.T


def _diff_prompt_attention(q, k, v, lam, subln_g, out_scale):
    B, T, _ = q.shape
    H, dv = DIFF_HEADS, 2 * DIFF_DH
    assert T % DIFF_TK == 0 and DIFF_TK % DIFF_QB == 0
    bf16 = jnp.bfloat16
    vt = jnp.transpose(v.reshape(B, T, H, dv), (0, 2, 3, 1)).astype(bf16)
    W = 2 * DIFF_QB
    return pl.pallas_call(
        functools.partial(_diff_prompt_kernel, out_scale=out_scale),
        grid=(B, H, T // DIFF_QB),
        in_specs=[pl.BlockSpec(memory_space=pltpu.SMEM),
                  pl.BlockSpec((1, DIFF_QB, dv), lambda b, h, i: (b, i, h)),
                  pl.BlockSpec((1, T, dv), lambda b, h, i: (b, 0, h)),
                  pl.BlockSpec((1, 1, dv, T), lambda b, h, i: (b, h, 0, 0)),
                  pl.BlockSpec((dv, 1), lambda b, h, i: (0, 0))],
        out_specs=pl.BlockSpec((1, DIFF_QB, dv), lambda b, h, i: (b, i, h)),
        out_shape=jax.ShapeDtypeStruct(q.shape, jnp.float32),
        scratch_shapes=[pltpu.VMEM((1, W), jnp.float32), pltpu.VMEM((1, W), jnp.float32), pltpu.VMEM((dv, W), jnp.float32)],
        compiler_params=pltpu.CompilerParams(dimension_semantics=("parallel", "parallel", "arbitrary"),
                                             vmem_limit_bytes=VMEM_LIMIT),
        name="diff_prompt",
    )(lam.reshape(1, 1), q, k.astype(bf16), vt, subln_g.reshape(dv, 1))


def _diff_sample_kernel(pt_ref, lam_ref, *refs, n_pages, out_scale):
    k_pages, v_pages = refs[:n_pages], refs[n_pages:2 * n_pages]
    q_ref, kn_ref, vn_ref, g_ref, o_ref = refs[2 * n_pages:]
    f32, bf16 = jnp.float32, jnp.bfloat16
    H, dd = DIFF_HEADS, DIFF_DH
    T = q_ref.shape[1]
    D = H * 2 * dd
    qt = jnp.concatenate([q_ref[0] * dd ** -0.5] * (LANE // T), axis=0)
    rowi = lax.broadcasted_iota(jnp.int32, (LANE, D), 0)
    coli = lax.broadcasted_iota(jnp.int32, (LANE, D), 1)
    qbd = jnp.where(coli // dd == rowi // T, qt, 0.0).astype(bf16)
    dot = functools.partial(jnp.dot, preferred_element_type=f32)
    s = [dot(qbd, kp[...].astype(bf16)) for kp in k_pages]
    key = lax.broadcasted_iota(jnp.int32, (1, LANE), 1)
    qidx = lax.broadcasted_iota(jnp.int32, (LANE, 1), 0) & (T - 1)
    s_n = jnp.where((key < T) & (key <= qidx), dot(qbd, kn_ref[0].astype(bf16)), -jnp.inf)
    m = jnp.max(s_n, axis=1, keepdims=True)
    for sp in s:
        m = jnp.maximum(m, jnp.max(sp, axis=1, keepdims=True))
    p_n = jnp.exp(s_n - m)
    l = jnp.sum(p_n, axis=1, keepdims=True)
    acc = lax.dot_general(p_n.astype(bf16), vn_ref[0].astype(bf16), _NT, preferred_element_type=f32)
    for vp, sp in zip(v_pages, s):
        p = jnp.exp(sp - m)
        l = l + jnp.sum(p, axis=1, keepdims=True)
        acc = acc + lax.dot_general(p.astype(bf16), vp[...].astype(bf16), _NT, preferred_element_type=f32)
    o = acc * (1.0 / jnp.maximum(l, 1e-30))
    lam = lam_ref[0, 0]
    for h in range(H):
        cols = slice(h * 2 * dd, (h + 1) * 2 * dd)
        d = o[2 * h * T:(2 * h + 1) * T, cols] - lam * o[(2 * h + 1) * T:(2 * h + 2) * T, cols]
        o_ref[0, :, cols] = d * lax.rsqrt(jnp.mean(d * d, axis=1, keepdims=True) + EPS) * (g_ref[...] * out_scale)


def _diff_sample_attention(q, new_rows, cache, page_table, lam, subln_g, out_scale):
    B, T, D = q.shape
    n_pool, page = cache.shape[:2]
    n_pages = page_table.shape[1]
    assert T == 8 and 2 * DIFF_HEADS * T == LANE and page == LANE
    f32 = jnp.float32
    slabs = jnp.transpose(cache, (0, 2, 3, 4, 1)).reshape(n_pool, 2, D, page)
    new = jnp.pad(jnp.transpose(new_rows, (0, 2, 3, 4, 1)).reshape(B, 2, D, T), ((0, 0), (0, 0), (0, 0), (0, LANE - T)))
    slab = lambda p, kv: pl.BlockSpec((None, None, D, page), lambda b, pt: (pt[b, p], kv, 0, 0))
    new_slab = lambda kv: pl.BlockSpec((1, None, D, LANE), lambda b, pt: (b, kv, 0, 0))
    grid_spec = pltpu.PrefetchScalarGridSpec(
        num_scalar_prefetch=1,
        grid=(B,),
        in_specs=[pl.BlockSpec(memory_space=pltpu.SMEM)] + [slab(p, 0) for p in range(n_pages)]
                 + [slab(p, 1) for p in range(n_pages)]
                 + [pl.BlockSpec((1, T, D), lambda b, pt: (b, 0, 0)), new_slab(0), new_slab(1),
                    pl.BlockSpec((1, 2 * DIFF_DH), lambda b, pt: (0, 0))],
        out_specs=pl.BlockSpec((1, T, D), lambda b, pt: (b, 0, 0)),
    )
    return pl.pallas_call(
        functools.partial(_diff_sample_kernel, n_pages=n_pages, out_scale=out_scale),
        grid_spec=grid_spec,
        out_shape=jax.ShapeDtypeStruct(q.shape, f32),
        compiler_params=pltpu.CompilerParams(dimension_semantics=("parallel",), vmem_limit_bytes=56 * 1024 * 1024),
        name="diff_sample",
    )(page_table, lam.reshape(1, 1), *([slabs] * (2 * n_pages)), q, new, new, subln_g.reshape(1, 2 * DIFF_DH))


def diff_mixer(h, pos0, past, layer_idx, w_in, lq1, lk1, lq2, lk2, subln_g, w_out):
    B, T, _ = h.shape
    H, dd = DIFF_HEADS, DIFF_DH
    f32 = jnp.float32
    dt = h.dtype
    pos = pos0 + jnp.arange(T, dtype=jnp.int32)
    proj = _mm(h, w_in)
    q = rope(proj[..., :D_MODEL].reshape(B, T, 2 * H, dd), pos)
    k = rope(proj[..., D_MODEL:2 * D_MODEL].reshape(B, T, 2 * H, dd), pos)
    v = proj[..., 2 * D_MODEL:].reshape(B, T, 2 * H, dd)
    new_rows = jnp.stack([k, v], 2)
    lam_init = 0.8 - 0.6 * math.exp(-0.3 * layer_idx)
    lam = (jnp.exp(jnp.sum(lq1.astype(f32) * lk1.astype(f32))) - jnp.exp(jnp.sum(lq2.astype(f32) * lk2.astype(f32))) + lam_init)
    if past is None:
        o = _diff_prompt_attention(q.reshape(B, T, D_MODEL), k.reshape(B, T, D_MODEL), v.reshape(B, T, D_MODEL),
                                   lam, subln_g, 1.0 - lam_init)
        return _mm(o, w_out), new_rows
    cache, page_table = past
    o = _diff_sample_attention(q.reshape(B, T, D_MODEL), new_rows, cache, page_table, lam, subln_g, 1.0 - lam_init)
    return _mm(o, w_out), new_rows


PEER_TB = 512
PEER_ET = 1024
_PEER_CAND = [(a, b) for a in range(PEER_TOPK) for b in range(PEER_TOPK) if (a + 1) * (b + 1) <= PEER_TOPK]


def _extract_top(s, n_iter, on_pick, break_ties=True):
    rows = s.shape[0]
    iota = lax.broadcasted_iota(jnp.int32, s.shape, 0)
    for r in range(n_iter):
        m = jnp.max(s, axis=0, keepdims=True)
        if break_ties:
            idx = jnp.min(jnp.where(s == m, iota, rows), axis=0, keepdims=True)
            sel = iota == idx
        else:
            sel = s == m
        on_pick(r, m, sel)
        s = jnp.where(sel, -jnp.inf, s)


def _peer_route_kernel(x_ref, shift_ref, scale_ref, g_ref, wq_ref, k1_ref, k2_ref, aof_ref,
                       h_ref, c1_ref, cnt1_ref, rank2_ref, e2_ref, q_scr, v1_scr, v2_scr, cand_scr):
    f32 = jnp.float32
    x = x_ref[...]
    tb = x.shape[0] * x.shape[1]
    y = x * lax.rsqrt(jnp.mean(x * x, -1, keepdims=True) + EPS) * g_ref[...]
    h = (y * (1.0 + scale_ref[...]) + shift_ref[...]).reshape(tb, D_MODEL)
    hb = h.astype(jnp.bfloat16)
    h_ref[...] = hb
    q = jnp.dot(hb, wq_ref[...], preferred_element_type=f32).astype(jnp.bfloat16)
    for hh in range(PEER_HEADS):
        q_scr[hh] = q[:, hh * PEER_QDIM:(hh + 1) * PEER_QDIM]
    nt = (((1,), (1,)), ((), ()))
    n_chunk = tb // LANE
    cand_scr[...] = jnp.full(cand_scr.shape, -jnp.inf, f32)

    def process(it, exact):
        hh = it // n_chunk
        c0 = pl.multiple_of((it % n_chunk) * LANE, LANE)
        qc = q_scr[hh, pl.ds(c0, LANE), :]
        s1 = lax.dot_general(k1_ref[hh], qc, nt, preferred_element_type=f32)
        s2 = lax.dot_general(k2_ref[hh], qc, nt, preferred_element_type=f32)
        ranks = []
        for s, v_scr in ((s1, v1_scr), (s2, v2_scr)):
            rank = [jnp.full(s.shape, float(N_KEYS), f32)]

            def pick(r, m, sel, v_scr=v_scr, rank=rank):
                v_scr[r:r + 1, :] = m
                rank[0] = jnp.where(sel, float(r), rank[0])

            _extract_top(s, PEER_TOPK, pick, break_ties=exact)
            ranks.append(rank[0])
        for k, (a, b) in enumerate(_PEER_CAND):
            cand_scr[k:k + 1, :] = v1_scr[a:a + 1, :] + v2_scr[b:b + 1, :]
        top1 = v1_scr[0:1, :]
        top2 = v2_scr[0:1, :]
        top_val = top1 + top2
        a_of = aof_ref[...]
        iota16 = lax.broadcasted_iota(jnp.int32, (PEER_TOPK, LANE), 0)
        st = {"z": jnp.zeros((1, LANE), f32), "cnt": jnp.zeros((PEER_TOPK, LANE), f32), "taken": jnp.zeros((1, LANE), f32)}

        def pick_c(r, m, sel, st=st):
            st["z"] = st["z"] + jnp.exp(m - top_val)
            a_sel = jnp.max(jnp.where(sel, a_of, 0), axis=0, keepdims=True)
            st["cnt"] = st["cnt"] + (iota16 == a_sel).astype(f32)
            if not exact:
                st["taken"] = st["taken"] + jnp.sum(sel.astype(f32), axis=0, keepdims=True)

        _extract_top(cand_scr[...], PEER_TOPK, pick_c, break_ties=exact)
        inv_z = 1.0 / st["z"]
        cnt1 = jnp.zeros(s1.shape, f32)
        for a in range(PEER_TOPK):
            cnt1 = jnp.where(ranks[0] == float(a), st["cnt"][a:a + 1], cnt1)
        c1_ref[hh, :, pl.ds(c0, LANE)] = jnp.exp(s1 - top1) * inv_z
        cnt1_ref[hh, :, pl.ds(c0, LANE)] = cnt1
        rank2_ref[hh, :, pl.ds(c0, LANE)] = ranks[1]
        e2_ref[hh, :, pl.ds(c0, LANE)] = jnp.exp(s2 - top2)
        if exact:
            return None
        want = float(PEER_TOPK)
        tied = st["taken"] != want
        for rank in ranks:
            tied = tied | (jnp.sum((rank < want).astype(f32), axis=0, keepdims=True) != want)
        return tied

    def body(it, carry):
        tied = process(it, exact=False)

        @pl.when(jnp.max(tied.astype(f32)) > 0.0)
        def _():
            process(it, exact=True)

        return carry

    lax.fori_loop(0, PEER_HEADS * n_chunk, body, 0)


def _peer_dense_kernel(hb_ref, u_ref, vt_ref, c1_ref, cnt1_ref, rank2_ref, e2_ref, x_ref, gm_ref,
                       o_ref, ht0_ref, ht1_ref, gh0_ref, gh1_ref, acc_ref):
    f32, bf16 = jnp.float32, jnp.bfloat16
    s = pl.program_id(1)
    tb = hb_ref.shape[0]
    n_rows = PEER_ET // N_KEYS
    sub = 16
    nt = (((1,), (1,)), ((), ()))

    @pl.when(s == 0)
    def _():
        acc_ref[...] = jnp.zeros_like(acc_ref)
        ht1_ref[...] = jnp.zeros_like(ht1_ref)
        gh0_ref[...] = jnp.zeros_like(gh0_ref)
        gh1_ref[...] = jnp.zeros_like(gh1_ref)

    def stages(ht_w, ht_r, gh_w, gh_r):
        ht_w[...] = lax.dot_general(u_ref[...], hb_ref[...], nt, preferred_element_type=f32)
        for c0 in range(0, tb, LANE):
            lanes = slice(c0, c0 + LANE)
            for s0 in range(0, N_KEYS, sub):
                g = [jnp.zeros((sub, LANE), f32) for _ in range(n_rows)]
                for hh in range(PEER_HEADS):
                    rk = rank2_ref[hh, s0:s0 + sub, lanes]
                    ev = e2_ref[hh, s0:s0 + sub, lanes]
                    for r in range(n_rows):
                        g[r] = g[r] + jnp.where(rk < cnt1_ref[hh, r:r + 1, lanes], ev * c1_ref[hh, r:r + 1, lanes], 0.0)
                for r in range(n_rows):
                    rows = slice(r * N_KEYS + s0, r * N_KEYS + s0 + sub)
                    pre = ht_r[rows, lanes]
                    act = 0.5 * pre * (1.0 + lax.erf(pre * (2.0 ** -0.5)))
                    gh_w[rows, lanes] = (g[r] * act).astype(bf16)
        acc_ref[...] += jnp.dot(vt_ref[...], gh_r[...], preferred_element_type=f32)

    @pl.when(s % 2 == 0)
    def _():
        stages(ht0_ref, ht1_ref, gh1_ref, gh0_ref)

    @pl.when(s % 2 == 1)
    def _():
        stages(ht1_ref, ht0_ref, gh0_ref, gh1_ref)

    @pl.when(s == pl.num_programs(1) - 1)
    def _():
        upd = acc_ref[...].T.reshape(x_ref.shape)
        o_ref[...] = x_ref[...] + gm_ref[...] * upd


def _peer_sublayer(x, shift, scale, gate, norm_g, wq_b, k1p, k2p, u_b, vt_b):
    B, T, D = x.shape
    n = B * T
    tb = PEER_TB
    assert n % tb == 0
    if T % tb == 0:
        nbs, tper, per = 1, tb, T // tb
        xmap = lambda i, *_: (i // per, i % per, 0)
        mmap = lambda i, *_: (i // per, 0, 0)
    else:
        assert tb % T == 0 and T % 8 == 0
        nbs, tper = tb // T, T
        xmap = lambda i, *_: (i, 0, 0)
        mmap = lambda i, *_: (i, 0, 0)
    nblk = n // tb
    f32 = jnp.float32
    x_spec = pl.BlockSpec((nbs, tper, D), xmap)
    m_spec = pl.BlockSpec((nbs, 1, D), mmap)
    n_cand_pad = -(-len(_PEER_CAND) // 8) * 8
    a_of = jnp.asarray(np.broadcast_to(np.array([a for a, _ in _PEER_CAND] + [0] * (n_cand_pad - len(_PEER_CAND)),
                                                np.int32)[:, None], (n_cand_pad, LANE)))
    route_shape = lambda dt: jax.ShapeDtypeStruct((PEER_HEADS, N_KEYS, n), dt)
    route_spec = pl.BlockSpec((PEER_HEADS, N_KEYS, tb), lambda i: (0, 0, i))
    hb, c1, cnt1, rank2, e2 = pl.pallas_call(
        _peer_route_kernel,
        grid=(nblk,),
        in_specs=[x_spec, m_spec, m_spec,
                  pl.BlockSpec((1, D), lambda i: (0, 0)),
                  pl.BlockSpec((D, PEER_HEADS * PEER_QDIM), lambda i: (0, 0)),
                  pl.BlockSpec((PEER_HEADS, N_KEYS, PEER_QDIM), lambda i: (0, 0, 0)),
                  pl.BlockSpec((PEER_HEADS, N_KEYS, PEER_QDIM), lambda i: (0, 0, 0)),
                  pl.BlockSpec((n_cand_pad, LANE), lambda i: (0, 0))],
        out_specs=[pl.BlockSpec((tb, D), lambda i: (i, 0)), route_spec, route_spec, route_spec, route_spec],
        out_shape=[jax.ShapeDtypeStruct((n, D), jnp.bfloat16)] + [route_shape(f32)] * 4,
        scratch_shapes=[pltpu.VMEM((PEER_HEADS, tb, PEER_QDIM), jnp.bfloat16), pltpu.VMEM((PEER_TOPK, LANE), f32),
                        pltpu.VMEM((PEER_TOPK, LANE), f32), pltpu.VMEM((n_cand_pad, LANE), f32)],
        compiler_params=pltpu.CompilerParams(dimension_semantics=("parallel",), vmem_limit_bytes=VMEM_LIMIT),
        name="peer_route",
    )(x, shift, scale, norm_g.reshape(1, D), wq_b, k1p, k2p, a_of)

    rows = PEER_ET // N_KEYS
    n_tiles = N_EXPERTS // PEER_ET
    tile = lambda s, lag: jnp.clip(s - lag, 0, n_tiles - 1)
    sub_spec = pl.BlockSpec((PEER_HEADS, rows, tb), lambda i, s: (0, tile(s, 1), i))
    full_spec = pl.BlockSpec((PEER_HEADS, N_KEYS, tb), lambda i, s: (0, 0, i))
    return pl.pallas_call(
        _peer_dense_kernel,
        grid=(nblk, n_tiles + 2),
        in_specs=[pl.BlockSpec((tb, D), lambda i, s: (i, 0)),
                  pl.BlockSpec((PEER_ET, D), lambda i, s: (tile(s, 0), 0)),
                  pl.BlockSpec((D, PEER_ET), lambda i, s: (0, tile(s, 2))),
                  sub_spec, sub_spec, full_spec, full_spec,
                  pl.BlockSpec((nbs, tper, D), lambda i, s: xmap(i)),
                  pl.BlockSpec((nbs, 1, D), lambda i, s: mmap(i))],
        out_specs=pl.BlockSpec((nbs, tper, D), lambda i, s: xmap(i)),
        out_shape=jax.ShapeDtypeStruct(x.shape, x.dtype),
        scratch_shapes=[pltpu.VMEM((PEER_ET, tb), f32), pltpu.VMEM((PEER_ET, tb), f32),
                        pltpu.VMEM((PEER_ET, tb), jnp.bfloat16), pltpu.VMEM((PEER_ET, tb), jnp.bfloat16),
                        pltpu.VMEM((D, tb), f32)],
        compiler_params=pltpu.CompilerParams(dimension_semantics=("parallel", "arbitrary"), vmem_limit_bytes=VMEM_LIMIT),
        name="peer_dense",
    )(hb, u_b, vt_b, c1, cnt1, rank2, e2, x, gate)


def _peer_weights(w_q, k1, k2, u_tab, v_tab):
    bf16 = jnp.bfloat16
    half = PEER_QDIM // 2
    k1p = jnp.pad(k1, ((0, 0), (0, 0), (0, half))).astype(bf16)
    k2p = jnp.pad(k2, ((0, 0), (0, 0), (half, 0))).astype(bf16)
    return w_q.astype(bf16), k1p, k2p, u_tab.astype(bf16), v_tab.astype(bf16).T


def _layer_pool(cache, j, page_table):
    return cache.reshape((-1,) + cache.shape[2:]), page_table + j * cache.shape[1]


def kernel(x_prompt, x_sample, cache_nsa_kv, cache_diff_kv, state_nsa_window, state_gdn_S, state_gdn_conv,
           page_table, c_prompt, c_sample, ada_w, ada_b, norm_mix_g, norm_ffn_g, final_norm_g,
           gdn_w_in, gdn_conv_w, gdn_a_log, gdn_dt_bias, gdn_norm_g, gdn_w_out,
           nsa_w_in, nsa_cmp_pos, nsa_cmp_w, nsa_w_out,
           diff_w_in, diff_lq1, diff_lk1, diff_lq2, diff_lk2, diff_subln_g, diff_w_out,
           peer_w_q, peer_k1, peer_k2, peer_u, peer_v):
    past_len = page_table.shape[1] * cache_nsa_kv.shape[2]
    peer_w = [_peer_weights(peer_w_q[i], peer_k1[i], peer_k2[i], peer_u[i], peer_v[i]) for i in range(DEPTH)]

    def trunk(x, c, sample):
        B, T, _ = x.shape
        pos0 = past_len if sample else 0
        cs = jax.nn.silu(c)
        new_S, new_conv, new_nsa_kv, new_nsa_win, new_diff_kv = [], [], [], [], []
        for i in range(DEPTH):
            mod = (cs @ ada_w[i] + ada_b[i]).reshape(B, 6, 1, D_MODEL)
            h = rmsnorm(x, norm_mix_g[i]) * (1.0 + mod[:, 1]) + mod[:, 0]
            j = i // N_MIXERS
            kind = i % N_MIXERS
            if kind == 0:
                if sample:
                    S0 = state_gdn_S[j].astype(jnp.float32)
                    buf = state_gdn_conv[j]
                else:
                    S0 = jnp.zeros((B, GDN_HEADS, GDN_DK, GDN_DV), jnp.float32)
                    buf = jnp.zeros((B, CONV_W - 1, GDN_HEADS * (2 * GDN_DK + GDN_DV)), x.dtype)
                m, buf_n, S_n = gdn_mixer(x, mod[:, 0], mod[:, 1], norm_mix_g[i], buf, S0, gdn_w_in[j], gdn_conv_w[j],
                                          gdn_a_log[j], gdn_dt_bias[j], gdn_norm_g[j], gdn_w_out[j])
                new_S.append(S_n.astype(x.dtype))
                new_conv.append(buf_n)
            elif kind == 1:
                past = _layer_pool(cache_nsa_kv, j, page_table) if sample else None
                wbuf = state_nsa_window[j] if sample else None
                m, kv_n, win_n = nsa_mixer(h, pos0, past, wbuf, nsa_w_in[j], nsa_cmp_pos[j], nsa_cmp_w[j], nsa_w_out[j])
                new_nsa_kv.append(kv_n)
                new_nsa_win.append(win_n)
            else:
                past = _layer_pool(cache_diff_kv, j, page_table) if sample else None
                m, kv_n = diff_mixer(h, pos0, past, i, diff_w_in[j], diff_lq1[j], diff_lk1[j], diff_lq2[j], diff_lk2[j], diff_subln_g[j], diff_w_out[j])
                new_diff_kv.append(kv_n)
            x = x + mod[:, 2] * m
            x = _peer_sublayer(x, mod[:, 3], mod[:, 4], mod[:, 5], norm_ffn_g[i], *peer_w[i])
        y = rmsnorm(x, final_norm_g)
        return y, jnp.stack(new_S), jnp.stack(new_conv), jnp.stack(new_nsa_kv), jnp.stack(new_nsa_win), jnp.stack(new_diff_kv)

    y_prompt, p_gdn_S, p_gdn_conv, p_nsa_kv, p_nsa_win, p_diff_kv = trunk(x_prompt, c_prompt, False)
    y_sample, s_gdn_S, s_gdn_conv, s_nsa_kv, s_nsa_win, s_diff_kv = trunk(x_sample, c_sample, True)
    return (y_prompt, y_sample, p_gdn_S, p_gdn_conv, p_nsa_kv, p_nsa_win, p_diff_kv, s_gdn_S, s_gdn_conv, s_nsa_kv, s_nsa_win, s_diff_kv)
```

```python
import functools
import math

import jax
import jax.numpy as jnp
import numpy as np
from jax import lax
from jax.experimental import pallas as pl
from jax.experimental.pallas import tpu as pltpu

D_MODEL = 1024
DEPTH = 4
N_MIXERS = 3
HEAD_DIM = 128
GDN_HEADS = D_MODEL // HEAD_DIM
GDN_DK = HEAD_DIM
GDN_DV = HEAD_DIM
CONV_W = 4
GDN_CHUNK = 64
NSA_HEADS = D_MODEL // HEAD_DIM
NSA_KV_GROUPS = 2
CMP_BLOCK = 32
SEL_BLOCK = 64
TOP_N = 16
WINDOW = 512
Q_BLOCK = 128
DIFF_HEADS = D_MODEL // HEAD_DIM
DIFF_DH = D_MODEL // DIFF_HEADS // 2
PEER_HEADS = 8
N_KEYS = 128
N_EXPERTS = N_KEYS * N_KEYS
PEER_TOPK = 16
PEER_QDIM = 128
ROPE_THETA = 10000.0
EPS = 1e-6

LANE = 128
VMEM_LIMIT = 48 * 1024 * 1024


def _mm_kernel(x_ref, w_ref, o_ref):
    o_ref[...] = jnp.dot(x_ref[...].astype(jnp.bfloat16), w_ref[...], preferred_element_type=jnp.float32)


def _mm(x, w, tm=512):
    lead = x.shape[:-1]
    K = x.shape[-1]
    N = w.shape[1]
    x2 = x.reshape(-1, K)
    M = x2.shape[0]
    n_pad = -(-N // 256) * 256
    wb = jnp.pad(w, ((0, 0), (0, n_pad - N))).astype(jnp.bfloat16)
    tm = min(tm, M)
    assert M % tm == 0
    out = pl.pallas_call(
        _mm_kernel,
        grid=(M // tm,),
        in_specs=[pl.BlockSpec((tm, K), lambda i: (i, 0)), pl.BlockSpec((K, n_pad), lambda i: (0, 0))],
        out_specs=pl.BlockSpec((tm, n_pad), lambda i: (i, 0)),
        out_shape=jax.ShapeDtypeStruct((M, n_pad), jnp.float32),
        compiler_params=pltpu.CompilerParams(dimension_semantics=("parallel",), vmem_limit_bytes=VMEM_LIMIT),
        name="proj_matmul",
    )(x2, wb)
    return out[:, :N].reshape(*lead, N)


def rmsnorm(x, g):
    xf = x.astype(jnp.float32)
    y = xf * lax.rsqrt(jnp.mean(xf * xf, -1, keepdims=True) + EPS)
    return (y * g.astype(jnp.float32)).astype(x.dtype)


def rope(x, pos):
    half = x.shape[-1] // 2
    inv = ROPE_THETA ** (-jnp.arange(half, dtype=jnp.float32) / half)
    ang = pos.astype(jnp.float32)[:, None] * inv[None, :]
    cos = jnp.cos(ang)[:, None, :]
    sin = jnp.sin(ang)[:, None, :]
    xf = x.astype(jnp.float32)
    x1, x2 = xf[..., :half], xf[..., half:]
    return jnp.concatenate([x1 * cos - x2 * sin, x2 * cos + x1 * sin], -1).astype(x.dtype)


GDN_TM = 256
_CARRY = 8


def _gdn_in_kernel(x_ref, shift_ref, scale_ref, ng_ref, w_ref, cw_ref, buf_ref, dyn_ref,
                   q_ref, k_ref, v_ref, z_ref, gb_ref, conv_ref, carry):
    f32 = jnp.float32
    i = pl.program_id(1)
    H, dk, dv = GDN_HEADS, GDN_DK, GDN_DV
    n_qkv = H * (2 * dk + dv)
    tm = x_ref.shape[1]

    @pl.when(i == 0)
    def _():
        carry[...] = buf_ref[0]

    x = x_ref[0]
    y = x * lax.rsqrt(jnp.mean(x * x, -1, keepdims=True) + EPS) * ng_ref[...]
    h = y * (1.0 + scale_ref[0]) + shift_ref[0]
    proj = jnp.dot(h.astype(jnp.bfloat16), w_ref[...], preferred_element_type=f32)
    xx = jnp.concatenate([carry[...], proj[:, :n_qkv]], axis=0)
    first = _CARRY - (CONV_W - 1)
    conv = xx[first:first + tm] * cw_ref[0:1, :]
    for j in range(1, CONV_W):
        conv = conv + xx[first + j:first + j + tm] * cw_ref[j:j + 1, :]
    act = conv * jax.nn.sigmoid(conv)
    carry[...] = xx[tm:tm + _CARRY]
    for hh in range(H):
        for ref, base in ((q_ref, 0), (k_ref, H * dk)):
            a = act[:, base + hh * dk:base + (hh + 1) * dk]
            ref[0, :, hh * dk:(hh + 1) * dk] = a * lax.rsqrt(jnp.sum(a * a, -1, keepdims=True) + EPS)
    v_ref[0] = act[:, 2 * H * dk:]
    z_ref[0] = proj[:, n_qkv:n_qkv + H * dv]
    tail = proj[:, n_qkv + H * dv:n_qkv + H * dv + LANE]
    t2 = tail + dyn_ref[1:2, :]
    softplus = jnp.maximum(t2, 0.0) + jnp.log(1.0 + jnp.exp(-jnp.abs(t2)))
    lane = lax.broadcasted_iota(jnp.int32, tail.shape, 1)
    gb = jnp.where(lane < H, jax.nn.sigmoid(tail), dyn_ref[0:1, :] * softplus)
    gb_ref[0] = gb[:, :2 * H]

    @pl.when(i == pl.num_programs(1) - 1)
    def _():
        conv_ref[0] = xx[tm:tm + _CARRY]


def gdn_mixer(x, shift, scale, mix_g, conv_buf, S0, w_in, conv_w, a_log, dt_bias, norm_g, w_out):
    B, T, D = x.shape
    H, dk, dv = GDN_HEADS, GDN_DK, GDN_DV
    n_qkv = H * (2 * dk + dv)
    f32 = jnp.float32
    tm = GDN_TM if T % GDN_TM == 0 else T
    assert T % tm == 0 and tm % 8 == 0
    n_cols = w_in.shape[1]
    n_pad = n_qkv + H * dv + LANE
    assert n_cols == n_qkv + H * dv + 2 * H
    wb = jnp.pad(w_in, ((0, 0), (0, n_pad - n_cols))).astype(jnp.bfloat16)
    buf8 = jnp.pad(conv_buf, ((0, 0), (_CARRY - (CONV_W - 1), 0), (0, 0)))
    dyn = jnp.zeros((2, LANE), f32).at[0, H:2 * H].set(-jnp.exp(a_log.astype(f32))).at[1, H:2 * H].set(dt_bias.astype(f32))
    tok = lambda w: pl.BlockSpec((1, tm, w), lambda b, i: (b, i, 0))
    per_seq = lambda r, w: pl.BlockSpec((1, r, w), lambda b, i: (b, 0, 0))
    whole = lambda r, w: pl.BlockSpec((r, w), lambda b, i: (0, 0))
    q, k, v, z, gb, conv8 = pl.pallas_call(
        _gdn_in_kernel,
        grid=(B, T // tm),
        in_specs=[tok(D), per_seq(1, D), per_seq(1, D), whole(1, D), whole(D, n_pad), whole(CONV_W, n_qkv),
                  per_seq(_CARRY, n_qkv), whole(2, LANE)],
        out_specs=[tok(H * dk), tok(H * dk), tok(H * dv), tok(H * dv), tok(2 * H), per_seq(_CARRY, n_qkv)],
        out_shape=[jax.ShapeDtypeStruct((B, T, H * dk), f32), jax.ShapeDtypeStruct((B, T, H * dk), f32),
                   jax.ShapeDtypeStruct((B, T, H * dv), f32), jax.ShapeDtypeStruct((B, T, H * dv), f32),
                   jax.ShapeDtypeStruct((B, T, 2 * H), f32), jax.ShapeDtypeStruct((B, _CARRY, n_qkv), f32)],
        scratch_shapes=[pltpu.VMEM((_CARRY, n_qkv), f32)],
        compiler_params=pltpu.CompilerParams(dimension_semantics=("parallel", "arbitrary"), vmem_limit_bytes=VMEM_LIMIT),
        name="gdn_in",
    )(x, shift, scale, mix_g.reshape(1, D), wb, conv_w, buf8, dyn)
    o, S = _gdn_delta_rule(q, k, v, z, gb[..., H:], gb[..., :H], S0, norm_g)
    return _mm(o, w_out), conv8[:, _CARRY - (CONV_W - 1):], S


def _dot_split3(a, b):
    f32, bf16 = jnp.float32, jnp.bfloat16
    ah = a.astype(bf16)
    al = (a - ah.astype(f32)).astype(bf16)
    bh = b.astype(bf16)
    bl = (b - bh.astype(f32)).astype(bf16)
    dot = functools.partial(jnp.dot, preferred_element_type=f32)
    return dot(ah, bh) + dot(ah, bl) + dot(al, bh)


def _gdn_chunk_kernel(q_ref, k_ref, v_ref, z_ref, col_ref, row_ref, s0_ref, ng_ref, o_ref, sout_ref, s_scr):
    f32, bf16 = jnp.float32, jnp.bfloat16
    n = pl.program_id(1)
    C = q_ref.shape[1]
    H, dk, dv = GDN_HEADS, GDN_DK, GDN_DV
    dot = functools.partial(jnp.dot, preferred_element_type=f32)
    nt = (((1,), (1,)), ((), ()))

    @pl.when(n == 0)
    def _():
        s_scr[...] = s0_ref[0]

    ii = lax.broadcasted_iota(jnp.int32, (C, C), 0)
    jj = lax.broadcasted_iota(jnp.int32, (C, C), 1)
    incl = ii >= jj
    strict = ii > jj
    eye = (ii == jj).astype(f32)
    heads = range(H)
    ks = [slice(h * dk, (h + 1) * dk) for h in heads]
    vs = [slice(h * dv, (h + 1) * dv) for h in heads]
    gc = [col_ref[0, :, h:h + 1] for h in heads]
    bc = [col_ref[0, :, H + h:H + h + 1] for h in heads]
    gr = [row_ref[0, 0, h:h + 1, :] for h in heads]
    decay = [jnp.exp(jnp.where(incl, gc[h] - gr[h], -jnp.inf)) for h in heads]
    kcb = [k_ref[0, :, ks[h]].astype(bf16) for h in heads]
    kb = [k_ref[0, :, ks[h]] * bc[h] for h in heads]
    mpow = [-jnp.where(strict, lax.dot_general(kb[h].astype(bf16), kcb[h], nt, preferred_element_type=f32) * decay[h], 0.0)
            for h in heads]
    tinv = [eye + mpow[h] for h in heads]
    for _ in range(C.bit_length() - 2):
        mpow = [_dot_split3(mpow[h], mpow[h]) for h in heads]
        tinv = [tinv[h] + _dot_split3(tinv[h], mpow[h]) for h in heads]
    eg = [jnp.exp(gc[h]) for h in heads]
    uw = [dot(tinv[h].astype(bf16), jnp.concatenate([v_ref[0, :, vs[h]] * bc[h], kb[h] * eg[h]], axis=1).astype(bf16))
          for h in heads]
    qh = [q_ref[0, :, ks[h]] * dk ** -0.5 for h in heads]
    a_intra = [(lax.dot_general(qh[h].astype(bf16), kcb[h], nt, preferred_element_type=f32) * decay[h]).astype(bf16)
               for h in heads]
    s_b = [s_scr[h].astype(bf16) for h in heads]
    v_nb = [(uw[h][:, :dv] - dot(uw[h][:, dv:].astype(bf16), s_b[h])).astype(bf16) for h in heads]
    o = [dot((qh[h] * eg[h]).astype(bf16), s_b[h]) + dot(a_intra[h], v_nb[h]) for h in heads]
    for h in heads:
        g_last = gr[h][:, C - 1:C]
        kd = (k_ref[0, :, ks[h]] * jnp.exp(g_last - gc[h])).astype(bf16)
        s_scr[h] = s_scr[h] * jnp.exp(g_last) + lax.dot_general(kd, v_nb[h], _TN, preferred_element_type=f32)
    for h in heads:
        y = o[h] * lax.rsqrt(jnp.mean(o[h] * o[h], -1, keepdims=True) + EPS) * ng_ref[...]
        zz = z_ref[0, :, vs[h]]
        o_ref[0, :, vs[h]] = y * (zz * jax.nn.sigmoid(zz))

    @pl.when(n == pl.num_programs(1) - 1)
    def _():
        sout_ref[0] = s_scr[...]


def _gdn_delta_rule(q, k, v, z, g, beta, S0, norm_g):
    B, T, _ = q.shape
    H, dk, dv = GDN_HEADS, GDN_DK, GDN_DV
    C = GDN_CHUNK if T % GDN_CHUNK == 0 else T
    assert C & (C - 1) == 0 and C % 8 == 0
    nC = T // C
    f32 = jnp.float32
    gcum = jnp.cumsum(g.astype(f32).reshape(B, nC, C, H), axis=2)
    col = jnp.concatenate([gcum.reshape(B, T, H), beta.astype(f32)], axis=-1)
    row = jnp.swapaxes(gcum, 2, 3)
    tok = lambda w: pl.BlockSpec((1, C, w), lambda b, n: (b, n, 0))
    o, s_out = pl.pallas_call(
        _gdn_chunk_kernel,
        grid=(B, nC),
        in_specs=[tok(H * dk), tok(H * dk), tok(H * dv), tok(H * dv), tok(2 * H),
                  pl.BlockSpec((1, 1, H, C), lambda b, n: (b, n, 0, 0)),
                  pl.BlockSpec((1, H, dk, dv), lambda b, n: (b, 0, 0, 0)),
                  pl.BlockSpec((1, dv), lambda b, n: (0, 0))],
        out_specs=[tok(H * dv), pl.BlockSpec((1, H, dk, dv), lambda b, n: (b, 0, 0, 0))],
        out_shape=[jax.ShapeDtypeStruct((B, T, H * dv), f32), jax.ShapeDtypeStruct((B, H, dk, dv), f32)],
        scratch_shapes=[pltpu.VMEM((H, dk, dv), f32)],
        compiler_params=pltpu.CompilerParams(dimension_semantics=("parallel", "arbitrary"), vmem_limit_bytes=VMEM_LIMIT),
        name="gdn_delta_rule",
    )(q, k, v, z, col, row, S0.astype(f32), norm_g.reshape(1, dv))
    return o, s_out


NSA_R = NSA_HEADS // NSA_KV_GROUPS
NSA_TK = 1024
NSA_WSPAN = WINDOW + Q_BLOCK


def _softmax_rows(s, mask):
    s = jnp.where(mask, s, -jnp.inf)
    m = jnp.max(s, axis=0, keepdims=True)
    m = jnp.where(m == -jnp.inf, 0.0, m)
    p = jnp.exp(s - m)
    return p * (1.0 / jnp.maximum(jnp.sum(p, axis=0, keepdims=True), 1e-30))


def _nsa_cmp_kernel(x_ref, pos_ref, w_ref, o_ref):
    xb = (x_ref[0, 0, 0, 0] + pos_ref[0]).astype(jnp.bfloat16)
    o_ref[0, 0, 0, 0] = jnp.dot(xb, w_ref[0], preferred_element_type=jnp.float32)


def _nsa_compress(rows_cmp, cmp_pos, cmp_w):
    B, T, _, G, dh = rows_cmp.shape
    n_sel = T // SEL_BLOCK
    ld = CMP_BLOCK * dh
    x = rows_cmp.reshape(B, n_sel, 2, CMP_BLOCK, 2, G, dh)
    x = jnp.transpose(x, (0, 4, 5, 2, 1, 3, 6)).reshape(B, 2, G, 2, n_sel, ld)
    return pl.pallas_call(
        _nsa_cmp_kernel,
        grid=(B, 2, G, 2),
        in_specs=[pl.BlockSpec((1, 1, 1, 1, n_sel, ld), lambda b, s, g, p: (b, s, g, p, 0, 0)),
                  pl.BlockSpec((1, 1, ld), lambda b, s, g, p: (s, 0, 0)),
                  pl.BlockSpec((1, ld, dh), lambda b, s, g, p: (s, 0, 0))],
        out_specs=pl.BlockSpec((1, 1, 1, 1, n_sel, dh), lambda b, s, g, p: (b, s, g, p, 0, 0)),
        out_shape=jax.ShapeDtypeStruct((B, 2, G, 2, n_sel, dh), jnp.float32),
        compiler_params=pltpu.CompilerParams(dimension_semantics=("parallel",) * 4, vmem_limit_bytes=VMEM_LIMIT),
        name="nsa_compress",
    )(x, cmp_pos.reshape(2, 1, ld), cmp_w.reshape(2, ld, dh).astype(jnp.bfloat16))


def _nsa_prompt_kernel(q_ref, gt_ref, kc_ref, vct_ref, ks_ref, vst_ref, kw_ref, vwt_ref, o_ref,
                       sel_scr, m_scr, l_scr, acc_scr):
    f32, bf16 = jnp.float32, jnp.bfloat16
    i = pl.program_id(2)
    R, QB = NSA_R, Q_BLOCK
    W = R * QB
    T = ks_ref.shape[2]
    n_sel = T // SEL_BLOCK
    nt = (((1,), (1,)), ((), ()))
    qb = q_ref[0]
    q = jnp.concatenate([qb[:, r * HEAD_DIM:(r + 1) * HEAD_DIM] for r in range(R)], axis=0)
    q = (q * HEAD_DIM ** -0.5).astype(bf16)
    lane = lax.broadcasted_iota(jnp.int32, (1, W), 1)
    qpos = i * QB + (lane & (QB - 1))

    s = lax.dot_general(kc_ref[0, 0], q, nt, preferred_element_type=f32)
    row = lax.broadcasted_iota(jnp.int32, (2 * n_sel, 1), 0)
    cidx = jnp.where(row < n_sel, 2 * row, 2 * (row - n_sel) + 1)
    p_c = _softmax_rows(s, (cidx + 1) * CMP_BLOCK - 1 <= qpos)
    o_c = jnp.dot(vct_ref[0, 0], p_c.astype(bf16), preferred_element_type=f32)

    p_r = p_c[:, 0:QB]
    for r in range(1, R):
        p_r = p_r + p_c[:, r * QB:(r + 1) * QB]
    imp = p_r[:n_sel] + p_r[n_sel:]
    blk = lax.broadcasted_iota(jnp.int32, (n_sel, QB), 0)
    qp = qpos[:, :QB]
    imp = jnp.where(blk == qp // SEL_BLOCK, jnp.inf, jnp.where(blk * SEL_BLOCK <= qp, imp, -jnp.inf))
    chosen = [jnp.zeros((n_sel, QB), f32)]

    def pick(r, m, sel, chosen=chosen):
        chosen[0] = jnp.where(sel & (m > -jnp.inf), 1.0, chosen[0])

    _extract_top(imp, min(TOP_N, n_sel), pick)
    sel_scr[...] = jnp.concatenate([chosen[0]] * R, axis=1)

    m_scr[...] = jnp.full(m_scr.shape, -jnp.inf, f32)
    l_scr[...] = jnp.zeros(l_scr.shape, f32)
    acc_scr[...] = jnp.zeros(acc_scr.shape, f32)
    blk_per_tile = NSA_TK // SEL_BLOCK

    def tile_body(kt, carry):
        k0 = pl.multiple_of(kt * NSA_TK, NSA_TK)
        s = lax.dot_general(ks_ref[0, 0, pl.ds(k0, NSA_TK), :], q, nt, preferred_element_type=f32)
        selrows = sel_scr[pl.ds(pl.multiple_of(kt * blk_per_tile, blk_per_tile), blk_per_tile), :]
        kpos = k0 + lax.broadcasted_iota(jnp.int32, (NSA_TK, 1), 0)
        parts = []
        for j in range(blk_per_tile):
            rows = slice(j * SEL_BLOCK, (j + 1) * SEL_BLOCK)
            ok = (selrows[j:j + 1, :] > 0.0) & (kpos[rows] <= qpos)
            parts.append(jnp.where(ok, s[rows], -jnp.inf))
        _online_softmax_tile(jnp.concatenate(parts, axis=0), vst_ref[0, 0, :, pl.ds(k0, NSA_TK)], m_scr, l_scr, acc_scr)
        return carry

    lax.fori_loop(0, (i * QB + QB + NSA_TK - 1) // NSA_TK, tile_body, 0)
    o_s = acc_scr[...] * (1.0 / jnp.maximum(l_scr[...], 1e-30))

    w0 = pl.multiple_of(jnp.clip(i * QB + QB - NSA_WSPAN, 0, T - NSA_WSPAN), QB)
    s = lax.dot_general(kw_ref[0, 0, pl.ds(w0, NSA_WSPAN), :], q, nt, preferred_element_type=f32)
    kpos = w0 + lax.broadcasted_iota(jnp.int32, (NSA_WSPAN, 1), 0)
    p_w = _softmax_rows(s, (kpos <= qpos) & (kpos > qpos - WINDOW))
    o_w = jnp.dot(vwt_ref[0, 0, :, pl.ds(w0, NSA_WSPAN)], p_w.astype(bf16), preferred_element_type=f32)

    g = gt_ref[0, 0, 0]
    o = g[0:1] * o_c + g[1:2] * o_s + g[2:3] * o_w
    for r in range(R):
        o_ref[0, :, r * HEAD_DIM:(r + 1) * HEAD_DIM] = o[:, r * QB:(r + 1) * QB].T


def _nsa_prompt_attention(q, gates, new_rows, win, cmp_pos, cmp_w):
    B, T, _ = q.shape
    G, R, dh, QB = NSA_KV_GROUPS, NSA_R, HEAD_DIM, Q_BLOCK
    assert T % NSA_TK == 0 and T >= NSA_WSPAN and (T // SEL_BLOCK) % 8 == 0
    nqb = T // QB
    n_sel = T // SEL_BLOCK
    W = R * QB
    bf16 = jnp.bfloat16
    cmp = _nsa_compress(new_rows[:, :, :2], cmp_pos, cmp_w).reshape(B, 2, G, 2 * n_sel, dh)
    kc = cmp[:, 0].astype(bf16)
    vct = jnp.swapaxes(cmp[:, 1], -1, -2).astype(bf16)
    ks = jnp.transpose(new_rows[:, :, 2], (0, 2, 1, 3)).astype(bf16)
    vst = jnp.transpose(new_rows[:, :, 3], (0, 2, 3, 1)).astype(bf16)
    kw = jnp.transpose(win[:, :, 0], (0, 2, 1, 3)).astype(bf16)
    vwt = jnp.transpose(win[:, :, 1], (0, 2, 3, 1)).astype(bf16)
    gt = jnp.transpose(gates.reshape(B, nqb, QB, G, R, 3), (0, 3, 1, 5, 4, 2)).reshape(B, G, nqb, 3, W)
    gt = jnp.pad(gt, ((0, 0), (0, 0), (0, 0), (0, 5), (0, 0)))
    per_bg = lambda *shape: pl.BlockSpec((1, 1) + shape, lambda b, g, i: (b, g, 0, 0))
    return pl.pallas_call(
        _nsa_prompt_kernel,
        grid=(B, G, nqb),
        in_specs=[pl.BlockSpec((1, QB, R * dh), lambda b, g, i: (b, i, g)),
                  pl.BlockSpec((1, 1, 1, 8, W), lambda b, g, i: (b, g, i, 0, 0)),
                  per_bg(2 * n_sel, dh), per_bg(dh, 2 * n_sel), per_bg(T, dh), per_bg(dh, T), per_bg(T, dh), per_bg(dh, T)],
        out_specs=pl.BlockSpec((1, QB, R * dh), lambda b, g, i: (b, i, g)),
        out_shape=jax.ShapeDtypeStruct(q.shape, jnp.float32),
        scratch_shapes=[pltpu.VMEM((n_sel, W), jnp.float32), pltpu.VMEM((1, W), jnp.float32),
                        pltpu.VMEM((1, W), jnp.float32), pltpu.VMEM((dh, W), jnp.float32)],
        compiler_params=pltpu.CompilerParams(dimension_semantics=("parallel", "parallel", "arbitrary"),
                                             vmem_limit_bytes=VMEM_LIMIT),
        name="nsa_prompt",
    )(q, gt, kc, vct, ks, vst, kw, vwt)


_TN = (((0,), (0,)), ((), ()))
_NT = (((1,), (1,)), ((), ()))


def _nsa_sample_kernel(pt_ref, *refs, n_pages, past_len):
    pages = refs[:n_pages]
    q_ref, rows_ref, wnew_ref, wbuf_ref, gt_ref, pos_ref, w_ref, o_ref, sel_scr, xs_scr, pr_scr = refs[n_pages:]
    f32, bf16 = jnp.float32, jnp.bfloat16
    G, R, dh = NSA_KV_GROUPS, NSA_R, HEAD_DIM
    T = q_ref.shape[1]
    n_blk = past_len // SEL_BLOCK
    n_w = wbuf_ref.shape[1]
    lane = lax.broadcasted_iota(jnp.int32, (1, LANE), 1)
    qidx = lane & (T - 1)
    qpos = past_len + qidx
    new_row = lax.broadcasted_iota(jnp.int32, (LANE, 1), 0)
    new_ok = (new_row < T) & (new_row <= qidx)

    def new_tile(ref, col):
        return jnp.concatenate([ref[0, :, col * dh:(col + 1) * dh], jnp.zeros((LANE - T, dh), f32)], axis=0).astype(bf16)

    page = pages[0].shape[1] // (4 * G)
    per_page = page // CMP_BLOCK
    n_cmp = n_pages * per_page
    cmp = []
    for s in range(2):
        for g in range(G):
            for p, pg in enumerate(pages):
                for l in range(CMP_BLOCK):
                    xs_scr[g * n_cmp + p * per_page:g * n_cmp + (p + 1) * per_page, l * dh:(l + 1) * dh] = (
                        pg[0, pl.ds(l * 4 * G + s * G + g, per_page, stride=CMP_BLOCK * 4 * G), :])
        cmp.append(jnp.dot((xs_scr[...] + pos_ref[s]).astype(bf16), w_ref[s], preferred_element_type=f32))
    for g in range(G):
        q = jnp.concatenate([q_ref[0, :, (g * R + r) * dh:(g * R + r + 1) * dh] for r in range(R)]
                            + [jnp.zeros((LANE - R * T, dh), f32)], axis=0)
        q = (q * dh ** -0.5).astype(bf16)
        kc = cmp[0][g * 2 * n_blk:(g + 1) * 2 * n_blk]
        vc = cmp[1][g * 2 * n_blk:(g + 1) * 2 * n_blk]
        s_c = lax.dot_general(kc.astype(bf16), q, _NT, preferred_element_type=f32)
        p_c = _softmax_rows(s_c, jnp.full(s_c.shape, True))
        o_c = lax.dot_general(vc.astype(bf16), p_c.astype(bf16), _TN, preferred_element_type=f32)
        p_r = p_c
        for r in range(1, R):
            p_r = p_r + pltpu.roll(p_c, LANE - r * T, axis=1)
        pr_scr[...] = p_r
        imp = pr_scr[pl.ds(0, n_blk, stride=2), :] + pr_scr[pl.ds(1, n_blk, stride=2), :]
        chosen = [jnp.zeros((n_blk, LANE), f32)]

        def pick(_, m, sel, chosen=chosen):
            chosen[0] = jnp.where(sel, 1.0, chosen[0])

        _extract_top(imp, min(TOP_N, n_blk + 1) - 1, pick)
        ch = jnp.where(lane < T, chosen[0], 0.0)
        ch4 = ch
        for r in range(1, R):
            ch4 = ch4 + pltpu.roll(ch, r * T, axis=1)
        sel_scr[...] = ch4
        ks = jnp.concatenate([pg[0, pl.ds(2 * G + g, page, stride=4 * G), :] for pg in pages], axis=0).astype(bf16)
        vs = jnp.concatenate([pg[0, pl.ds(3 * G + g, page, stride=4 * G), :] for pg in pages], axis=0).astype(bf16)
        s_p = lax.dot_general(ks, q, _NT, preferred_element_type=f32)
        s_p = jnp.concatenate([jnp.where(sel_scr[j:j + 1, :] > 0.0, s_p[j * SEL_BLOCK:(j + 1) * SEL_BLOCK], -jnp.inf)
                               for j in range(n_blk)], axis=0)
        s_n = lax.dot_general(new_tile(rows_ref, 2 * G + g), q, _NT, preferred_element_type=f32)
        s_all = jnp.concatenate([s_p, jnp.where(new_ok, s_n, -jnp.inf)], axis=0)
        p_s = _softmax_rows(s_all, s_all > -jnp.inf).astype(bf16)
        o_s = (lax.dot_general(vs, p_s[:past_len], _TN, preferred_element_type=f32)
               + lax.dot_general(new_tile(rows_ref, 3 * G + g), p_s[past_len:], _TN, preferred_element_type=f32))
        kw = wbuf_ref[0, :, g * dh:(g + 1) * dh].astype(bf16)
        vw = wbuf_ref[0, :, (G + g) * dh:(G + g + 1) * dh].astype(bf16)
        s_w = lax.dot_general(kw, q, _NT, preferred_element_type=f32)
        wpos = past_len - n_w + lax.broadcasted_iota(jnp.int32, (n_w, 1), 0)
        s_w = jnp.where(wpos > qpos - WINDOW, s_w, -jnp.inf)
        s_n = lax.dot_general(new_tile(wnew_ref, g), q, _NT, preferred_element_type=f32)
        s_all = jnp.concatenate([s_w, jnp.where(new_ok, s_n, -jnp.inf)], axis=0)
        p_w = _softmax_rows(s_all, s_all > -jnp.inf).astype(bf16)
        o_w = (lax.dot_general(vw, p_w[:n_w], _TN, preferred_element_type=f32)
               + lax.dot_general(new_tile(wnew_ref, G + g), p_w[n_w:], _TN, preferred_element_type=f32))
        gt = gt_ref[0, g]
        o = (gt[0:1] * o_c + gt[1:2] * o_s + gt[2:3] * o_w).T
        for r in range(R):
            o_ref[0, :, (g * R + r) * dh:(g * R + r + 1) * dh] = o[r * T:(r + 1) * T]


def _nsa_sample_attention(q, gates, new_rows, win_new, cache, page_table, wbuf, cmp_pos, cmp_w):
    B, T, _ = q.shape
    G, R, dh = NSA_KV_GROUPS, NSA_R, HEAD_DIM
    n_pool, page = cache.shape[:2]
    n_pages = page_table.shape[1]
    past_len = n_pages * page
    n_w = wbuf.shape[1]
    assert T == 8 and R * T <= LANE and page % SEL_BLOCK == 0 and n_w + T > WINDOW
    assert past_len % SEL_BLOCK == 0 and T <= CMP_BLOCK
    row_w = 4 * G * dh
    per_page = page // SEL_BLOCK
    f32 = jnp.float32
    cache2 = cache.reshape(n_pool, page * 4 * G, dh)
    n_blk = past_len // SEL_BLOCK
    gt = jnp.transpose(gates.reshape(B, T, G, R, 3), (0, 2, 4, 3, 1)).reshape(B, G, 3, R * T)
    gt = jnp.pad(gt, ((0, 0), (0, 0), (0, 5), (0, LANE - R * T)))
    page_spec = lambda p: pl.BlockSpec((1, page * 4 * G, dh), lambda b, pt: (pt[b, p], 0, 0))
    seq = lambda *shape: pl.BlockSpec((1,) + shape, lambda b, pt: (b,) + (0,) * len(shape))
    whole = lambda *shape: pl.BlockSpec(shape, lambda b, pt: (0,) * len(shape))
    grid_spec = pltpu.PrefetchScalarGridSpec(
        num_scalar_prefetch=1,
        grid=(B,),
        in_specs=[page_spec(p) for p in range(n_pages)]
                 + [seq(T, R * G * dh), seq(T, row_w), seq(T, 2 * G * dh), seq(n_w, 2 * G * dh), seq(G, 8, LANE),
                    whole(2, 1, CMP_BLOCK * dh), whole(2, CMP_BLOCK * dh, dh)],
        out_specs=seq(T, R * G * dh),
        scratch_shapes=[pltpu.VMEM((n_blk, LANE), f32), pltpu.VMEM((G * 2 * n_blk, CMP_BLOCK * dh), f32),
                        pltpu.VMEM((2 * n_blk, LANE), f32)],
    )
    return pl.pallas_call(
        functools.partial(_nsa_sample_kernel, n_pages=n_pages, past_len=past_len),
        grid_spec=grid_spec,
        out_shape=jax.ShapeDtypeStruct(q.shape, f32),
        compiler_params=pltpu.CompilerParams(dimension_semantics=("parallel",), vmem_limit_bytes=VMEM_LIMIT),
        name="nsa_sample",
    )(page_table, *([cache2] * n_pages), q, new_rows.reshape(B, T, row_w), win_new.reshape(B, T, 2 * G * dh),
      wbuf.reshape(B, n_w, 2 * G * dh), gt, cmp_pos.reshape(2, 1, CMP_BLOCK * dh),
      cmp_w.reshape(2, CMP_BLOCK * dh, dh).astype(jnp.bfloat16))


def nsa_mixer(h, pos0, past, wbuf, w_in, cmp_pos, cmp_w, w_out):
    B, T, _ = h.shape
    G, dh = NSA_KV_GROUPS, HEAD_DIM
    qd = NSA_HEADS * dh
    kvd = 6 * G * dh
    pos = pos0 + jnp.arange(T, dtype=jnp.int32)
    proj = _mm(h, w_in)
    q = rope(proj[..., :qd].reshape(B, T, NSA_HEADS, dh), pos)
    kv = proj[..., qd:qd + kvd].reshape(B, T, 6, G, dh)
    gates = jax.nn.sigmoid(proj[..., qd + kvd:].reshape(B, T, NSA_HEADS, 3))
    keys = rope(kv[:, :, 0::2].reshape(B, T, 3 * G, dh), pos).reshape(B, T, 3, G, dh)
    kv = jnp.stack([keys[:, :, 0], kv[:, :, 1], keys[:, :, 1], kv[:, :, 3], keys[:, :, 2], kv[:, :, 5]], axis=2)
    new_rows = kv[:, :, :4]
    win = kv[:, :, 4:] if past is None else jnp.concatenate([wbuf, kv[:, :, 4:]], 1)
    if past is None:
        o = _nsa_prompt_attention(q.reshape(B, T, qd), gates, new_rows, win, cmp_pos, cmp_w)
    else:
        cache, page_table = past
        o = _nsa_sample_attention(q.reshape(B, T, qd), gates, new_rows, kv[:, :, 4:], cache, page_table, wbuf,
                                  cmp_pos, cmp_w)
    new_win = win[:, -min(WINDOW, win.shape[1]):]
    return _mm(o.reshape(B, T, D_MODEL), w_out), new_rows, new_win


DIFF_QB = 512
DIFF_TK = 1024


def _online_softmax_tile(s, vt_tile, m_scr, l_scr, acc_scr):
    m_old = m_scr[...]
    m_new = jnp.maximum(m_old, jnp.max(s, axis=0, keepdims=True))
    m_safe = jnp.where(m_new == -jnp.inf, 0.0, m_new)
    alpha = jnp.exp(m_old - m_safe)
    p = jnp.exp(s - m_safe)
    l_scr[...] = alpha * l_scr[...] + jnp.sum(p, axis=0, keepdims=True)
    acc_scr[...] = alpha * acc_scr[...] + jnp.dot(vt_tile, p.astype(jnp.bfloat16), preferred_element_type=jnp.float32)
    m_scr[...] = m_new


def _diff_prompt_kernel(lam_ref, q_ref, k_ref, vt_ref, g_ref, o_ref, m_scr, l_scr, acc_scr, *, out_scale):
    f32, bf16 = jnp.float32, jnp.bfloat16
    i = pl.program_id(2)
    QB, TK, dd = DIFF_QB, DIFF_TK, DIFF_DH
    W = 2 * QB
    nt = (((1,), (1,)), ((), ()))
    qb = q_ref[0] * dd ** -0.5
    col = lax.broadcasted_iota(jnp.int32, (1, 2 * dd), 1)
    q = jnp.concatenate([jnp.where(col < dd, qb, 0.0), jnp.where(col >= dd, qb, 0.0)], axis=0).astype(bf16)
    lane = lax.broadcasted_iota(jnp.int32, (1, W), 1)
    qpos = i * QB + (lane & (QB - 1))
    m_scr[...] = jnp.full(m_scr.shape, -jnp.inf, f32)
    l_scr[...] = jnp.zeros(l_scr.shape, f32)
    acc_scr[...] = jnp.zeros(acc_scr.shape, f32)
    n_full = (i * QB) // TK

    def full_tile(kt, carry):
        k0 = pl.multiple_of(kt * TK, TK)
        s = lax.dot_general(k_ref[0, pl.ds(k0, TK), :], q, nt, preferred_element_type=f32)
        _online_softmax_tile(s, vt_ref[0, 0, :, pl.ds(k0, TK)], m_scr, l_scr, acc_scr)
        return carry

    lax.fori_loop(0, n_full, full_tile, 0)
    k0 = pl.multiple_of(n_full * TK, TK)
    s = lax.dot_general(k_ref[0, pl.ds(k0, TK), :], q, nt, preferred_element_type=f32)
    kpos = k0 + lax.broadcasted_iota(jnp.int32, (TK, 1), 0)
    _online_softmax_tile(jnp.where(kpos <= qpos, s, -jnp.inf), vt_ref[0, 0, :, pl.ds(k0, TK)], m_scr, l_scr, acc_scr)
    o = acc_scr[...] * (1.0 / jnp.maximum(l_scr[...], 1e-30))
    o = o[:, :QB] - lam_ref[0, 0] * o[:, QB:]
    o = o * lax.rsqrt(jnp.mean(o * o, axis=0, keepdims=True) + EPS) * (g_ref[...] * out_scale)
    o_ref[0] = o.T


def _diff_prompt_attention(q, k, v, lam, subln_g, out_scale):
    B, T, _ = q.shape
    H, dv = DIFF_HEADS, 2 * DIFF_DH
    assert T % DIFF_TK == 0 and DIFF_TK % DIFF_QB == 0
    bf16 = jnp.bfloat16
    vt = jnp.transpose(v.reshape(B, T, H, dv), (0, 2, 3, 1)).astype(bf16)
    W = 2 * DIFF_QB
    return pl.pallas_call(
        functools.partial(_diff_prompt_kernel, out_scale=out_scale),
        grid=(B, H, T // DIFF_QB),
        in_specs=[pl.BlockSpec(memory_space=pltpu.SMEM),
                  pl.BlockSpec((1, DIFF_QB, dv), lambda b, h, i: (b, i, h)),
                  pl.BlockSpec((1, T, dv), lambda b, h, i: (b, 0, h)),
                  pl.BlockSpec((1, 1, dv, T), lambda b, h, i: (b, h, 0, 0)),
                  pl.BlockSpec((dv, 1), lambda b, h, i: (0, 0))],
        out_specs=pl.BlockSpec((1, DIFF_QB, dv), lambda b, h, i: (b, i, h)),
        out_shape=jax.ShapeDtypeStruct(q.shape, jnp.float32),
        scratch_shapes=[pltpu.VMEM((1, W), jnp.float32), pltpu.VMEM((1, W), jnp.float32), pltpu.VMEM((dv, W), jnp.float32)],
        compiler_params=pltpu.CompilerParams(dimension_semantics=("parallel", "parallel", "arbitrary"),
                                             vmem_limit_bytes=VMEM_LIMIT),
        name="diff_prompt",
    )(lam.reshape(1, 1), q, k.astype(bf16), vt, subln_g.reshape(dv, 1))


def _diff_sample_kernel(pt_ref, lam_ref, *refs, n_pages, out_scale):
    k_pages, v_pages = refs[:n_pages], refs[n_pages:2 * n_pages]
    q_ref, kn_ref, vn_ref, g_ref, o_ref = refs[2 * n_pages:]
    f32, bf16 = jnp.float32, jnp.bfloat16
    H, dd = DIFF_HEADS, DIFF_DH
    T = q_ref.shape[1]
    D = H * 2 * dd
    qt = jnp.concatenate([q_ref[0] * dd ** -0.5] * (LANE // T), axis=0)
    rowi = lax.broadcasted_iota(jnp.int32, (LANE, D), 0)
    coli = lax.broadcasted_iota(jnp.int32, (LANE, D), 1)
    qbd = jnp.where(coli // dd == rowi // T, qt, 0.0).astype(bf16)
    dot = functools.partial(jnp.dot, preferred_element_type=f32)
    s = [dot(qbd, kp[...].astype(bf16)) for kp in k_pages]
    key = lax.broadcasted_iota(jnp.int32, (1, LANE), 1)
    qidx = lax.broadcasted_iota(jnp.int32, (LANE, 1), 0) & (T - 1)
    s_n = jnp.where((key < T) & (key <= qidx), dot(qbd, kn_ref[0].astype(bf16)), -jnp.inf)
    m = jnp.max(s_n, axis=1, keepdims=True)
    for sp in s:
        m = jnp.maximum(m, jnp.max(sp, axis=1, keepdims=True))
    p_n = jnp.exp(s_n - m)
    l = jnp.sum(p_n, axis=1, keepdims=True)
    acc = lax.dot_general(p_n.astype(bf16), vn_ref[0].astype(bf16), _NT, preferred_element_type=f32)
    for vp, sp in zip(v_pages, s):
        p = jnp.exp(sp - m)
        l = l + jnp.sum(p, axis=1, keepdims=True)
        acc = acc + lax.dot_general(p.astype(bf16), vp[...].astype(bf16), _NT, preferred_element_type=f32)
    o = acc * (1.0 / jnp.maximum(l, 1e-30))
    lam = lam_ref[0, 0]
    for h in range(H):
        cols = slice(h * 2 * dd, (h + 1) * 2 * dd)
        d = o[2 * h * T:(2 * h + 1) * T, cols] - lam * o[(2 * h + 1) * T:(2 * h + 2) * T, cols]
        o_ref[0, :, cols] = d * lax.rsqrt(jnp.mean(d * d, axis=1, keepdims=True) + EPS) * (g_ref[...] * out_scale)


def _diff_sample_attention(q, new_rows, cache, page_table, lam, subln_g, out_scale):
    B, T, D = q.shape
    n_pool, page = cache.shape[:2]
    n_pages = page_table.shape[1]
    assert T == 8 and 2 * DIFF_HEADS * T == LANE and page == LANE
    f32 = jnp.float32
    slabs = jnp.transpose(cache, (0, 2, 3, 4, 1)).reshape(n_pool, 2, D, page)
    new = jnp.pad(jnp.transpose(new_rows, (0, 2, 3, 4, 1)).reshape(B, 2, D, T), ((0, 0), (0, 0), (0, 0), (0, LANE - T)))
    slab = lambda p, kv: pl.BlockSpec((None, None, D, page), lambda b, pt: (pt[b, p], kv, 0, 0))
    new_slab = lambda kv: pl.BlockSpec((1, None, D, LANE), lambda b, pt: (b, kv, 0, 0))
    grid_spec = pltpu.PrefetchScalarGridSpec(
        num_scalar_prefetch=1,
        grid=(B,),
        in_specs=[pl.BlockSpec(memory_space=pltpu.SMEM)] + [slab(p, 0) for p in range(n_pages)]
                 + [slab(p, 1) for p in range(n_pages)]
                 + [pl.BlockSpec((1, T, D), lambda b, pt: (b, 0, 0)), new_slab(0), new_slab(1),
                    pl.BlockSpec((1, 2 * DIFF_DH), lambda b, pt: (0, 0))],
        out_specs=pl.BlockSpec((1, T, D), lambda b, pt: (b, 0, 0)),
    )
    return pl.pallas_call(
        functools.partial(_diff_sample_kernel, n_pages=n_pages, out_scale=out_scale),
        grid_spec=grid_spec,
        out_shape=jax.ShapeDtypeStruct(q.shape, f32),
        compiler_params=pltpu.CompilerParams(dimension_semantics=("parallel",), vmem_limit_bytes=56 * 1024 * 1024),
        name="diff_sample",
    )(page_table, lam.reshape(1, 1), *([slabs] * (2 * n_pages)), q, new, new, subln_g.reshape(1, 2 * DIFF_DH))


def diff_mixer(h, pos0, past, layer_idx, w_in, lq1, lk1, lq2, lk2, subln_g, w_out):
    B, T, _ = h.shape
    H, dd = DIFF_HEADS, DIFF_DH
    f32 = jnp.float32
    dt = h.dtype
    pos = pos0 + jnp.arange(T, dtype=jnp.int32)
    proj = _mm(h, w_in)
    q = rope(proj[..., :D_MODEL].reshape(B, T, 2 * H, dd), pos)
    k = rope(proj[..., D_MODEL:2 * D_MODEL].reshape(B, T, 2 * H, dd), pos)
    v = proj[..., 2 * D_MODEL:].reshape(B, T, 2 * H, dd)
    new_rows = jnp.stack([k, v], 2)
    lam_init = 0.8 - 0.6 * math.exp(-0.3 * layer_idx)
    lam = (jnp.exp(jnp.sum(lq1.astype(f32) * lk1.astype(f32))) - jnp.exp(jnp.sum(lq2.astype(f32) * lk2.astype(f32))) + lam_init)
    if past is None:
        o = _diff_prompt_attention(q.reshape(B, T, D_MODEL), k.reshape(B, T, D_MODEL), v.reshape(B, T, D_MODEL),
                                   lam, subln_g, 1.0 - lam_init)
        return _mm(o, w_out), new_rows
    cache, page_table = past
    o = _diff_sample_attention(q.reshape(B, T, D_MODEL), new_rows, cache, page_table, lam, subln_g, 1.0 - lam_init)
    return _mm(o, w_out), new_rows


PEER_TB = 512
PEER_ET = 1024
_PEER_CAND = [(a, b) for a in range(PEER_TOPK) for b in range(PEER_TOPK) if (a + 1) * (b + 1) <= PEER_TOPK]


def _extract_top(s, n_iter, on_pick, break_ties=True):
    rows = s.shape[0]
    iota = lax.broadcasted_iota(jnp.int32, s.shape, 0)
    for r in range(n_iter):
        m = jnp.max(s, axis=0, keepdims=True)
        if break_ties:
            idx = jnp.min(jnp.where(s == m, iota, rows), axis=0, keepdims=True)
            sel = iota == idx
        else:
            sel = s == m
        on_pick(r, m, sel)
        s = jnp.where(sel, -jnp.inf, s)


def _peer_route_kernel(x_ref, shift_ref, scale_ref, g_ref, wq_ref, k1_ref, k2_ref, aof_ref,
                       h_ref, c1_ref, cnt1_ref, rank2_ref, e2_ref, q_scr, v1_scr, v2_scr, cand_scr):
    f32 = jnp.float32
    x = x_ref[...]
    tb = x.shape[0] * x.shape[1]
    y = x * lax.rsqrt(jnp.mean(x * x, -1, keepdims=True) + EPS) * g_ref[...]
    h = (y * (1.0 + scale_ref[...]) + shift_ref[...]).reshape(tb, D_MODEL)
    hb = h.astype(jnp.bfloat16)
    h_ref[...] = hb
    q = jnp.dot(hb, wq_ref[...], preferred_element_type=f32).astype(jnp.bfloat16)
    for hh in range(PEER_HEADS):
        q_scr[hh] = q[:, hh * PEER_QDIM:(hh + 1) * PEER_QDIM]
    nt = (((1,), (1,)), ((), ()))
    n_chunk = tb // LANE
    cand_scr[...] = jnp.full(cand_scr.shape, -jnp.inf, f32)

    def process(it, exact):
        hh = it // n_chunk
        c0 = pl.multiple_of((it % n_chunk) * LANE, LANE)
        qc = q_scr[hh, pl.ds(c0, LANE), :]
        s1 = lax.dot_general(k1_ref[hh], qc, nt, preferred_element_type=f32)
        s2 = lax.dot_general(k2_ref[hh], qc, nt, preferred_element_type=f32)
        ranks = []
        for s, v_scr in ((s1, v1_scr), (s2, v2_scr)):
            rank = [jnp.full(s.shape, float(N_KEYS), f32)]

            def pick(r, m, sel, v_scr=v_scr, rank=rank):
                v_scr[r:r + 1, :] = m
                rank[0] = jnp.where(sel, float(r), rank[0])

            _extract_top(s, PEER_TOPK, pick, break_ties=exact)
            ranks.append(rank[0])
        for k, (a, b) in enumerate(_PEER_CAND):
            cand_scr[k:k + 1, :] = v1_scr[a:a + 1, :] + v2_scr[b:b + 1, :]
        top1 = v1_scr[0:1, :]
        top2 = v2_scr[0:1, :]
        top_val = top1 + top2
        a_of = aof_ref[...]
        iota16 = lax.broadcasted_iota(jnp.int32, (PEER_TOPK, LANE), 0)
        st = {"z": jnp.zeros((1, LANE), f32), "cnt": jnp.zeros((PEER_TOPK, LANE), f32), "taken": jnp.zeros((1, LANE), f32)}

        def pick_c(r, m, sel, st=st):
            st["z"] = st["z"] + jnp.exp(m - top_val)
            a_sel = jnp.max(jnp.where(sel, a_of, 0), axis=0, keepdims=True)
            st["cnt"] = st["cnt"] + (iota16 == a_sel).astype(f32)
            if not exact:
                st["taken"] = st["taken"] + jnp.sum(sel.astype(f32), axis=0, keepdims=True)

        _extract_top(cand_scr[...], PEER_TOPK, pick_c, break_ties=exact)
        inv_z = 1.0 / st["z"]
        cnt1 = jnp.zeros(s1.shape, f32)
        for a in range(PEER_TOPK):
            cnt1 = jnp.where(ranks[0] == float(a), st["cnt"][a:a + 1], cnt1)
        c1_ref[hh, :, pl.ds(c0, LANE)] = jnp.exp(s1 - top1) * inv_z
        cnt1_ref[hh, :, pl.ds(c0, LANE)] = cnt1
        rank2_ref[hh, :, pl.ds(c0, LANE)] = ranks[1]
        e2_ref[hh, :, pl.ds(c0, LANE)] = jnp.exp(s2 - top2)
        if exact:
            return None
        want = float(PEER_TOPK)
        tied = st["taken"] != want
        for rank in ranks:
            tied = tied | (jnp.sum((rank < want).astype(f32), axis=0, keepdims=True) != want)
        return tied

    def body(it, carry):
        tied = process(it, exact=False)

        @pl.when(jnp.max(tied.astype(f32)) > 0.0)
        def _():
            process(it, exact=True)

        return carry

    lax.fori_loop(0, PEER_HEADS * n_chunk, body, 0)


def _peer_dense_kernel(hb_ref, u_ref, vt_ref, c1_ref, cnt1_ref, rank2_ref, e2_ref, x_ref, gm_ref,
                       o_ref, ht0_ref, ht1_ref, gh0_ref, gh1_ref, acc_ref):
    f32, bf16 = jnp.float32, jnp.bfloat16
    s = pl.program_id(1)
    tb = hb_ref.shape[0]
    n_rows = PEER_ET // N_KEYS
    sub = 16
    nt = (((1,), (1,)), ((), ()))

    @pl.when(s == 0)
    def _():
        acc_ref[...] = jnp.zeros_like(acc_ref)
        ht1_ref[...] = jnp.zeros_like(ht1_ref)
        gh0_ref[...] = jnp.zeros_like(gh0_ref)
        gh1_ref[...] = jnp.zeros_like(gh1_ref)

    def stages(ht_w, ht_r, gh_w, gh_r):
        ht_w[...] = lax.dot_general(u_ref[...], hb_ref[...], nt, preferred_element_type=f32)
        for c0 in range(0, tb, LANE):
            lanes = slice(c0, c0 + LANE)
            for s0 in range(0, N_KEYS, sub):
                g = [jnp.zeros((sub, LANE), f32) for _ in range(n_rows)]
                for hh in range(PEER_HEADS):
                    rk = rank2_ref[hh, s0:s0 + sub, lanes]
                    ev = e2_ref[hh, s0:s0 + sub, lanes]
                    for r in range(n_rows):
                        g[r] = g[r] + jnp.where(rk < cnt1_ref[hh, r:r + 1, lanes], ev * c1_ref[hh, r:r + 1, lanes], 0.0)
                for r in range(n_rows):
                    rows = slice(r * N_KEYS + s0, r * N_KEYS + s0 + sub)
                    pre = ht_r[rows, lanes]
                    act = 0.5 * pre * (1.0 + lax.erf(pre * (2.0 ** -0.5)))
                    gh_w[rows, lanes] = (g[r] * act).astype(bf16)
        acc_ref[...] += jnp.dot(vt_ref[...], gh_r[...], preferred_element_type=f32)

    @pl.when(s % 2 == 0)
    def _():
        stages(ht0_ref, ht1_ref, gh1_ref, gh0_ref)

    @pl.when(s % 2 == 1)
    def _():
        stages(ht1_ref, ht0_ref, gh0_ref, gh1_ref)

    @pl.when(s == pl.num_programs(1) - 1)
    def _():
        upd = acc_ref[...].T.reshape(x_ref.shape)
        o_ref[...] = x_ref[...] + gm_ref[...] * upd


def _peer_sublayer(x, shift, scale, gate, norm_g, wq_b, k1p, k2p, u_b, vt_b):
    B, T, D = x.shape
    n = B * T
    tb = PEER_TB
    assert n % tb == 0
    if T % tb == 0:
        nbs, tper, per = 1, tb, T // tb
        xmap = lambda i, *_: (i // per, i % per, 0)
        mmap = lambda i, *_: (i // per, 0, 0)
    else:
        assert tb % T == 0 and T % 8 == 0
        nbs, tper = tb // T, T
        xmap = lambda i, *_: (i, 0, 0)
        mmap = lambda i, *_: (i, 0, 0)
    nblk = n // tb
    f32 = jnp.float32
    x_spec = pl.BlockSpec((nbs, tper, D), xmap)
    m_spec = pl.BlockSpec((nbs, 1, D), mmap)
    n_cand_pad = -(-len(_PEER_CAND) // 8) * 8
    a_of = jnp.asarray(np.broadcast_to(np.array([a for a, _ in _PEER_CAND] + [0] * (n_cand_pad - len(_PEER_CAND)),
                                                np.int32)[:, None], (n_cand_pad, LANE)))
    route_shape = lambda dt: jax.ShapeDtypeStruct((PEER_HEADS, N_KEYS, n), dt)
    route_spec = pl.BlockSpec((PEER_HEADS, N_KEYS, tb), lambda i: (0, 0, i))
    hb, c1, cnt1, rank2, e2 = pl.pallas_call(
        _peer_route_kernel,
        grid=(nblk,),
        in_specs=[x_spec, m_spec, m_spec,
                  pl.BlockSpec((1, D), lambda i: (0, 0)),
                  pl.BlockSpec((D, PEER_HEADS * PEER_QDIM), lambda i: (0, 0)),
                  pl.BlockSpec((PEER_HEADS, N_KEYS, PEER_QDIM), lambda i: (0, 0, 0)),
                  pl.BlockSpec((PEER_HEADS, N_KEYS, PEER_QDIM), lambda i: (0, 0, 0)),
                  pl.BlockSpec((n_cand_pad, LANE), lambda i: (0, 0))],
        out_specs=[pl.BlockSpec((tb, D), lambda i: (i, 0)), route_spec, route_spec, route_spec, route_spec],
        out_shape=[jax.ShapeDtypeStruct((n, D), jnp.bfloat16)] + [route_shape(f32)] * 4,
        scratch_shapes=[pltpu.VMEM((PEER_HEADS, tb, PEER_QDIM), jnp.bfloat16), pltpu.VMEM((PEER_TOPK, LANE), f32),
                        pltpu.VMEM((PEER_TOPK, LANE), f32), pltpu.VMEM((n_cand_pad, LANE), f32)],
        compiler_params=pltpu.CompilerParams(dimension_semantics=("parallel",), vmem_limit_bytes=VMEM_LIMIT),
        name="peer_route",
    )(x, shift, scale, norm_g.reshape(1, D), wq_b, k1p, k2p, a_of)

    rows = PEER_ET // N_KEYS
    n_tiles = N_EXPERTS // PEER_ET
    tile = lambda s, lag: jnp.clip(s - lag, 0, n_tiles - 1)
    sub_spec = pl.BlockSpec((PEER_HEADS, rows, tb), lambda i, s: (0, tile(s, 1), i))
    full_spec = pl.BlockSpec((PEER_HEADS, N_KEYS, tb), lambda i, s: (0, 0, i))
    return pl.pallas_call(
        _peer_dense_kernel,
        grid=(nblk, n_tiles + 2),
        in_specs=[pl.BlockSpec((tb, D), lambda i, s: (i, 0)),
                  pl.BlockSpec((PEER_ET, D), lambda i, s: (tile(s, 0), 0)),
                  pl.BlockSpec((D, PEER_ET), lambda i, s: (0, tile(s, 2))),
                  sub_spec, sub_spec, full_spec, full_spec,
                  pl.BlockSpec((nbs, tper, D), lambda i, s: xmap(i)),
                  pl.BlockSpec((nbs, 1, D), lambda i, s: mmap(i))],
        out_specs=pl.BlockSpec((nbs, tper, D), lambda i, s: xmap(i)),
        out_shape=jax.ShapeDtypeStruct(x.shape, x.dtype),
        scratch_shapes=[pltpu.VMEM((PEER_ET, tb), f32), pltpu.VMEM((PEER_ET, tb), f32),
                        pltpu.VMEM((PEER_ET, tb), jnp.bfloat16), pltpu.VMEM((PEER_ET, tb), jnp.bfloat16),
                        pltpu.VMEM((D, tb), f32)],
        compiler_params=pltpu.CompilerParams(dimension_semantics=("parallel", "arbitrary"), vmem_limit_bytes=VMEM_LIMIT),
        name="peer_dense",
    )(hb, u_b, vt_b, c1, cnt1, rank2, e2, x, gate)


def _peer_weights(w_q, k1, k2, u_tab, v_tab):
    bf16 = jnp.bfloat16
    half = PEER_QDIM // 2
    k1p = jnp.pad(k1, ((0, 0), (0, 0), (0, half))).astype(bf16)
    k2p = jnp.pad(k2, ((0, 0), (0, 0), (half, 0))).astype(bf16)
    return w_q.astype(bf16), k1p, k2p, u_tab.astype(bf16), v_tab.astype(bf16).T


def _layer_pool(cache, j, page_table):
    return cache.reshape((-1,) + cache.shape[2:]), page_table + j * cache.shape[1]


def kernel(x_prompt, x_sample, cache_nsa_kv, cache_diff_kv, state_nsa_window, state_gdn_S, state_gdn_conv,
           page_table, c_prompt, c_sample, ada_w, ada_b, norm_mix_g, norm_ffn_g, final_norm_g,
           gdn_w_in, gdn_conv_w, gdn_a_log, gdn_dt_bias, gdn_norm_g, gdn_w_out,
           nsa_w_in, nsa_cmp_pos, nsa_cmp_w, nsa_w_out,
           diff_w_in, diff_lq1, diff_lk1, diff_lq2, diff_lk2, diff_subln_g, diff_w_out,
           peer_w_q, peer_k1, peer_k2, peer_u, peer_v):
    past_len = page_table.shape[1] * cache_nsa_kv.shape[2]
    peer_w = [_peer_weights(peer_w_q[i], peer_k1[i], peer_k2[i], peer_u[i], peer_v[i]) for i in range(DEPTH)]

    def trunk(x, c, sample):
        B, T, _ = x.shape
        pos0 = past_len if sample else 0
        cs = jax.nn.silu(c)
        new_S, new_conv, new_nsa_kv, new_nsa_win, new_diff_kv = [], [], [], [], []
        for i in range(DEPTH):
            mod = (cs @ ada_w[i] + ada_b[i]).reshape(B, 6, 1, D_MODEL)
            h = rmsnorm(x, norm_mix_g[i]) * (1.0 + mod[:, 1]) + mod[:, 0]
            j = i // N_MIXERS
            kind = i % N_MIXERS
            if kind == 0:
                if sample:
                    S0 = state_gdn_S[j].astype(jnp.float32)
                    buf = state_gdn_conv[j]
                else:
                    S0 = jnp.zeros((B, GDN_HEADS, GDN_DK, GDN_DV), jnp.float32)
                    buf = jnp.zeros((B, CONV_W - 1, GDN_HEADS * (2 * GDN_DK + GDN_DV)), x.dtype)
                m, buf_n, S_n = gdn_mixer(x, mod[:, 0], mod[:, 1], norm_mix_g[i], buf, S0, gdn_w_in[j], gdn_conv_w[j],
                                          gdn_a_log[j], gdn_dt_bias[j], gdn_norm_g[j], gdn_w_out[j])
                new_S.append(S_n.astype(x.dtype))
                new_conv.append(buf_n)
            elif kind == 1:
                past = _layer_pool(cache_nsa_kv, j, page_table) if sample else None
                wbuf = state_nsa_window[j] if sample else None
                m, kv_n, win_n = nsa_mixer(h, pos0, past, wbuf, nsa_w_in[j], nsa_cmp_pos[j], nsa_cmp_w[j], nsa_w_out[j])
                new_nsa_kv.append(kv_n)
                new_nsa_win.append(win_n)
            else:
                past = _layer_pool(cache_diff_kv, j, page_table) if sample else None
                m, kv_n = diff_mixer(h, pos0, past, i, diff_w_in[j], diff_lq1[j], diff_lk1[j], diff_lq2[j], diff_lk2[j], diff_subln_g[j], diff_w_out[j])
                new_diff_kv.append(kv_n)
            x = x + mod[:, 2] * m
            x = _peer_sublayer(x, mod[:, 3], mod[:, 4], mod[:, 5], norm_ffn_g[i], *peer_w[i])
        y = rmsnorm(x, final_norm_g)
        return y, jnp.stack(new_S), jnp.stack(new_conv), jnp.stack(new_nsa_kv), jnp.stack(new_nsa_win), jnp.stack(new_diff_kv)

    y_prompt, p_gdn_S, p_gdn_conv, p_nsa_kv, p_nsa_win, p_diff_kv = trunk(x_prompt, c_prompt, False)
    y_sample, s_gdn_S, s_gdn_conv, s_nsa_kv, s_nsa_win, s_diff_kv = trunk(x_sample, c_sample, True)
    return (y_prompt, y_sample, p_gdn_S, p_gdn_conv, p_nsa_kv, p_nsa_win, p_diff_kv, s_gdn_S, s_gdn_conv, s_nsa_kv, s_nsa_win, s_diff_kv)
```
